```python
import jax, jax.numpy as jnp
from jax import lax
import numpy as np

D_MODEL = 2048
BATCH = 2
SEQ = 8192
DEPTH = 1

EPS = 1e-6
CHUNK = 64

A_HEADS = 8
A_DK = 128
A_DV = 128
A_WIDTH = A_HEADS * A_DV

B_QK_HEADS = 4
B_V_HEADS = 8
B_DK = 128
B_DV = 128
B_WIDTH = B_V_HEADS * B_DV
CONV = 4
B_CONV_CH = 2 * B_QK_HEADS * B_DK + B_WIDTH

MIX_WIDTH = A_WIDTH + B_WIDTH

IN_SIZES = (A_HEADS * A_DK, A_HEADS * A_DK, A_WIDTH, A_WIDTH,
            B_QK_HEADS * B_DK, B_QK_HEADS * B_DK, B_WIDTH, B_WIDTH,
            B_V_HEADS, B_V_HEADS)
IN_WIDTH = 4 * A_HEADS * A_DK + 2 * B_QK_HEADS * B_DK + 2 * B_WIDTH + 2 * B_V_HEADS

P_HEADS = 8
N_KEYS = 128
N_EXPERTS = N_KEYS * N_KEYS
P_DKEY = 256
P_TOPK = 16
P_BLOCK = 128

kernel_name = "hymba_hgrn2_gdn_peer"


def _rmsnorm(x, g):
    xf = x.astype(jnp.float32)
    y = xf * lax.rsqrt(jnp.mean(xf * xf, axis=-1, keepdims=True) + EPS)
    return (y * g.astype(jnp.float32)).astype(x.dtype)


def _gated_rmsnorm(o, z, g):
    of = o.astype(jnp.float32)
    y = of * lax.rsqrt(jnp.mean(of * of, axis=-1, keepdims=True) + EPS)
    return y * g.astype(jnp.float32) * jax.nn.silu(z.astype(jnp.float32))


def _l2norm(x):
    xf = x.astype(jnp.float32)
    return xf * lax.rsqrt(jnp.sum(xf * xf, axis=-1, keepdims=True) + EPS)


def _chunk(t):
    b, s, h = t.shape[:3]
    rest = t.shape[3:]
    t = t.reshape((b, s // CHUNK, CHUNK, h) + rest)
    return t.transpose((1, 0, 3, 2) + tuple(range(4, t.ndim)))


def _unchunk(t):
    n, b, h, c, d = t.shape
    return t.transpose(1, 0, 3, 2, 4).reshape(b, n * c, h, d)


def _causal_conv_silu(x, w):
    s = x.shape[1]
    xp = jnp.pad(x, ((0, 0), (CONV - 1, 0), (0, 0)))
    y = xp[:, 0:s, :] * w[0]
    for j in range(1, CONV):
        y = y + xp[:, j:j + s, :] * w[j]
    return jax.nn.silu(y)


def _hgrn2(q, f_logit, v, lb):
    q = q.astype(jnp.float32)
    z = f_logit.astype(jnp.float32)
    v = v.astype(jnp.float32)
    lb = lb.astype(jnp.float32)
    log_f = jnp.log(lb + (1.0 - lb) * jax.nn.sigmoid(z))
    k = (1.0 - lb) * jax.nn.sigmoid(-z)
    qc, kc, vc = _chunk(q), _chunk(k), _chunk(v)
    bc = jnp.cumsum(_chunk(log_f), axis=-2)
    causal = jnp.tril(jnp.ones((CHUNK, CHUNK), dtype=bool))

    def step(S, inp):
        q_c, k_c, v_c, b_c = inp
        diff = b_c[..., :, None, :] - b_c[..., None, :, :]
        decay = jnp.exp(jnp.where(causal[:, :, None], diff, -jnp.inf))
        att = jnp.einsum('bhtd,bhsd,bhtsd->bhts', q_c, k_c, decay)
        o = (jnp.einsum('bhtd,bhdv->bhtv', q_c * jnp.exp(b_c), S)
             + jnp.einsum('bhts,bhsv->bhtv', att, v_c))
        b_last = b_c[..., -1:, :]
        S = (S * jnp.exp(b_last[..., 0, :, None])
             + jnp.einsum('bhsd,bhsv->bhdv', k_c * jnp.exp(b_last - b_c), v_c))
        return S, o

    S0 = jnp.zeros((q.shape[0], q.shape[2], A_DK, A_DV), jnp.float32)
    _, o = lax.scan(step, S0, (qc, kc, vc, bc))
    return _unchunk(o)


def _gated_delta(q, k, v, beta, g):
    q = _chunk(q.astype(jnp.float32) * (B_DK ** -0.5))
    k = _chunk(k.astype(jnp.float32))
    v = _chunk(v.astype(jnp.float32))
    beta = _chunk(beta.astype(jnp.float32))
    gc = jnp.cumsum(_chunk(g.astype(jnp.float32)), axis=-1)
    incl = jnp.tril(jnp.ones((CHUNK, CHUNK), dtype=bool))
    strict = jnp.tril(jnp.ones((CHUNK, CHUNK), dtype=bool), k=-1)
    L = jnp.exp(jnp.where(incl, gc[..., :, None] - gc[..., None, :], -jnp.inf))
    kb = k * beta[..., None]
    M = jnp.eye(CHUNK, dtype=jnp.float32) + jnp.where(
        strict, jnp.einsum('...td,...sd->...ts', kb, k) * L, 0.0)
    u = lax.linalg.triangular_solve(M, v * beta[..., None], left_side=True,
                                    lower=True, unit_diagonal=True)
    w = lax.linalg.triangular_solve(M, kb * jnp.exp(gc)[..., None], left_side=True,
                                    lower=True, unit_diagonal=True)
    a_qk = jnp.einsum('...td,...sd->...ts', q, k) * L
    qg = q * jnp.exp(gc)[..., None]
    g_last = gc[..., -1:]
    kd = k * jnp.exp(g_last - gc)[..., None]
    dl = jnp.exp(g_last)[..., None]

    def step(S, inp):
        qg_c, kd_c, u_c, w_c, a_c, dl_c = inp
        v_new = u_c - jnp.einsum('bhtd,bhdv->bhtv', w_c, S)
        o = (jnp.einsum('bhtd,bhdv->bhtv', qg_c, S)
             + jnp.einsum('bhts,bhsv->bhtv', a_c, v_new))
        S = S * dl_c + jnp.einsum('bhsd,bhsv->bhdv', kd_c, v_new)
        return S, o

    S0 = jnp.zeros((qg.shape[1], qg.shape[2], B_DK, B_DV), jnp.float32)
    _, o = lax.scan(step, S0, (qg, kd, u, w, a_qk, dl))
    return _unchunk(o)


def _peer(xn, w_query, sub_keys, u_tab, v_tab):
    b, s, d = xn.shape
    t = b * s
    xt = xn.reshape(t, d)
    qry = jnp.einsum('td,dhk->thk', xt, w_query).astype(jnp.float32)
    half = P_DKEY // 2
    s1 = jnp.einsum('thk,hnk->thn', qry[..., :half], sub_keys[0].astype(jnp.float32))
    s2 = jnp.einsum('thk,hnk->thn', qry[..., half:], sub_keys[1].astype(jnp.float32))
    v1, i1 = lax.top_k(s1, P_TOPK)
    v2, i2 = lax.top_k(s2, P_TOPK)
    cand = (v1[..., :, None] + v2[..., None, :]).reshape(t, P_HEADS, P_TOPK * P_TOPK)
    best, pos = lax.top_k(cand, P_TOPK)
    expert = (jnp.take_along_axis(i1, pos // P_TOPK, axis=-1) * N_KEYS
              + jnp.take_along_axis(i2, pos % P_TOPK, axis=-1))
    gate = jax.nn.softmax(best, axis=-1)
    nblk = t // P_BLOCK

    def blk(args):
        xb, eb, gb = args
        hid = jax.nn.gelu(jnp.einsum('pd,phkd->phk', xb, u_tab[eb]).astype(jnp.float32),
                          approximate=False)
        return jnp.einsum('phk,phkd->pd', (gb * hid).astype(xb.dtype), v_tab[eb])

    y = lax.map(blk, (xt.reshape(nblk, P_BLOCK, d),
                      expert.reshape(nblk, P_BLOCK, P_HEADS, P_TOPK),
                      gate.reshape(nblk, P_BLOCK, P_HEADS, P_TOPK)))
    return y.reshape(b, s, d)


def setup_inputs(seed: int = 0) -> dict:
    key = jax.random.key(seed)
    ks = jax.random.split(key, 20)
    f32 = jnp.float32
    nrm = lambda k, shape, scale: jax.random.normal(k, shape, f32) * scale
    gain = lambda k, shape: 1.0 + 0.02 * jax.random.normal(k, shape, f32)
    dt = jnp.exp(jax.random.uniform(ks[8], (DEPTH, B_V_HEADS), f32,
                                    minval=np.log(1e-3), maxval=np.log(1e-1)))
    return {
        "x": nrm(ks[0], (BATCH, SEQ, D_MODEL), 1.0),
        "attn_norm_g": gain(ks[1], (DEPTH, D_MODEL)),
        "w_in": nrm(ks[2], (DEPTH, D_MODEL, IN_WIDTH), D_MODEL ** -0.5),
        "hgrn_lb_logits": nrm(ks[3], (DEPTH + 1, A_HEADS, A_DK), 0.1),
        "hgrn_norm_g": gain(ks[4], (DEPTH, A_DV)),
        "gdn_conv_w": nrm(ks[5], (DEPTH, CONV, B_CONV_CH), CONV ** -0.5),
        "gdn_A_log": jnp.log(jax.random.uniform(ks[6], (DEPTH, B_V_HEADS), f32,
                                                minval=1.0, maxval=16.0)),
        "gdn_dt_bias": dt + jnp.log(-jnp.expm1(-dt)),
        "gdn_norm_g": gain(ks[7], (DEPTH, B_DV)),
        "w_out": nrm(ks[9], (DEPTH, MIX_WIDTH, D_MODEL), MIX_WIDTH ** -0.5),
        "ffn_norm_g": gain(ks[10], (DEPTH, D_MODEL)),
        "peer_w_query": nrm(ks[11], (DEPTH, D_MODEL, P_HEADS, P_DKEY), D_MODEL ** -0.5),
        "peer_sub_keys": nrm(ks[12], (DEPTH, 2, P_HEADS, N_KEYS, P_DKEY // 2),
                             (P_DKEY // 2) ** -0.5),
        "peer_u": nrm(ks[13], (DEPTH, N_EXPERTS, D_MODEL), D_MODEL ** -0.5),
        "peer_v": nrm(ks[14], (DEPTH, N_EXPERTS, D_MODEL), P_HEADS ** -0.5),
        "final_norm_g": gain(ks[15], (D_MODEL,)),
    }


def reference(x, attn_norm_g, w_in, hgrn_lb_logits, hgrn_norm_g, gdn_conv_w, gdn_A_log,
              gdn_dt_bias, gdn_norm_g, w_out, ffn_norm_g, peer_w_query, peer_sub_keys,
              peer_u, peer_v, final_norm_g):
    b, s, _ = x.shape
    lb_all = jnp.cumsum(jax.nn.softmax(hgrn_lb_logits.astype(jnp.float32), axis=0), axis=0)
    offs = []
    acc = 0
    for n in IN_SIZES[:-1]:
        acc += n
        offs.append(acc)
    rep = B_V_HEADS // B_QK_HEADS
    h = x
    for l in range(DEPTH):
        xn = _rmsnorm(h, attn_norm_g[l])
        proj = xn @ w_in[l]
        a_q, a_f, a_i, a_g, b_q, b_k, b_v, b_z, b_b, b_a = jnp.split(proj, offs, axis=-1)
        o_a = _hgrn2(a_q.reshape(b, s, A_HEADS, A_DK), a_f.reshape(b, s, A_HEADS, A_DK),
                     a_i.reshape(b, s, A_HEADS, A_DV), lb_all[l])
        o_a = _gated_rmsnorm(o_a, a_g.reshape(b, s, A_HEADS, A_DV), hgrn_norm_g[l])
        qkv = _causal_conv_silu(jnp.concatenate([b_q, b_k, b_v], axis=-1), gdn_conv_w[l])
        nqk = B_QK_HEADS * B_DK
        gq = _l2norm(qkv[..., :nqk].reshape(b, s, B_QK_HEADS, B_DK))
        gk = _l2norm(qkv[..., nqk:2 * nqk].reshape(b, s, B_QK_HEADS, B_DK))
        gv = qkv[..., 2 * nqk:].reshape(b, s, B_V_HEADS, B_DV)
        gq = jnp.repeat(gq, rep, axis=2)
        gk = jnp.repeat(gk, rep, axis=2)
        beta = jax.nn.sigmoid(b_b.astype(jnp.float32))
        g = -jnp.exp(gdn_A_log[l].astype(jnp.float32)) * jax.nn.softplus(
            b_a.astype(jnp.float32) + gdn_dt_bias[l].astype(jnp.float32))
        o_b = _gated_delta(gq, gk, gv, beta, g)
        o_b = _gated_rmsnorm(o_b, b_z.reshape(b, s, B_V_HEADS, B_DV), gdn_norm_g[l])
        mix = jnp.concatenate([o_a.reshape(b, s, A_WIDTH), o_b.reshape(b, s, B_WIDTH)],
                              axis=-1).astype(x.dtype)
        h = h + mix @ w_out[l]
        hn = _rmsnorm(h, ffn_norm_g[l])
        h = h + _peer(hn, peer_w_query[l], peer_sub_keys[l], peer_u[l], peer_v[l])
    return _rmsnorm(h, final_norm_g)
```

```python
import functools

import jax
import jax.numpy as jnp
from jax import lax
from jax.experimental import pallas as pl
from jax.experimental.pallas import tpu as pltpu

F32 = jnp.float32
BF16 = jnp.bfloat16

EPS = 1e-6
CHUNK = 64
SUB = 16
HD = 128
A_HEADS = 8
B_QK_HEADS = 4
B_V_HEADS = 8
CONV = 4
P_HEADS = 8
N_KEYS = 128
P_TOPK = 16
VMEM_LIMIT = 56 * 1024 * 1024

HIGHEST = lax.Precision.HIGHEST


def _nt(a, b):
    return lax.dot_general(a, b, (((1,), (1,)), ((), ())), preferred_element_type=F32)


def _tn(a, b):
    return lax.dot_general(a, b, (((0,), (0,)), ((), ())), preferred_element_type=F32)


def _mm(a, b):
    return jnp.dot(a, b, preferred_element_type=F32)


def _hp(a, b):
    return jnp.dot(a, b, precision=HIGHEST, preferred_element_type=F32)


def _sigmoid(x):
    return 1.0 / (1.0 + jnp.exp(-x))


def _softplus(x):
    return jnp.maximum(x, 0.0) + jnp.log1p(jnp.exp(-jnp.abs(x)))


def _bcast_rows(x, idx):
    n = x.shape[1]
    return jnp.concatenate(
        [jnp.broadcast_to(x[r:r + 1, :], (SUB, n)) for r in idx], axis=0)


def _inproj_body(x_ref, g_ref, w_ref, o_ref):
    x = x_ref[...]
    ms = jnp.mean(x * x, axis=-1, keepdims=True)
    xn = (x * lax.rsqrt(ms + EPS) * g_ref[...]).astype(BF16)
    o_ref[...] = _mm(xn, w_ref[...])


def _in_proj(x2, g, w, tm, tn):
    t, d = x2.shape
    n = w.shape[1]
    return pl.pallas_call(
        _inproj_body,
        grid=(n // tn, t // tm),
        in_specs=[
            pl.BlockSpec((tm, d), lambda j, i: (i, 0)),
            pl.BlockSpec((1, d), lambda j, i: (0, 0)),
            pl.BlockSpec((d, tn), lambda j, i: (0, j)),
        ],
        out_specs=pl.BlockSpec((tm, tn), lambda j, i: (i, j)),
        out_shape=jax.ShapeDtypeStruct((t, n), F32),
        compiler_params=pltpu.CompilerParams(
            dimension_semantics=("parallel", "parallel"),
            vmem_limit_bytes=VMEM_LIMIT),
        name="in_proj",
    )(x2, g, w)


def _hgrn_body(q_ref, f_ref, i_ref, g_ref, lb_ref, gn_ref, o_ref, st_ref, *, ts):
    @pl.when(pl.program_id(1) == 0)
    def _():
        st_ref[...] = jnp.zeros_like(st_ref)

    row = lax.broadcasted_iota(jnp.int32, (CHUNK, CHUNK), 0)
    col = lax.broadcasted_iota(jnp.int32, (CHUNK, CHUNK), 1)
    tril_f = (col <= row).astype(F32)
    blk_r = row // SUB
    blk_c = col // SUB
    rblk = lax.broadcasted_iota(jnp.int32, (CHUNK, HD), 0) // SUB
    nsub = CHUNK // SUB
    gn = gn_ref[...]

    def chunk_body(c, carry):
        r0 = pl.multiple_of(c * CHUNK, CHUNK)
        rows = pl.ds(r0, CHUNK)
        for h in range(A_HEADS):
            cs = slice(h * HD, (h + 1) * HD)
            q = q_ref[0, rows, cs]
            z = f_ref[0, rows, cs]
            v = i_ref[0, rows, cs]
            gate = g_ref[0, rows, cs]
            lb = lb_ref[h:h + 1, :]
            logf = jnp.log(lb + (1.0 - lb) * _sigmoid(z))
            k = (1.0 - lb) * _sigmoid(-z)
            b = _hp(tril_f, logf)
            b_last = b[CHUNK - 1:CHUNK, :]
            st = st_ref[h]
            o = _nt((q * jnp.exp(b)).astype(BF16), st.astype(BF16))
            bnd = [b[SUB * j + SUB - 1:SUB * j + SUB, :] for j in range(nsub)]
            k_off = k * jnp.exp(_bcast_rows(b, [SUB * j + SUB - 1 for j in range(nsub)]) - b)
            qcat = jnp.concatenate(
                [q * jnp.exp(jnp.minimum(b - bnd[j], 0.0)) for j in range(nsub - 1)], axis=1)
            kcat = jnp.concatenate(
                [jnp.where(rblk == j, k_off, 0.0) for j in range(nsub - 1)], axis=1)
            att_off = _nt(qcat.astype(BF16), kcat.astype(BF16))
            ref_rows = _bcast_rows(b, [SUB * i for i in range(nsub)])
            q_d = q * jnp.exp(b - ref_rows)
            k_d = k * jnp.exp(ref_rows - b)
            att_d = _nt(q_d.astype(BF16), k_d.astype(BF16))
            att = jnp.where(blk_r > blk_c, att_off,
                            jnp.where((blk_r == blk_c) & (col <= row), att_d, 0.0))
            o = o + _mm(att.astype(BF16), v.astype(BF16))
            kdec = k * jnp.exp(b_last - b)
            st_ref[h] = st * jnp.exp(b_last) + _tn(v.astype(BF16), kdec.astype(BF16))
            y = o * lax.rsqrt(jnp.mean(o * o, axis=-1, keepdims=True) + EPS)
            y = y * gn * (gate * _sigmoid(gate))
            o_ref[0, rows, cs] = y.astype(o_ref.dtype)
        return carry

    lax.fori_loop(0, ts // CHUNK, chunk_body, 0)


def _hgrn(proj3, lb, gn, ts):
    b, s, _ = proj3.shape
    w = A_HEADS * HD
    return pl.pallas_call(
        functools.partial(_hgrn_body, ts=ts),
        grid=(b, s // ts),
        in_specs=[
            pl.BlockSpec((1, ts, w), lambda bi, ti: (bi, ti, 0)),
            pl.BlockSpec((1, ts, w), lambda bi, ti: (bi, ti, 1)),
            pl.BlockSpec((1, ts, w), lambda bi, ti: (bi, ti, 2)),
            pl.BlockSpec((1, ts, w), lambda bi, ti: (bi, ti, 3)),
            pl.BlockSpec((A_HEADS, HD), lambda bi, ti: (0, 0)),
            pl.BlockSpec((1, HD), lambda bi, ti: (0, 0)),
        ],
        out_specs=pl.BlockSpec((1, ts, w), lambda bi, ti: (bi, ti, 0)),
        out_shape=jax.ShapeDtypeStruct((b, s, w), BF16),
        scratch_shapes=[pltpu.VMEM((A_HEADS, HD, HD), F32)],
        compiler_params=pltpu.CompilerParams(
            dimension_semantics=("parallel", "arbitrary"),
            vmem_limit_bytes=VMEM_LIMIT),
        name="hgrn",
    )(proj3, proj3, proj3, proj3, lb, gn)


def _gdn_body(bq_ref, bk_ref, bv_ref, bz_ref, sm_ref, smt_ref, cw_ref, pcol_ref, prow_ref,
              gn_ref, o_ref, s_ref, tail_ref, qn_ref, kn_ref, vv_ref, *, ts):
    nqk = B_QK_HEADS * HD

    @pl.when(pl.program_id(1) == 0)
    def _():
        s_ref[...] = jnp.zeros_like(s_ref)
        tail_ref[...] = jnp.zeros_like(tail_ref)

    def conv_silu(x, c0):
        n = x.shape[1]
        xe = jnp.concatenate([tail_ref[:, c0:c0 + n], x], axis=0)
        w = cw_ref[:, c0:c0 + n]
        y = x * w[CONV - 1:CONV, :]
        for j in range(1, CONV):
            y = y + pltpu.roll(xe, j, axis=0)[8:, :] * w[CONV - 1 - j:CONV - j, :]
        tail_ref[:, c0:c0 + n] = x[ts - 8:, :]
        return y * _sigmoid(y)

    def l2n(x):
        return x * lax.rsqrt(jnp.sum(x * x, axis=-1, keepdims=True) + EPS)

    yq = conv_silu(bq_ref[0], 0)
    yk = conv_silu(bk_ref[0], nqk)
    vv_ref[...] = conv_silu(bv_ref[0], 2 * nqk)
    for h in range(B_QK_HEADS):
        cs = slice(h * HD, (h + 1) * HD)
        qn_ref[:, cs] = l2n(yq[:, cs]) * (HD ** -0.5)
        kn_ref[:, cs] = l2n(yk[:, cs])

    row = lax.broadcasted_iota(jnp.int32, (CHUNK, CHUNK), 0)
    col = lax.broadcasted_iota(jnp.int32, (CHUNK, CHUNK), 1)
    incl = col <= row
    strict = col < row
    tril_f = incl.astype(F32)
    triu_f = (row <= col).astype(F32)
    same_blk = (row // SUB) == (col // SUB)
    eye = (row == col).astype(F32)
    gn = gn_ref[...]
    a_col = pcol_ref[0:1, :]
    dt_col = pcol_ref[1:2, :]
    a_row = prow_ref[:, 0:1]
    dt_row = prow_ref[:, 1:2]
    rep = B_V_HEADS // B_QK_HEADS

    def chunk_body(c, carry):
        r0 = pl.multiple_of(c * CHUNK, CHUNK)
        rows = pl.ds(r0, CHUNK)
        sm = sm_ref[0, rows, :]
        beta_c = _sigmoid(sm)
        gc_c = _hp(tril_f, -a_col * _softplus(sm + dt_col))
        smt = smt_ref[0, c]
        gc_r = _hp(-a_row * _softplus(smt + dt_row), triu_f)
        for h in range(B_V_HEADS):
            qk = h // rep
            qq = qn_ref[rows, qk * HD:(qk + 1) * HD]
            kk = kn_ref[rows, qk * HD:(qk + 1) * HD]
            vs = vv_ref[rows, h * HD:(h + 1) * HD]
            zz = bz_ref[0, rows, h * HD:(h + 1) * HD]
            bcol = beta_c[:, h:h + 1]
            gcol = gc_c[:, B_V_HEADS + h:B_V_HEADS + h + 1]
            grow = gc_r[B_V_HEADS + h:B_V_HEADS + h + 1, :]
            decay = jnp.where(incl, jnp.exp(jnp.minimum(gcol - grow, 0.0)), 0.0)
            kb = kk * bcol
            kk16 = kk.astype(BF16)
            a = jnp.where(strict, _nt(kb.astype(BF16), kk16) * decay, 0.0)
            dg = jnp.where(same_blk, a, 0.0)
            off = a - dg
            tinv = eye - dg
            p = dg
            for _ in range(3):
                p = _hp(p, p)
                tinv = tinv + _hp(tinv, p)
            pm = _hp(tinv, off)
            pm2 = _hp(pm, pm)
            egc = jnp.exp(gcol)
            rhs = jnp.concatenate([vs * bcol, kb * egc], axis=1)
            x = _hp(tinv, rhs)
            x = x + _hp(pm2, x)
            x = x - _hp(pm, x)
            u = x[:, :HD]
            w = x[:, HD:]
            a_qk = jnp.where(incl, _nt(qq.astype(BF16), kk16) * decay, 0.0)
            g_last = gcol[CHUNK - 1:CHUNK, :]
            st = s_ref[h]
            st16 = st.astype(BF16)
            v_new = u - _mm(w.astype(BF16), st16)
            o = _mm((qq * egc).astype(BF16), st16) + _mm(a_qk.astype(BF16), v_new.astype(BF16))
            kd = kk * jnp.exp(g_last - gcol)
            s_ref[h] = st * jnp.exp(g_last) + _tn(kd.astype(BF16), v_new.astype(BF16))
            y = o * lax.rsqrt(jnp.mean(o * o, axis=-1, keepdims=True) + EPS)
            y = y * gn * (zz * _sigmoid(zz))
            o_ref[0, rows, h * HD:(h + 1) * HD] = y.astype(o_ref.dtype)
        return carry

    lax.fori_loop(0, ts // CHUNK, chunk_body, 0)


def _gdn(proj3, smt, conv_w, pcol, prow, gn, ts):
    b, s, _ = proj3.shape
    nqk = B_QK_HEADS * HD
    wv = B_V_HEADS * HD
    off_q = (4 * A_HEADS * HD) // nqk
    off_v = (4 * A_HEADS * HD + 2 * nqk) // wv
    off_s = (4 * A_HEADS * HD + 2 * nqk + 2 * wv) // HD
    nch = ts // CHUNK
    return pl.pallas_call(
        functools.partial(_gdn_body, ts=ts),
        grid=(b, s // ts),
        in_specs=[
            pl.BlockSpec((1, ts, nqk), lambda bi, ti: (bi, ti, off_q)),
            pl.BlockSpec((1, ts, nqk), lambda bi, ti: (bi, ti, off_q + 1)),
            pl.BlockSpec((1, ts, wv), lambda bi, ti: (bi, ti, off_v)),
            pl.BlockSpec((1, ts, wv), lambda bi, ti: (bi, ti, off_v + 1)),
            pl.BlockSpec((1, ts, HD), lambda bi, ti: (bi, ti, off_s)),
            pl.BlockSpec((1, nch, 2 * B_V_HEADS, CHUNK), lambda bi, ti: (bi, ti, 0, 0)),
            pl.BlockSpec((CONV, 2 * nqk + wv), lambda bi, ti: (0, 0)),
            pl.BlockSpec((2, HD), lambda bi, ti: (0, 0)),
            pl.BlockSpec((2 * B_V_HEADS, HD), lambda bi, ti: (0, 0)),
            pl.BlockSpec((1, HD), lambda bi, ti: (0, 0)),
        ],
        out_specs=pl.BlockSpec((1, ts, wv), lambda bi, ti: (bi, ti, 0)),
        out_shape=jax.ShapeDtypeStruct((b, s, wv), BF16),
        scratch_shapes=[
            pltpu.VMEM((B_V_HEADS, HD, HD), F32),
            pltpu.VMEM((8, 2 * nqk + wv), F32),
            pltpu.VMEM((ts, nqk), F32),
            pltpu.VMEM((ts, nqk), F32),
            pltpu.VMEM((ts, wv), F32),
        ],
        compiler_params=pltpu.CompilerParams(
            dimension_semantics=("parallel", "arbitrary"),
            vmem_limit_bytes=VMEM_LIMIT),
        name="gdn",
    )(proj3, proj3, proj3, proj3, proj3, smt, conv_w, pcol, prow, gn)


def _outproj_body(x_ref, oa_ref, ob_ref, wo_ref, fg_ref, wq_ref, k1_ref, k2_ref,
                  h_ref, hn_ref, s1_ref, s2_ref):
    wa = oa_ref.shape[1]
    h = x_ref[...] + _mm(oa_ref[...], wo_ref[:wa, :]) + _mm(ob_ref[...], wo_ref[wa:, :])
    h_ref[...] = h
    hn = h * lax.rsqrt(jnp.mean(h * h, axis=-1, keepdims=True) + EPS) * fg_ref[...]
    hn_ref[...] = hn
    qry = _mm(hn.astype(BF16), wq_ref[...])
    for hh in range(P_HEADS):
        q1 = qry[:, hh * 2 * HD:hh * 2 * HD + HD].astype(BF16)
        q2 = qry[:, hh * 2 * HD + HD:(hh + 1) * 2 * HD].astype(BF16)
        s1_ref[hh] = _nt(k1_ref[hh].astype(BF16), q1)
        s2_ref[hh] = _nt(k2_ref[hh].astype(BF16), q2)


def _out_proj(x2, oa, ob, wo, fg, wq, k1, k2, tm):
    t, d = x2.shape
    wa = oa.shape[1]
    wb = ob.shape[1]
    const = dict(pipeline_mode=pl.Buffered(1))
    return pl.pallas_call(
        _outproj_body,
        grid=(t // tm,),
        in_specs=[
            pl.BlockSpec((tm, d), lambda i: (i, 0)),
            pl.BlockSpec((tm, wa), lambda i: (i, 0)),
            pl.BlockSpec((tm, wb), lambda i: (i, 0)),
            pl.BlockSpec((wa + wb, d), lambda i: (0, 0), **const),
            pl.BlockSpec((1, d), lambda i: (0, 0)),
            pl.BlockSpec((d, P_HEADS * 2 * HD), lambda i: (0, 0), **const),
            pl.BlockSpec((P_HEADS, N_KEYS, HD), lambda i: (0, 0, 0)),
            pl.BlockSpec((P_HEADS, N_KEYS, HD), lambda i: (0, 0, 0)),
        ],
        out_specs=[
            pl.BlockSpec((tm, d), lambda i: (i, 0)),
            pl.BlockSpec((tm, d), lambda i: (i, 0)),
            pl.BlockSpec((P_HEADS, N_KEYS, tm), lambda i: (0, 0, i)),
            pl.BlockSpec((P_HEADS, N_KEYS, tm), lambda i: (0, 0, i)),
        ],
        out_shape=[
            jax.ShapeDtypeStruct((t, d), F32),
            jax.ShapeDtypeStruct((t, d), F32),
            jax.ShapeDtypeStruct((P_HEADS, N_KEYS, t), F32),
            jax.ShapeDtypeStruct((P_HEADS, N_KEYS, t), F32),
        ],
        compiler_params=pltpu.CompilerParams(
            dimension_semantics=("parallel",),
            vmem_limit_bytes=VMEM_LIMIT),
        name="out_proj",
    )(x2, oa, ob, wo, fg, wq, k1, k2)


_PAIRS = [(i, j) for i in range(P_TOPK) for j in range(P_TOPK) if (i + 1) * (j + 1) <= P_TOPK]
_NCAND = -(-len(_PAIRS) // 8) * 8


def _topk_body(s1_ref, s2_ref, pos_ref, e_ref, g_ref, v_scr, i_scr, c_scr, ce_scr, b_scr, x_scr):
    tt = s1_ref.shape[2]
    kio = lax.broadcasted_iota(jnp.int32, (N_KEYS, tt), 0).astype(F32)
    neg = -jnp.inf
    for half, sref in ((0, s1_ref), (1, s2_ref)):
        s = sref[0]
        for r in range(P_TOPK):
            m = jnp.max(s, axis=0, keepdims=True)
            idx = jnp.min(jnp.where(s == m, kio, float(N_KEYS)), axis=0, keepdims=True)
            v_scr[half, r:r + 1, :] = m
            i_scr[half, r:r + 1, :] = idx
            s = jnp.where(kio == idx, neg, s)
    c_scr[...] = jnp.full(c_scr.shape, neg, F32)
    ce_scr[...] = jnp.zeros(ce_scr.shape, F32)
    for c, (i, j) in enumerate(_PAIRS):
        c_scr[c:c + 1, :] = v_scr[0, i:i + 1, :] + v_scr[1, j:j + 1, :]
        ce_scr[c:c + 1, :] = i_scr[0, i:i + 1, :] * float(N_KEYS) + i_scr[1, j:j + 1, :]
    cand = c_scr[...]
    ce = ce_scr[...]
    pos = jnp.broadcast_to(pos_ref[:, 0:1], cand.shape)
    for r in range(P_TOPK):
        m = jnp.max(cand, axis=0, keepdims=True)
        sel = jnp.min(jnp.where(cand == m, pos, 1e9), axis=0, keepdims=True)
        hit = pos == sel
        b_scr[r:r + 1, :] = m
        x_scr[r:r + 1, :] = jnp.max(jnp.where(hit, ce, -1.0), axis=0, keepdims=True)
        cand = jnp.where(hit, neg, cand)
    best = b_scr[...]
    ex = jnp.exp(best - best[0:1, :])
    g_ref[...] = ex / jnp.sum(ex, axis=0, keepdims=True)
    e_ref[...] = x_scr[...].astype(jnp.int32)


def _peer_topk(s1, s2, pos, tt):
    t = s1.shape[2]
    return pl.pallas_call(
        _topk_body,
        grid=(t // tt, P_HEADS),
        in_specs=[
            pl.BlockSpec((1, N_KEYS, tt), lambda i, h: (h, 0, i)),
            pl.BlockSpec((1, N_KEYS, tt), lambda i, h: (h, 0, i)),
            pl.BlockSpec((_NCAND, HD), lambda i, h: (0, 0)),
        ],
        out_specs=[
            pl.BlockSpec((P_TOPK, tt), lambda i, h: (h, i)),
            pl.BlockSpec((P_TOPK, tt), lambda i, h: (h, i)),
        ],
        out_shape=[
            jax.ShapeDtypeStruct((P_HEADS * P_TOPK, t), jnp.int32),
            jax.ShapeDtypeStruct((P_HEADS * P_TOPK, t), F32),
        ],
        scratch_shapes=[
            pltpu.VMEM((2, P_TOPK, tt), F32),
            pltpu.VMEM((2, P_TOPK, tt), F32),
            pltpu.VMEM((_NCAND, tt), F32),
            pltpu.VMEM((_NCAND, tt), F32),
            pltpu.VMEM((P_TOPK, tt), F32),
            pltpu.VMEM((P_TOPK, tt), F32),
        ],
        compiler_params=pltpu.CompilerParams(
            dimension_semantics=("parallel", "parallel"),
            vmem_limit_bytes=VMEM_LIMIT),
        name="peer_topk",
    )(s1, s2, pos)


PEER_TB = 128
PEER_G = 8
NSLOT = P_HEADS * P_TOPK
SG = 8


def _gelu(x):
    return 0.5 * x * (1.0 + lax.erf(x * (2.0 ** -0.5)))


def _peer_body(idx_ref, gt_ref, hn_ref, h_ref, fg_ref, u_hbm, v_hbm, o_ref,
               ubuf, vbuf, hid_scr, c_scr, y_scr, sem, *, final):
    tb, d = hn_ref.shape
    ngroups = tb // PEER_G
    nlc = d // HD

    def row_copies(g, set_, r):
        tok = g * PEER_G + r // NSLOT
        e = idx_ref[tok, r % NSLOT]
        cu = pltpu.make_async_copy(u_hbm.at[pl.ds(e, 1), :], ubuf.at[set_, pl.ds(r, 1), :],
                                   sem.at[0, set_])
        cv = pltpu.make_async_copy(v_hbm.at[pl.ds(e, 1), :], vbuf.at[set_, pl.ds(r, 1), :],
                                   sem.at[1, set_])
        return cu, cv

    def issue(g, set_):
        def body(r8, carry):
            for k in range(8):
                cu, cv = row_copies(g, set_, r8 * 8 + k)
                cu.start()
                cv.start()
            return carry
        lax.fori_loop(0, PEER_G * NSLOT // 8, body, 0)

    def wait(g, set_):
        def body(r8, carry):
            for k in range(8):
                cu, cv = row_copies(g, set_, r8 * 8 + k)
                cu.wait()
                cv.wait()
            return carry
        lax.fori_loop(0, PEER_G * NSLOT // 8, body, 0)

    lane = lax.broadcasted_iota(jnp.int32, (SG, tb), 1)
    issue(0, 0)

    def group_body(g, carry):
        set_ = g % 2

        @pl.when(g + 1 < ngroups)
        def _():
            issue(g + 1, 1 - set_)

        wait(g, set_)

        def hid_tok(r, c2):
            t = g * PEER_G + r
            xb = jnp.broadcast_to(hn_ref[pl.ds(t, 1), :], (SG, d))
            for sg in range(NSLOT // SG):
                rows = ubuf[set_, pl.ds(r * NSLOT + sg * SG, SG), :]
                acc = rows[:, 0:HD] * xb[:, 0:HD]
                for lc in range(1, nlc):
                    acc = acc + rows[:, lc * HD:(lc + 1) * HD] * xb[:, lc * HD:(lc + 1) * HD]
                hs = jnp.sum(acc, axis=1, keepdims=True)
                sl = pl.ds(sg * SG, SG)
                hid_scr[sl, :] = jnp.where(lane == t, hs, hid_scr[sl, :])
            return c2
        lax.fori_loop(0, PEER_G, hid_tok, 0)

        c_scr[...] = gt_ref[...] * _gelu(hid_scr[...])

        def mix_tok(r, c2):
            t = g * PEER_G + r
            accs = [jnp.zeros((SG, HD), F32) for _ in range(nlc)]
            for sg in range(NSLOT // SG):
                sl = pl.ds(sg * SG, SG)
                cc = jnp.sum(jnp.where(lane == t, c_scr[sl, :], 0.0), axis=1, keepdims=True)
                rows = vbuf[set_, pl.ds(r * NSLOT + sg * SG, SG), :]
                for lc in range(nlc):
                    accs[lc] = accs[lc] + rows[:, lc * HD:(lc + 1) * HD] * cc
            yrow = jnp.concatenate(
                [jnp.sum(a, axis=0, keepdims=True) for a in accs], axis=1)
            y_scr[pl.ds(t, 1), :] = yrow
            return c2
        lax.fori_loop(0, PEER_G, mix_tok, 0)
        return carry

    hid_scr[...] = jnp.zeros_like(hid_scr)
    lax.fori_loop(0, ngroups, group_body, 0)

    hh = h_ref[...] + y_scr[...]
    if final:
        hh = hh * lax.rsqrt(jnp.mean(hh * hh, axis=-1, keepdims=True) + EPS) * fg_ref[...]
    o_ref[...] = hh


def _peer_mix(idx, gt, hn, h, fg, u_tab, v_tab, final):
    t, d = hn.shape
    tb = PEER_TB
    return pl.pallas_call(
        functools.partial(_peer_body, final=final),
        grid=(t // tb,),
        in_specs=[
            pl.BlockSpec((tb, NSLOT), lambda i: (i, 0), memory_space=pltpu.SMEM),
            pl.BlockSpec((NSLOT, tb), lambda i: (0, i)),
            pl.BlockSpec((tb, d), lambda i: (i, 0)),
            pl.BlockSpec((tb, d), lambda i: (i, 0)),
            pl.BlockSpec((1, d), lambda i: (0, 0)),
            pl.BlockSpec(memory_space=pl.ANY),
            pl.BlockSpec(memory_space=pl.ANY),
        ],
        out_specs=pl.BlockSpec((tb, d), lambda i: (i, 0)),
        out_shape=jax.ShapeDtypeStruct((t, d), F32),
        scratch_shapes=[
            pltpu.VMEM((2, PEER_G * NSLOT, d), F32),
            pltpu.VMEM((2, PEER_G * NSLOT, d), F32),
            pltpu.VMEM((NSLOT, tb), F32),
            pltpu.VMEM((NSLOT, tb), F32),
            pltpu.VMEM((tb, d), F32),
            pltpu.SemaphoreType.DMA((2, 2)),
        ],
        compiler_params=pltpu.CompilerParams(
            dimension_semantics=("arbitrary",),
            vmem_limit_bytes=VMEM_LIMIT),
        name="peer_mix",
    )(idx, gt, hn, h, fg, u_tab, v_tab)


def _tile(n, pref):
    return pref if n % pref == 0 else n


def kernel(x, attn_norm_g, w_in, hgrn_lb_logits, hgrn_norm_g, gdn_conv_w, gdn_A_log, gdn_dt_bias,
           gdn_norm_g, w_out, ffn_norm_g, peer_w_query, peer_sub_keys, peer_u, peer_v,
           final_norm_g):
    b, s, d = x.shape
    t = b * s
    depth = w_in.shape[0]
    in_width = w_in.shape[2]
    n_small = 2 * B_V_HEADS
    n_main = in_width - n_small
    in_pad = n_main + HD
    lb_all = jnp.cumsum(jax.nn.softmax(hgrn_lb_logits.astype(F32), axis=0), axis=0)
    pos = jnp.array([i * P_TOPK + j for i, j in _PAIRS]
                    + [10 ** 6 + c for c in range(_NCAND - len(_PAIRS))], F32)
    pos = jnp.broadcast_to(pos[:, None], (_NCAND, HD))

    h2 = x.reshape(t, d)
    for l in range(depth):
        w_l = jnp.pad(w_in[l], ((0, 0), (0, in_pad - in_width))).astype(BF16)
        tn = in_pad // 3 if (in_pad // HD) % 3 == 0 else in_pad
        proj = _in_proj(h2, attn_norm_g[l][None, :], w_l, _tile(t, 512), tn)
        proj3 = proj.reshape(b, s, in_pad)

        o_a = _hgrn(proj3, lb_all[l], hgrn_norm_g[l][None, :], _tile(s, 256))

        ts = _tile(s, 256)
        small = proj3[:, :, n_main:n_main + n_small]
        smt = small.reshape(b, s // CHUNK, CHUNK, n_small).transpose(0, 1, 3, 2)
        zeros8 = jnp.zeros((B_V_HEADS,), F32)
        a_neg = jnp.exp(gdn_A_log[l].astype(F32))
        dtb = gdn_dt_bias[l].astype(F32)
        pcol = jnp.zeros((2, HD), F32)
        pcol = pcol.at[0, B_V_HEADS:n_small].set(a_neg).at[1, B_V_HEADS:n_small].set(dtb)
        prow = jnp.zeros((n_small, HD), F32)
        prow = prow.at[:, 0].set(jnp.concatenate([zeros8, a_neg]))
        prow = prow.at[:, 1].set(jnp.concatenate([zeros8, dtb]))
        o_b = _gdn(proj3, smt, gdn_conv_w[l], pcol, prow, gdn_norm_g[l][None, :], ts)

        wq = peer_w_query[l].reshape(d, P_HEADS * 2 * HD).astype(BF16)
        h2, hn, s1, s2 = _out_proj(
            h2, o_a.reshape(t, -1), o_b.reshape(t, -1), w_out[l].astype(BF16),
            ffn_norm_g[l][None, :], wq, peer_sub_keys[l, 0], peer_sub_keys[l, 1], _tile(t, 256))

        e_t, g_t = _peer_topk(s1, s2, pos, _tile(t, 256))
        h2 = _peer_mix(e_t.T, g_t, hn, h2, final_norm_g[None, :], peer_u[l], peer_v[l],
                       l == depth - 1)
    return h2.reshape(b, s, d)
```

```python
import functools

import jax
import jax.numpy as jnp
from jax import lax
from jax.experimental import pallas as pl
from jax.experimental.pallas import tpu as pltpu

F32 = jnp.float32
BF16 = jnp.bfloat16

EPS = 1e-6
CHUNK = 64
SUB = 16
HD = 128
A_HEADS = 8
B_QK_HEADS = 4
B_V_HEADS = 8
CONV = 4
P_HEADS = 8
N_KEYS = 128
P_TOPK = 16
VMEM_LIMIT = 56 * 1024 * 1024


def _nt(a, b):
    return lax.dot_general(a, b, (((1,), (1,)), ((), ())), preferred_element_type=F32)


def _tn(a, b):
    return lax.dot_general(a, b, (((0,), (0,)), ((), ())), preferred_element_type=F32)


def _mm(a, b):
    return jnp.dot(a, b, preferred_element_type=F32)


def _split_bf16(x, n):
    parts = []
    for _ in range(n - 1):
        p = x.astype(BF16)
        parts.append(p)
        x = x - p.astype(F32)
    parts.append(x.astype(BF16))
    return parts


def _mm3(a, b):
    ah, al = _split_bf16(a, 2)
    bh, bl = _split_bf16(b, 2)
    return _mm(ah, bh) + (_mm(ah, bl) + _mm(al, bh))


def _cumsum_mm(tri, g, g_is_lhs):
    t16 = tri.astype(BF16)
    out = None
    for p in reversed(_split_bf16(g, 3)):
        term = _mm(p, t16) if g_is_lhs else _mm(t16, p)
        out = term if out is None else out + term
    return out


def _sigmoid(x):
    return 1.0 / (1.0 + jnp.exp(-x))


def _softplus(x):
    return jnp.maximum(x, 0.0) + jnp.log1p(jnp.exp(-jnp.abs(x)))


def _bcast_rows(x, idx):
    n = x.shape[1]
    return jnp.concatenate(
        [jnp.broadcast_to(x[r:r + 1, :], (SUB, n)) for r in idx], axis=0)


def _inproj_body(x_ref, g_ref, w_ref, o_ref):
    x = x_ref[...]
    ms = jnp.mean(x * x, axis=-1, keepdims=True)
    xn = (x * lax.rsqrt(ms + EPS) * g_ref[...]).astype(BF16)
    o_ref[...] = _mm(xn, w_ref[...])


def _in_proj(x2, g, w, tm, tn):
    t, d = x2.shape
    n = w.shape[1]
    return pl.pallas_call(
        _inproj_body,
        grid=(n // tn, t // tm),
        in_specs=[
            pl.BlockSpec((tm, d), lambda j, i: (i, 0)),
            pl.BlockSpec((1, d), lambda j, i: (0, 0)),
            pl.BlockSpec((d, tn), lambda j, i: (0, j)),
        ],
        out_specs=pl.BlockSpec((tm, tn), lambda j, i: (i, j)),
        out_shape=jax.ShapeDtypeStruct((t, n), F32),
        compiler_params=pltpu.CompilerParams(
            dimension_semantics=("parallel", "parallel"),
            vmem_limit_bytes=VMEM_LIMIT),
        name="in_proj",
    )(x2, g, w)


def _hgrn_body(q_ref, f_ref, i_ref, g_ref, lb_ref, gn_ref, o_ref, st_ref, *, ts):
    @pl.when(pl.program_id(1) == 0)
    def _():
        st_ref[...] = jnp.zeros_like(st_ref)

    row = lax.broadcasted_iota(jnp.int32, (CHUNK, CHUNK), 0)
    col = lax.broadcasted_iota(jnp.int32, (CHUNK, CHUNK), 1)
    tril_f = (col <= row).astype(F32)
    blk_r = row // SUB
    blk_c = col // SUB
    rblk = lax.broadcasted_iota(jnp.int32, (CHUNK, HD), 0) // SUB
    nsub = CHUNK // SUB
    gn = gn_ref[...]

    def chunk_body(c, carry):
        r0 = pl.multiple_of(c * CHUNK, CHUNK)
        rows = pl.ds(r0, CHUNK)
        hs = range(A_HEADS)
        cols = [slice(h * HD, (h + 1) * HD) for h in hs]
        q = [q_ref[0, rows, cols[h]] for h in hs]
        z = [f_ref[0, rows, cols[h]] for h in hs]
        v16 = [i_ref[0, rows, cols[h]].astype(BF16) for h in hs]
        lb = [lb_ref[h:h + 1, :] for h in hs]
        sts = [st_ref[h] for h in hs]
        k = [(1.0 - lb[h]) * _sigmoid(-z[h]) for h in hs]
        b = [_cumsum_mm(tril_f, jnp.log(lb[h] + (1.0 - lb[h]) * _sigmoid(z[h])), False)
             for h in hs]
        o = [_nt((q[h] * jnp.exp(b[h])).astype(BF16), sts[h].astype(BF16)) for h in hs]
        att = []
        for h in hs:
            bnd = [b[h][SUB * j + SUB - 1:SUB * j + SUB, :] for j in range(nsub)]
            k_off = k[h] * jnp.exp(
                _bcast_rows(b[h], [SUB * j + SUB - 1 for j in range(nsub)]) - b[h])
            qcat = jnp.concatenate(
                [q[h] * jnp.exp(jnp.minimum(b[h] - bnd[j], 0.0)) for j in range(nsub - 1)],
                axis=1)
            kcat = jnp.concatenate(
                [jnp.where(rblk == j, k_off, 0.0) for j in range(nsub - 1)], axis=1)
            att_off = _nt(qcat.astype(BF16), kcat.astype(BF16))
            ref_rows = _bcast_rows(b[h], [SUB * i for i in range(nsub)])
            q_d = q[h] * jnp.exp(b[h] - ref_rows)
            k_d = k[h] * jnp.exp(ref_rows - b[h])
            att_d = _nt(q_d.astype(BF16), k_d.astype(BF16))
            att.append(jnp.where(blk_r > blk_c, att_off,
                                 jnp.where((blk_r == blk_c) & (col <= row), att_d, 0.0)))
        o = [o[h] + _mm(att[h].astype(BF16), v16[h]) for h in hs]
        new_sts = []
        for h in hs:
            b_last = b[h][CHUNK - 1:CHUNK, :]
            kdec = k[h] * jnp.exp(b_last - b[h])
            new_sts.append(sts[h] * jnp.exp(b_last) + _tn(v16[h], kdec.astype(BF16)))
        for h in hs:
            st_ref[h] = new_sts[h]
        for h in hs:
            gate = g_ref[0, rows, cols[h]]
            y = o[h] * lax.rsqrt(jnp.mean(o[h] * o[h], axis=-1, keepdims=True) + EPS)
            o_ref[0, rows, cols[h]] = (y * gn * (gate * _sigmoid(gate))).astype(o_ref.dtype)
        return carry

    lax.fori_loop(0, ts // CHUNK, chunk_body, 0)


def _hgrn(proj3, lb, gn, ts):
    b, s, _ = proj3.shape
    w = A_HEADS * HD
    return pl.pallas_call(
        functools.partial(_hgrn_body, ts=ts),
        grid=(b, s // ts),
        in_specs=[
            pl.BlockSpec((1, ts, w), lambda bi, ti: (bi, ti, 0)),
            pl.BlockSpec((1, ts, w), lambda bi, ti: (bi, ti, 1)),
            pl.BlockSpec((1, ts, w), lambda bi, ti: (bi, ti, 2)),
            pl.BlockSpec((1, ts, w), lambda bi, ti: (bi, ti, 3)),
            pl.BlockSpec((A_HEADS, HD), lambda bi, ti: (0, 0)),
            pl.BlockSpec((1, HD), lambda bi, ti: (0, 0)),
        ],
        out_specs=pl.BlockSpec((1, ts, w), lambda bi, ti: (bi, ti, 0)),
        out_shape=jax.ShapeDtypeStruct((b, s, w), BF16),
        scratch_shapes=[pltpu.VMEM((A_HEADS, HD, HD), F32)],
        compiler_params=pltpu.CompilerParams(
            dimension_semantics=("parallel", "arbitrary"),
            vmem_limit_bytes=VMEM_LIMIT),
        name="hgrn",
    )(proj3, proj3, proj3, proj3, lb, gn)


def _gdn_body(bq_ref, bk_ref, bv_ref, bz_ref, sm_ref, smt_ref, cw_ref, pcol_ref, prow_ref,
              gn_ref, o_ref, s_ref, tail_ref, qn_ref, kn_ref, vv_ref,
              u_s, w_s, qg_s, kd_s, aqk_s, dl_s, *, ts):
    nqk = B_QK_HEADS * HD

    @pl.when(pl.program_id(1) == 0)
    def _():
        s_ref[...] = jnp.zeros_like(s_ref)
        tail_ref[...] = jnp.zeros_like(tail_ref)

    def conv_silu(x, c0):
        n = x.shape[1]
        xe = jnp.concatenate([tail_ref[:, c0:c0 + n], x], axis=0)
        w = cw_ref[:, c0:c0 + n]
        y = x * w[CONV - 1:CONV, :]
        for j in range(1, CONV):
            y = y + pltpu.roll(xe, j, axis=0)[8:, :] * w[CONV - 1 - j:CONV - j, :]
        tail_ref[:, c0:c0 + n] = x[ts - 8:, :]
        return y * _sigmoid(y)

    def l2n(x):
        return x * lax.rsqrt(jnp.sum(x * x, axis=-1, keepdims=True) + EPS)

    yq = conv_silu(bq_ref[0], 0)
    yk = conv_silu(bk_ref[0], nqk)
    vv_ref[...] = conv_silu(bv_ref[0], 2 * nqk)
    for h in range(B_QK_HEADS):
        cs = slice(h * HD, (h + 1) * HD)
        qn_ref[:, cs] = l2n(yq[:, cs]) * (HD ** -0.5)
        kn_ref[:, cs] = l2n(yk[:, cs])

    row = lax.broadcasted_iota(jnp.int32, (CHUNK, CHUNK), 0)
    col = lax.broadcasted_iota(jnp.int32, (CHUNK, CHUNK), 1)
    incl = col <= row
    strict = col < row
    tril_f = incl.astype(F32)
    triu_f = (row <= col).astype(F32)
    same_blk = (row // SUB) == (col // SUB)
    eye = (row == col).astype(F32)
    gn = gn_ref[...]
    a_col = pcol_ref[0:1, :]
    dt_col = pcol_ref[1:2, :]
    a_row = prow_ref[:, 0:1]
    dt_row = prow_ref[:, 1:2]
    rep = B_V_HEADS // B_QK_HEADS

    def prep_chunk(c, carry):
        r0 = pl.multiple_of(c * CHUNK, CHUNK)
        rows = pl.ds(r0, CHUNK)
        sm = sm_ref[0, rows, :]
        beta_c = _sigmoid(sm)
        gc_c = _cumsum_mm(tril_f, -a_col * _softplus(sm + dt_col), False)
        smt = smt_ref[0, c]
        gc_r = _cumsum_mm(triu_f, -a_row * _softplus(smt + dt_row), True)
        hs = range(B_V_HEADS)
        qq = [qn_ref[rows, (h // rep) * HD:(h // rep + 1) * HD] for h in hs]
        kk = [kn_ref[rows, (h // rep) * HD:(h // rep + 1) * HD] for h in hs]
        kk16 = [k.astype(BF16) for k in kk]
        bcol = [beta_c[:, h:h + 1] for h in hs]
        gcol = [gc_c[:, B_V_HEADS + h:B_V_HEADS + h + 1] for h in hs]
        decay = [jnp.where(incl, jnp.exp(jnp.minimum(
            gcol[h] - gc_r[B_V_HEADS + h:B_V_HEADS + h + 1, :], 0.0)), 0.0) for h in hs]
        kb = [kk[h] * bcol[h] for h in hs]
        a = [jnp.where(strict, _nt(kb[h].astype(BF16), kk16[h]) * decay[h], 0.0) for h in hs]
        dg = [jnp.where(same_blk, a[h], 0.0) for h in hs]
        off = [a[h] - dg[h] for h in hs]
        tinv = [eye - dg[h] for h in hs]
        p = dg
        for _ in range(3):
            p = [_mm3(p[h], p[h]) for h in hs]
            tinv = [tinv[h] + _mm3(tinv[h], p[h]) for h in hs]
        pm = [_mm3(tinv[h], off[h]) for h in hs]
        pm2 = [_mm3(pm[h], pm[h]) for h in hs]
        egc = [jnp.exp(gcol[h]) for h in hs]
        x = [_mm3(tinv[h], jnp.concatenate(
            [vv_ref[rows, h * HD:(h + 1) * HD] * bcol[h], kb[h] * egc[h]], axis=1)) for h in hs]
        x = [x[h] + _mm3(pm2[h], x[h]) for h in hs]
        x = [x[h] - _mm3(pm[h], x[h]) for h in hs]
        aqk = [jnp.where(incl, _nt(qq[h].astype(BF16), kk16[h]) * decay[h], 0.0) for h in hs]
        for h in hs:
            cs = slice(h * HD, (h + 1) * HD)
            g_last = gcol[h][CHUNK - 1:CHUNK, :]
            u_s[rows, cs] = x[h][:, :HD]
            w_s[rows, cs] = x[h][:, HD:].astype(BF16)
            qg_s[rows, cs] = (qq[h] * egc[h]).astype(BF16)
            kd_s[rows, cs] = (kk[h] * jnp.exp(g_last - gcol[h])).astype(BF16)
            aqk_s[c, h] = aqk[h].astype(BF16)
            dl_s[c, h:h + 1, :] = jnp.broadcast_to(jnp.exp(g_last), (1, HD))
        return carry

    lax.fori_loop(0, ts // CHUNK, prep_chunk, 0)

    def scan_chunk(c, carry):
        r0 = pl.multiple_of(c * CHUNK, CHUNK)
        rows = pl.ds(r0, CHUNK)
        hs = range(B_V_HEADS)
        cols = [slice(h * HD, (h + 1) * HD) for h in hs]
        sts = [s_ref[h] for h in hs]
        st16 = [s.astype(BF16) for s in sts]
        v16 = [(u_s[rows, cols[h]] - _mm(w_s[rows, cols[h]], st16[h])).astype(BF16) for h in hs]
        o = [_mm(qg_s[rows, cols[h]], st16[h]) + _mm(aqk_s[c, h], v16[h]) for h in hs]
        new_sts = [sts[h] * dl_s[c, h:h + 1, :] + _tn(kd_s[rows, cols[h]], v16[h]) for h in hs]
        for h in hs:
            s_ref[h] = new_sts[h]
        for h in hs:
            zz = bz_ref[0, rows, cols[h]]
            y = o[h] * lax.rsqrt(jnp.mean(o[h] * o[h], axis=-1, keepdims=True) + EPS)
            o_ref[0, rows, cols[h]] = (y * gn * (zz * _sigmoid(zz))).astype(o_ref.dtype)
        return carry

    lax.fori_loop(0, ts // CHUNK, scan_chunk, 0)


def _gdn(proj3, smt, conv_w, pcol, prow, gn, ts):
    b, s, _ = proj3.shape
    nqk = B_QK_HEADS * HD
    wv = B_V_HEADS * HD
    off_q = (4 * A_HEADS * HD) // nqk
    off_v = (4 * A_HEADS * HD + 2 * nqk) // wv
    off_s = (4 * A_HEADS * HD + 2 * nqk + 2 * wv) // HD
    nch = ts // CHUNK
    return pl.pallas_call(
        functools.partial(_gdn_body, ts=ts),
        grid=(b, s // ts),
        in_specs=[
            pl.BlockSpec((1, ts, nqk), lambda bi, ti: (bi, ti, off_q)),
            pl.BlockSpec((1, ts, nqk), lambda bi, ti: (bi, ti, off_q + 1)),
            pl.BlockSpec((1, ts, wv), lambda bi, ti: (bi, ti, off_v)),
            pl.BlockSpec((1, ts, wv), lambda bi, ti: (bi, ti, off_v + 1)),
            pl.BlockSpec((1, ts, HD), lambda bi, ti: (bi, ti, off_s)),
            pl.BlockSpec((1, nch, 2 * B_V_HEADS, CHUNK), lambda bi, ti: (bi, ti, 0, 0)),
            pl.BlockSpec((CONV, 2 * nqk + wv), lambda bi, ti: (0, 0)),
            pl.BlockSpec((2, HD), lambda bi, ti: (0, 0)),
            pl.BlockSpec((2 * B_V_HEADS, HD), lambda bi, ti: (0, 0)),
            pl.BlockSpec((1, HD), lambda bi, ti: (0, 0)),
        ],
        out_specs=pl.BlockSpec((1, ts, wv), lambda bi, ti: (bi, ti, 0)),
        out_shape=jax.ShapeDtypeStruct((b, s, wv), BF16),
        scratch_shapes=[
            pltpu.VMEM((B_V_HEADS, HD, HD), F32),
            pltpu.VMEM((8, 2 * nqk + wv), F32),
            pltpu.VMEM((ts, nqk), F32),
            pltpu.VMEM((ts, nqk), F32),
            pltpu.VMEM((ts, wv), F32),
            pltpu.VMEM((ts, wv), F32),
            pltpu.VMEM((ts, wv), BF16),
            pltpu.VMEM((ts, wv), BF16),
            pltpu.VMEM((ts, wv), BF16),
            pltpu.VMEM((nch, B_V_HEADS, CHUNK, CHUNK), BF16),
            pltpu.VMEM((nch, B_V_HEADS, HD), F32),
        ],
        compiler_params=pltpu.CompilerParams(
            dimension_semantics=("parallel", "arbitrary"),
            vmem_limit_bytes=VMEM_LIMIT),
        name="gdn",
    )(proj3, proj3, proj3, proj3, proj3, smt, conv_w, pcol, prow, gn)


def _outproj_body(x_ref, oa_ref, ob_ref, wo_ref, fg_ref, wq_ref, k1_ref, k2_ref,
                  h_ref, hn_ref, s1_ref, s2_ref):
    wa = oa_ref.shape[1]
    h = x_ref[...] + _mm(oa_ref[...], wo_ref[:wa, :]) + _mm(ob_ref[...], wo_ref[wa:, :])
    h_ref[...] = h
    hn = h * lax.rsqrt(jnp.mean(h * h, axis=-1, keepdims=True) + EPS) * fg_ref[...]
    hn_ref[...] = hn
    qry = _mm(hn.astype(BF16), wq_ref[...])
    for hh in range(P_HEADS):
        q1 = qry[:, hh * 2 * HD:hh * 2 * HD + HD].astype(BF16)
        q2 = qry[:, hh * 2 * HD + HD:(hh + 1) * 2 * HD].astype(BF16)
        s1_ref[hh] = _nt(k1_ref[hh].astype(BF16), q1)
        s2_ref[hh] = _nt(k2_ref[hh].astype(BF16), q2)


def _out_proj(x2, oa, ob, wo, fg, wq, k1, k2, tm):
    t, d = x2.shape
    wa = oa.shape[1]
    wb = ob.shape[1]
    const = dict(pipeline_mode=pl.Buffered(1))
    return pl.pallas_call(
        _outproj_body,
        grid=(t // tm,),
        in_specs=[
            pl.BlockSpec((tm, d), lambda i: (i, 0)),
            pl.BlockSpec((tm, wa), lambda i: (i, 0)),
            pl.BlockSpec((tm, wb), lambda i: (i, 0)),
            pl.BlockSpec((wa + wb, d), lambda i: (0, 0), **const),
            pl.BlockSpec((1, d), lambda i: (0, 0)),
            pl.BlockSpec((d, P_HEADS * 2 * HD), lambda i: (0, 0), **const),
            pl.BlockSpec((P_HEADS, N_KEYS, HD), lambda i: (0, 0, 0)),
            pl.BlockSpec((P_HEADS, N_KEYS, HD), lambda i: (0, 0, 0)),
        ],
        out_specs=[
            pl.BlockSpec((tm, d), lambda i: (i, 0)),
            pl.BlockSpec((tm, d), lambda i: (i, 0)),
            pl.BlockSpec((P_HEADS, N_KEYS, tm), lambda i: (0, 0, i)),
            pl.BlockSpec((P_HEADS, N_KEYS, tm), lambda i: (0, 0, i)),
        ],
        out_shape=[
            jax.ShapeDtypeStruct((t, d), F32),
            jax.ShapeDtypeStruct((t, d), F32),
            jax.ShapeDtypeStruct((P_HEADS, N_KEYS, t), F32),
            jax.ShapeDtypeStruct((P_HEADS, N_KEYS, t), F32),
        ],
        compiler_params=pltpu.CompilerParams(
            dimension_semantics=("parallel",),
            vmem_limit_bytes=VMEM_LIMIT),
        name="out_proj",
    )(x2, oa, ob, wo, fg, wq, k1, k2)


_PAIRS = [(i, j) for i in range(P_TOPK) for j in range(P_TOPK) if (i + 1) * (j + 1) <= P_TOPK]
_NCAND = -(-len(_PAIRS) // 8) * 8


def _topk_body(s1_ref, s2_ref, pos_ref, e_ref, g_ref, v_scr, i_scr, c_scr, ce_scr, b_scr, x_scr):
    tt = s1_ref.shape[2]
    kio = lax.broadcasted_iota(jnp.int32, (N_KEYS, tt), 0).astype(F32)
    neg = -jnp.inf
    for half, sref in ((0, s1_ref), (1, s2_ref)):
        s = sref[0]
        for r in range(P_TOPK):
            m = jnp.max(s, axis=0, keepdims=True)
            idx = jnp.min(jnp.where(s == m, kio, float(N_KEYS)), axis=0, keepdims=True)
            v_scr[half, r:r + 1, :] = m
            i_scr[half, r:r + 1, :] = idx
            s = jnp.where(kio == idx, neg, s)
    c_scr[...] = jnp.full(c_scr.shape, neg, F32)
    ce_scr[...] = jnp.zeros(ce_scr.shape, F32)
    for c, (i, j) in enumerate(_PAIRS):
        c_scr[c:c + 1, :] = v_scr[0, i:i + 1, :] + v_scr[1, j:j + 1, :]
        ce_scr[c:c + 1, :] = i_scr[0, i:i + 1, :] * float(N_KEYS) + i_scr[1, j:j + 1, :]
    cand = c_scr[...]
    ce = ce_scr[...]
    pos = jnp.broadcast_to(pos_ref[:, 0:1], cand.shape)
    for r in range(P_TOPK):
        m = jnp.max(cand, axis=0, keepdims=True)
        sel = jnp.min(jnp.where(cand == m, pos, 1e9), axis=0, keepdims=True)
        hit = pos == sel
        b_scr[r:r + 1, :] = m
        x_scr[r:r + 1, :] = jnp.max(jnp.where(hit, ce, -1.0), axis=0, keepdims=True)
        cand = jnp.where(hit, neg, cand)
    best = b_scr[...]
    ex = jnp.exp(best - best[0:1, :])
    g_ref[...] = ex / jnp.sum(ex, axis=0, keepdims=True)
    e_ref[...] = x_scr[...].astype(jnp.int32)


def _peer_topk(s1, s2, pos, tt):
    t = s1.shape[2]
    return pl.pallas_call(
        _topk_body,
        grid=(t // tt, P_HEADS),
        in_specs=[
            pl.BlockSpec((1, N_KEYS, tt), lambda i, h: (h, 0, i)),
            pl.BlockSpec((1, N_KEYS, tt), lambda i, h: (h, 0, i)),
            pl.BlockSpec((_NCAND, HD), lambda i, h: (0, 0)),
        ],
        out_specs=[
            pl.BlockSpec((P_TOPK, tt), lambda i, h: (h, i)),
            pl.BlockSpec((P_TOPK, tt), lambda i, h: (h, i)),
        ],
        out_shape=[
            jax.ShapeDtypeStruct((P_HEADS * P_TOPK, t), jnp.int32),
            jax.ShapeDtypeStruct((P_HEADS * P_TOPK, t), F32),
        ],
        scratch_shapes=[
            pltpu.VMEM((2, P_TOPK, tt), F32),
            pltpu.VMEM((2, P_TOPK, tt), F32),
            pltpu.VMEM((_NCAND, tt), F32),
            pltpu.VMEM((_NCAND, tt), F32),
            pltpu.VMEM((P_TOPK, tt), F32),
            pltpu.VMEM((P_TOPK, tt), F32),
        ],
        compiler_params=pltpu.CompilerParams(
            dimension_semantics=("parallel", "parallel"),
            vmem_limit_bytes=VMEM_LIMIT),
        name="peer_topk",
    )(s1, s2, pos)


PEER_TB = 128
PEER_G = 8
NSLOT = P_HEADS * P_TOPK
SG = 8


def _gelu(x):
    return 0.5 * x * (1.0 + lax.erf(x * (2.0 ** -0.5)))


def _peer_body(idx_ref, gt_ref, hn_ref, h_ref, fg_ref, uv_hbm, o_ref,
               buf, hid_scr, c_scr, y_scr, sem, *, final):
    tb, d = hn_ref.shape
    ngroups = tb // PEER_G
    nlc = d // HD
    rows_tok = 2 * NSLOT
    nsg = rows_tok // SG
    per_tile = NSLOT // 2 // nsg
    step = pl.program_id(0)

    def issue_tok(tok, set_, r, s0, s1):
        for s in range(s0, s1):
            e = idx_ref[tok, s]
            pltpu.make_async_copy(uv_hbm.at[e, pl.ds(0, 2), :],
                                  buf.at[set_, r * nsg + (2 * s) // SG, pl.ds((2 * s) % SG, 2), :],
                                  sem.at[set_]).start()

    def wait_set(set_):
        pltpu.make_async_copy(uv_hbm.at[pl.ds(0, PEER_G * nsg)], buf.at[set_],
                              sem.at[set_]).wait()

    @pl.when(step == 0)
    def _():
        def first(r, carry):
            issue_tok(r, 0, r, 0, NSLOT)
            return carry
        lax.fori_loop(0, PEER_G, first, 0)

    lane = lax.broadcasted_iota(jnp.int32, (SG, tb), 1)
    hid_scr[...] = jnp.zeros_like(hid_scr)

    def group_body(g, carry):
        set_ = g % 2
        nxt = 1 - set_
        wait_set(set_)


        def hid_tok(r, c2):
            t = g * PEER_G + r
            xb = jnp.broadcast_to(hn_ref[pl.ds(t, 1), :], (SG, d))
            for sg in range(nsg):
                issue_tok((g + 1) * PEER_G + r, nxt, r, sg * per_tile, (sg + 1) * per_tile)
                rows = buf[set_, r * nsg + sg]
                acc = rows[:, 0:HD] * xb[:, 0:HD]
                for lc in range(1, nlc):
                    acc = acc + rows[:, lc * HD:(lc + 1) * HD] * xb[:, lc * HD:(lc + 1) * HD]
                hs = jnp.sum(acc, axis=1, keepdims=True)
                sl = pl.ds(sg * SG, SG)
                hid_scr[sl, :] = jnp.where(lane == t, hs, hid_scr[sl, :])
            return c2
        lax.fori_loop(0, PEER_G, hid_tok, 0)

        odd = lax.broadcasted_iota(jnp.int32, hid_scr.shape, 0) % 2 == 1
        c_scr[...] = jnp.where(
            odd, gt_ref[...] * pltpu.roll(_gelu(hid_scr[...]), 1, axis=0), 0.0)

        def mix_tok(r, c2):
            t = g * PEER_G + r
            accs = [jnp.zeros((SG, HD), F32) for _ in range(nlc)]
            for sg in range(nsg):
                issue_tok((g + 1) * PEER_G + r, nxt, r,
                          NSLOT // 2 + sg * per_tile, NSLOT // 2 + (sg + 1) * per_tile)
                sl = pl.ds(sg * SG, SG)
                cc = jnp.sum(jnp.where(lane == t, c_scr[sl, :], 0.0), axis=1, keepdims=True)
                rows = buf[set_, r * nsg + sg]
                for lc in range(nlc):
                    accs[lc] = accs[lc] + rows[:, lc * HD:(lc + 1) * HD] * cc
            yrow = jnp.concatenate(
                [jnp.sum(a, axis=0, keepdims=True) for a in accs], axis=1)
            y_scr[pl.ds(t, 1), :] = yrow
            return c2
        lax.fori_loop(0, PEER_G, mix_tok, 0)
        return carry

    lax.fori_loop(0, ngroups, group_body, 0)

    @pl.when(step == pl.num_programs(0) - 1)
    def _():
        wait_set(ngroups % 2)

    hh = h_ref[...] + y_scr[...]
    if final:
        hh = hh * lax.rsqrt(jnp.mean(hh * hh, axis=-1, keepdims=True) + EPS) * fg_ref[...]
    o_ref[...] = hh


def _peer_mix(idx, gt, hn, h, fg, uv_tab, final):
    t, d = hn.shape
    tb = PEER_TB
    nsteps = t // tb
    assert (tb // PEER_G) % 2 == 0
    idx_pad = jnp.concatenate([idx, jnp.zeros((tb, NSLOT), idx.dtype)], axis=0)
    idx_ext = jnp.concatenate(
        [idx.reshape(nsteps, tb, NSLOT),
         idx_pad[tb:].reshape(nsteps, tb, NSLOT)[:, :PEER_G]], axis=1)
    return pl.pallas_call(
        functools.partial(_peer_body, final=final),
        grid=(nsteps,),
        in_specs=[
            pl.BlockSpec((None, tb + PEER_G, NSLOT), lambda i: (i, 0, 0),
                         memory_space=pltpu.SMEM),
            pl.BlockSpec((2 * NSLOT, tb), lambda i: (0, i)),
            pl.BlockSpec((tb, d), lambda i: (i, 0)),
            pl.BlockSpec((tb, d), lambda i: (i, 0)),
            pl.BlockSpec((1, d), lambda i: (0, 0)),
            pl.BlockSpec(memory_space=pl.ANY),
        ],
        out_specs=pl.BlockSpec((tb, d), lambda i: (i, 0)),
        out_shape=jax.ShapeDtypeStruct((t, d), F32),
        scratch_shapes=[
            pltpu.VMEM((2, PEER_G * 2 * NSLOT // SG, SG, d), F32),
            pltpu.VMEM((2 * NSLOT, tb), F32),
            pltpu.VMEM((2 * NSLOT, tb), F32),
            pltpu.VMEM((tb, d), F32),
            pltpu.SemaphoreType.DMA((2,)),
        ],
        compiler_params=pltpu.CompilerParams(
            dimension_semantics=("arbitrary",),
            vmem_limit_bytes=VMEM_LIMIT),
        name="peer_mix",
    )(idx_ext, gt, hn, h, fg, uv_tab)


def _tile(n, pref):
    return pref if n % pref == 0 else n


def kernel(x, attn_norm_g, w_in, hgrn_lb_logits, hgrn_norm_g, gdn_conv_w, gdn_A_log, gdn_dt_bias,
           gdn_norm_g, w_out, ffn_norm_g, peer_w_query, peer_sub_keys, peer_u, peer_v,
           final_norm_g):
    b, s, d = x.shape
    t = b * s
    depth = w_in.shape[0]
    in_width = w_in.shape[2]
    n_small = 2 * B_V_HEADS
    n_main = in_width - n_small
    in_pad = n_main + HD
    lb_all = jnp.cumsum(jax.nn.softmax(hgrn_lb_logits.astype(F32), axis=0), axis=0)
    pos = jnp.array([i * P_TOPK + j for i, j in _PAIRS]
                    + [10 ** 6 + c for c in range(_NCAND - len(_PAIRS))], F32)
    pos = jnp.broadcast_to(pos[:, None], (_NCAND, HD))

    h2 = x.reshape(t, d)
    for l in range(depth):
        w_l = jnp.pad(w_in[l], ((0, 0), (0, in_pad - in_width))).astype(BF16)
        tn = in_pad // 3 if (in_pad // HD) % 3 == 0 else in_pad
        proj = _in_proj(h2, attn_norm_g[l][None, :], w_l, _tile(t, 512), tn)
        proj3 = proj.reshape(b, s, in_pad)

        o_a = _hgrn(proj3, lb_all[l], hgrn_norm_g[l][None, :], _tile(s, 256))

        ts = _tile(s, 256)
        small = proj3[:, :, n_main:n_main + n_small]
        smt = small.reshape(b, s // CHUNK, CHUNK, n_small).transpose(0, 1, 3, 2)
        zeros8 = jnp.zeros((B_V_HEADS,), F32)
        a_neg = jnp.exp(gdn_A_log[l].astype(F32))
        dtb = gdn_dt_bias[l].astype(F32)
        pcol = jnp.zeros((2, HD), F32)
        pcol = pcol.at[0, B_V_HEADS:n_small].set(a_neg).at[1, B_V_HEADS:n_small].set(dtb)
        prow = jnp.zeros((n_small, HD), F32)
        prow = prow.at[:, 0].set(jnp.concatenate([zeros8, a_neg]))
        prow = prow.at[:, 1].set(jnp.concatenate([zeros8, dtb]))
        o_b = _gdn(proj3, smt, gdn_conv_w[l], pcol, prow, gdn_norm_g[l][None, :], ts)

        wq = peer_w_query[l].reshape(d, P_HEADS * 2 * HD).astype(BF16)
        h2, hn, s1, s2 = _out_proj(
            h2, o_a.reshape(t, -1), o_b.reshape(t, -1), w_out[l].astype(BF16),
            ffn_norm_g[l][None, :], wq, peer_sub_keys[l, 0], peer_sub_keys[l, 1], _tile(t, 256))

        e_t, g_t = _peer_topk(s1, s2, pos, _tile(t, 256))
        uv_tab = jnp.zeros((peer_u.shape[1], SG, d), F32)
        uv_tab = uv_tab.at[:, 0].set(peer_u[l]).at[:, 1].set(peer_v[l])
        h2 = _peer_mix(e_t.T, jnp.repeat(g_t, 2, axis=0), hn, h2, final_norm_g[None, :], uv_tab, l == depth - 1)
    return h2.reshape(b, s, d)
```

```python
import functools

import jax
import jax.numpy as jnp
from jax import lax
from jax.experimental import pallas as pl
from jax.experimental.pallas import tpu as pltpu

F32 = jnp.float32
BF16 = jnp.bfloat16

EPS = 1e-6
CHUNK = 64
SUB = 16
HD = 128
A_HEADS = 8
B_QK_HEADS = 4
B_V_HEADS = 8
CONV = 4
P_HEADS = 8
N_KEYS = 128
P_TOPK = 16
VMEM_LIMIT = 56 * 1024 * 1024


def _nt(a, b):
    return lax.dot_general(a, b, (((1,), (1,)), ((), ())), preferred_element_type=F32)


def _tn(a, b):
    return lax.dot_general(a, b, (((0,), (0,)), ((), ())), preferred_element_type=F32)


def _mm(a, b):
    return jnp.dot(a, b, preferred_element_type=F32)


def _split_bf16(x, n):
    parts = []
    for _ in range(n - 1):
        p = x.astype(BF16)
        parts.append(p)
        x = x - p.astype(F32)
    parts.append(x.astype(BF16))
    return parts


def _mm3(a, b):
    ah, al = _split_bf16(a, 2)
    bh, bl = _split_bf16(b, 2)
    return _mm(ah, bh) + (_mm(ah, bl) + _mm(al, bh))


def _cumsum_mm(tri, g, g_is_lhs):
    t16 = tri.astype(BF16)
    out = None
    for p in reversed(_split_bf16(g, 3)):
        term = _mm(p, t16) if g_is_lhs else _mm(t16, p)
        out = term if out is None else out + term
    return out


def _sigmoid(x):
    return 1.0 / (1.0 + jnp.exp(-x))


def _softplus(x):
    return jnp.maximum(x, 0.0) + jnp.log1p(jnp.exp(-jnp.abs(x)))


def _bcast_rows(x, idx):
    n = x.shape[1]
    return jnp.concatenate(
        [jnp.broadcast_to(x[r:r + 1, :], (SUB, n)) for r in idx], axis=0)


def _inproj_body(x_ref, g_ref, w_ref, o_ref):
    x = x_ref[...]
    ms = jnp.mean(x * x, axis=-1, keepdims=True)
    xn = (x * lax.rsqrt(ms + EPS) * g_ref[...]).astype(BF16)
    o_ref[...] = _mm(xn, w_ref[...])


def _in_proj(x2, g, w, tm, tn):
    t, d = x2.shape
    n = w.shape[1]
    return pl.pallas_call(
        _inproj_body,
        grid=(n // tn, t // tm),
        in_specs=[
            pl.BlockSpec((tm, d), lambda j, i: (i, 0)),
            pl.BlockSpec((1, d), lambda j, i: (0, 0)),
            pl.BlockSpec((d, tn), lambda j, i: (0, j)),
        ],
        out_specs=pl.BlockSpec((tm, tn), lambda j, i: (i, j)),
        out_shape=jax.ShapeDtypeStruct((t, n), F32),
        compiler_params=pltpu.CompilerParams(
            dimension_semantics=("parallel", "parallel"),
            vmem_limit_bytes=VMEM_LIMIT),
        name="in_proj",
    )(x2, g, w)


def _hgrn_body(q_ref, f_ref, i_ref, g_ref, lb_ref, gn_ref, o_ref, st_ref, *, ts):
    @pl.when(pl.program_id(1) == 0)
    def _():
        st_ref[...] = jnp.zeros_like(st_ref)

    row = lax.broadcasted_iota(jnp.int32, (CHUNK, CHUNK), 0)
    col = lax.broadcasted_iota(jnp.int32, (CHUNK, CHUNK), 1)
    tril_f = (col <= row).astype(F32)
    blk_r = row // SUB
    blk_c = col // SUB
    rblk = lax.broadcasted_iota(jnp.int32, (CHUNK, HD), 0) // SUB
    nsub = CHUNK // SUB
    gn = gn_ref[...]

    def chunk_body(c, carry):
        r0 = pl.multiple_of(c * CHUNK, CHUNK)
        rows = pl.ds(r0, CHUNK)
        hs = range(A_HEADS)
        cols = [slice(h * HD, (h + 1) * HD) for h in hs]
        q = [q_ref[0, rows, cols[h]] for h in hs]
        z = [f_ref[0, rows, cols[h]] for h in hs]
        v16 = [i_ref[0, rows, cols[h]].astype(BF16) for h in hs]
        lb = [lb_ref[h:h + 1, :] for h in hs]
        sts = [st_ref[h] for h in hs]
        k = [(1.0 - lb[h]) * _sigmoid(-z[h]) for h in hs]
        b = [_cumsum_mm(tril_f, jnp.log(lb[h] + (1.0 - lb[h]) * _sigmoid(z[h])), False)
             for h in hs]
        o = [_nt((q[h] * jnp.exp(b[h])).astype(BF16), sts[h].astype(BF16)) for h in hs]
        att = []
        for h in hs:
            bnd = [b[h][SUB * j + SUB - 1:SUB * j + SUB, :] for j in range(nsub)]
            k_off = k[h] * jnp.exp(
                _bcast_rows(b[h], [SUB * j + SUB - 1 for j in range(nsub)]) - b[h])
            qcat = jnp.concatenate(
                [q[h] * jnp.exp(jnp.minimum(b[h] - bnd[j], 0.0)) for j in range(nsub - 1)],
                axis=1)
            kcat = jnp.concatenate(
                [jnp.where(rblk == j, k_off, 0.0) for j in range(nsub - 1)], axis=1)
            att_off = _nt(qcat.astype(BF16), kcat.astype(BF16))
            ref_rows = _bcast_rows(b[h], [SUB * i for i in range(nsub)])
            q_d = q[h] * jnp.exp(b[h] - ref_rows)
            k_d = k[h] * jnp.exp(ref_rows - b[h])
            att_d = _nt(q_d.astype(BF16), k_d.astype(BF16))
            att.append(jnp.where(blk_r > blk_c, att_off,
                                 jnp.where((blk_r == blk_c) & (col <= row), att_d, 0.0)))
        o = [o[h] + _mm(att[h].astype(BF16), v16[h]) for h in hs]
        new_sts = []
        for h in hs:
            b_last = b[h][CHUNK - 1:CHUNK, :]
            kdec = k[h] * jnp.exp(b_last - b[h])
            new_sts.append(sts[h] * jnp.exp(b_last) + _tn(v16[h], kdec.astype(BF16)))
        for h in hs:
            st_ref[h] = new_sts[h]
        for h in hs:
            gate = g_ref[0, rows, cols[h]]
            y = o[h] * lax.rsqrt(jnp.mean(o[h] * o[h], axis=-1, keepdims=True) + EPS)
            o_ref[0, rows, cols[h]] = (y * gn * (gate * _sigmoid(gate))).astype(o_ref.dtype)
        return carry

    lax.fori_loop(0, ts // CHUNK, chunk_body, 0)


def _hgrn(proj3, lb, gn, ts):
    b, s, _ = proj3.shape
    w = A_HEADS * HD
    return pl.pallas_call(
        functools.partial(_hgrn_body, ts=ts),
        grid=(b, s // ts),
        in_specs=[
            pl.BlockSpec((1, ts, w), lambda bi, ti: (bi, ti, 0)),
            pl.BlockSpec((1, ts, w), lambda bi, ti: (bi, ti, 1)),
            pl.BlockSpec((1, ts, w), lambda bi, ti: (bi, ti, 2)),
            pl.BlockSpec((1, ts, w), lambda bi, ti: (bi, ti, 3)),
            pl.BlockSpec((A_HEADS, HD), lambda bi, ti: (0, 0)),
            pl.BlockSpec((1, HD), lambda bi, ti: (0, 0)),
        ],
        out_specs=pl.BlockSpec((1, ts, w), lambda bi, ti: (bi, ti, 0)),
        out_shape=jax.ShapeDtypeStruct((b, s, w), BF16),
        scratch_shapes=[pltpu.VMEM((A_HEADS, HD, HD), F32)],
        compiler_params=pltpu.CompilerParams(
            dimension_semantics=("parallel", "arbitrary"),
            vmem_limit_bytes=VMEM_LIMIT),
        name="hgrn",
    )(proj3, proj3, proj3, proj3, lb, gn)


def _gdn_body(bq_ref, bk_ref, bv_ref, bz_ref, sm_ref, smt_ref, cw_ref, pcol_ref, prow_ref,
              gn_ref, o_ref, s_ref, tail_ref, qn_ref, kn_ref, vv_ref,
              u_s, w_s, qg_s, kd_s, aqk_s, dl_s, *, ts):
    nqk = B_QK_HEADS * HD

    @pl.when(pl.program_id(1) == 0)
    def _():
        s_ref[...] = jnp.zeros_like(s_ref)
        tail_ref[...] = jnp.zeros_like(tail_ref)

    def conv_silu(x, c0):
        n = x.shape[1]
        xe = jnp.concatenate([tail_ref[:, c0:c0 + n], x], axis=0)
        w = cw_ref[:, c0:c0 + n]
        y = x * w[CONV - 1:CONV, :]
        for j in range(1, CONV):
            y = y + pltpu.roll(xe, j, axis=0)[8:, :] * w[CONV - 1 - j:CONV - j, :]
        tail_ref[:, c0:c0 + n] = x[ts - 8:, :]
        return y * _sigmoid(y)

    def l2n(x):
        return x * lax.rsqrt(jnp.sum(x * x, axis=-1, keepdims=True) + EPS)

    yq = conv_silu(bq_ref[0], 0)
    yk = conv_silu(bk_ref[0], nqk)
    vv_ref[...] = conv_silu(bv_ref[0], 2 * nqk)
    for h in range(B_QK_HEADS):
        cs = slice(h * HD, (h + 1) * HD)
        qn_ref[:, cs] = l2n(yq[:, cs]) * (HD ** -0.5)
        kn_ref[:, cs] = l2n(yk[:, cs])

    row = lax.broadcasted_iota(jnp.int32, (CHUNK, CHUNK), 0)
    col = lax.broadcasted_iota(jnp.int32, (CHUNK, CHUNK), 1)
    incl = col <= row
    strict = col < row
    tril_f = incl.astype(F32)
    triu_f = (row <= col).astype(F32)
    same_blk = (row // SUB) == (col // SUB)
    eye = (row == col).astype(F32)
    gn = gn_ref[...]
    a_col = pcol_ref[0:1, :]
    dt_col = pcol_ref[1:2, :]
    a_row = prow_ref[:, 0:1]
    dt_row = prow_ref[:, 1:2]
    rep = B_V_HEADS // B_QK_HEADS

    def prep_chunk(c, carry):
        r0 = pl.multiple_of(c * CHUNK, CHUNK)
        rows = pl.ds(r0, CHUNK)
        sm = sm_ref[0, rows, :]
        beta_c = _sigmoid(sm)
        gc_c = _cumsum_mm(tril_f, -a_col * _softplus(sm + dt_col), False)
        smt = smt_ref[0, c]
        gc_r = _cumsum_mm(triu_f, -a_row * _softplus(smt + dt_row), True)
        hs = range(B_V_HEADS)
        qq = [qn_ref[rows, (h // rep) * HD:(h // rep + 1) * HD] for h in hs]
        kk = [kn_ref[rows, (h // rep) * HD:(h // rep + 1) * HD] for h in hs]
        kk16 = [k.astype(BF16) for k in kk]
        bcol = [beta_c[:, h:h + 1] for h in hs]
        gcol = [gc_c[:, B_V_HEADS + h:B_V_HEADS + h + 1] for h in hs]
        decay = [jnp.where(incl, jnp.exp(jnp.minimum(
            gcol[h] - gc_r[B_V_HEADS + h:B_V_HEADS + h + 1, :], 0.0)), 0.0) for h in hs]
        kb = [kk[h] * bcol[h] for h in hs]
        a = [jnp.where(strict, _nt(kb[h].astype(BF16), kk16[h]) * decay[h], 0.0) for h in hs]
        dg = [jnp.where(same_blk, a[h], 0.0) for h in hs]
        off = [a[h] - dg[h] for h in hs]
        tinv = [eye - dg[h] for h in hs]
        p = dg
        for _ in range(3):
            p = [_mm3(p[h], p[h]) for h in hs]
            tinv = [tinv[h] + _mm3(tinv[h], p[h]) for h in hs]
        pm = [_mm3(tinv[h], off[h]) for h in hs]
        pm2 = [_mm3(pm[h], pm[h]) for h in hs]
        egc = [jnp.exp(gcol[h]) for h in hs]
        x = [_mm3(tinv[h], jnp.concatenate(
            [vv_ref[rows, h * HD:(h + 1) * HD] * bcol[h], kb[h] * egc[h]], axis=1)) for h in hs]
        x = [x[h] + _mm3(pm2[h], x[h]) for h in hs]
        x = [x[h] - _mm3(pm[h], x[h]) for h in hs]
        aqk = [jnp.where(incl, _nt(qq[h].astype(BF16), kk16[h]) * decay[h], 0.0) for h in hs]
        for h in hs:
            cs = slice(h * HD, (h + 1) * HD)
            g_last = gcol[h][CHUNK - 1:CHUNK, :]
            u_s[rows, cs] = x[h][:, :HD]
            w_s[rows, cs] = x[h][:, HD:].astype(BF16)
            qg_s[rows, cs] = (qq[h] * egc[h]).astype(BF16)
            kd_s[rows, cs] = (kk[h] * jnp.exp(g_last - gcol[h])).astype(BF16)
            aqk_s[c, h] = aqk[h].astype(BF16)
            dl_s[c, h:h + 1, :] = jnp.broadcast_to(jnp.exp(g_last), (1, HD))
        return carry

    lax.fori_loop(0, ts // CHUNK, prep_chunk, 0)

    def scan_chunk(c, carry):
        r0 = pl.multiple_of(c * CHUNK, CHUNK)
        rows = pl.ds(r0, CHUNK)
        hs = range(B_V_HEADS)
        cols = [slice(h * HD, (h + 1) * HD) for h in hs]
        sts = [s_ref[h] for h in hs]
        st16 = [s.astype(BF16) for s in sts]
        v16 = [(u_s[rows, cols[h]] - _mm(w_s[rows, cols[h]], st16[h])).astype(BF16) for h in hs]
        o = [_mm(qg_s[rows, cols[h]], st16[h]) + _mm(aqk_s[c, h], v16[h]) for h in hs]
        new_sts = [sts[h] * dl_s[c, h:h + 1, :] + _tn(kd_s[rows, cols[h]], v16[h]) for h in hs]
        for h in hs:
            s_ref[h] = new_sts[h]
        for h in hs:
            zz = bz_ref[0, rows, cols[h]]
            y = o[h] * lax.rsqrt(jnp.mean(o[h] * o[h], axis=-1, keepdims=True) + EPS)
            o_ref[0, rows, cols[h]] = (y * gn * (zz * _sigmoid(zz))).astype(o_ref.dtype)
        return carry

    lax.fori_loop(0, ts // CHUNK, scan_chunk, 0)


def _gdn(proj3, smt, conv_w, pcol, prow, gn, ts):
    b, s, _ = proj3.shape
    nqk = B_QK_HEADS * HD
    wv = B_V_HEADS * HD
    off_q = (4 * A_HEADS * HD) // nqk
    off_v = (4 * A_HEADS * HD + 2 * nqk) // wv
    off_s = (4 * A_HEADS * HD + 2 * nqk + 2 * wv) // HD
    nch = ts // CHUNK
    return pl.pallas_call(
        functools.partial(_gdn_body, ts=ts),
        grid=(b, s // ts),
        in_specs=[
            pl.BlockSpec((1, ts, nqk), lambda bi, ti: (bi, ti, off_q)),
            pl.BlockSpec((1, ts, nqk), lambda bi, ti: (bi, ti, off_q + 1)),
            pl.BlockSpec((1, ts, wv), lambda bi, ti: (bi, ti, off_v)),
            pl.BlockSpec((1, ts, wv), lambda bi, ti: (bi, ti, off_v + 1)),
            pl.BlockSpec((1, ts, HD), lambda bi, ti: (bi, ti, off_s)),
            pl.BlockSpec((1, nch, 2 * B_V_HEADS, CHUNK), lambda bi, ti: (bi, ti, 0, 0)),
            pl.BlockSpec((CONV, 2 * nqk + wv), lambda bi, ti: (0, 0)),
            pl.BlockSpec((2, HD), lambda bi, ti: (0, 0)),
            pl.BlockSpec((2 * B_V_HEADS, HD), lambda bi, ti: (0, 0)),
            pl.BlockSpec((1, HD), lambda bi, ti: (0, 0)),
        ],
        out_specs=pl.BlockSpec((1, ts, wv), lambda bi, ti: (bi, ti, 0)),
        out_shape=jax.ShapeDtypeStruct((b, s, wv), BF16),
        scratch_shapes=[
            pltpu.VMEM((B_V_HEADS, HD, HD), F32),
            pltpu.VMEM((8, 2 * nqk + wv), F32),
            pltpu.VMEM((ts, nqk), F32),
            pltpu.VMEM((ts, nqk), F32),
            pltpu.VMEM((ts, wv), F32),
            pltpu.VMEM((ts, wv), F32),
            pltpu.VMEM((ts, wv), BF16),
            pltpu.VMEM((ts, wv), BF16),
            pltpu.VMEM((ts, wv), BF16),
            pltpu.VMEM((nch, B_V_HEADS, CHUNK, CHUNK), BF16),
            pltpu.VMEM((nch, B_V_HEADS, HD), F32),
        ],
        compiler_params=pltpu.CompilerParams(
            dimension_semantics=("parallel", "arbitrary"),
            vmem_limit_bytes=VMEM_LIMIT),
        name="gdn",
    )(proj3, proj3, proj3, proj3, proj3, smt, conv_w, pcol, prow, gn)


def _outproj_body(x_ref, oa_ref, ob_ref, wo_ref, fg_ref, wq_ref, k1_ref, k2_ref,
                  h_ref, hn_ref, s1_ref, s2_ref):
    wa = oa_ref.shape[1]
    h = x_ref[...] + _mm(oa_ref[...], wo_ref[:wa, :]) + _mm(ob_ref[...], wo_ref[wa:, :])
    h_ref[...] = h
    hn = h * lax.rsqrt(jnp.mean(h * h, axis=-1, keepdims=True) + EPS) * fg_ref[...]
    hn_ref[...] = hn
    qry = _mm(hn.astype(BF16), wq_ref[...])
    for hh in range(P_HEADS):
        q1 = qry[:, hh * 2 * HD:hh * 2 * HD + HD].astype(BF16)
        q2 = qry[:, hh * 2 * HD + HD:(hh + 1) * 2 * HD].astype(BF16)
        s1_ref[hh] = _nt(k1_ref[hh].astype(BF16), q1)
        s2_ref[hh] = _nt(k2_ref[hh].astype(BF16), q2)


def _out_proj(x2, oa, ob, wo, fg, wq, k1, k2, tm):
    t, d = x2.shape
    wa = oa.shape[1]
    wb = ob.shape[1]
    const = dict(pipeline_mode=pl.Buffered(1))
    return pl.pallas_call(
        _outproj_body,
        grid=(t // tm,),
        in_specs=[
            pl.BlockSpec((tm, d), lambda i: (i, 0)),
            pl.BlockSpec((tm, wa), lambda i: (i, 0)),
            pl.BlockSpec((tm, wb), lambda i: (i, 0)),
            pl.BlockSpec((wa + wb, d), lambda i: (0, 0), **const),
            pl.BlockSpec((1, d), lambda i: (0, 0)),
            pl.BlockSpec((d, P_HEADS * 2 * HD), lambda i: (0, 0), **const),
            pl.BlockSpec((P_HEADS, N_KEYS, HD), lambda i: (0, 0, 0)),
            pl.BlockSpec((P_HEADS, N_KEYS, HD), lambda i: (0, 0, 0)),
        ],
        out_specs=[
            pl.BlockSpec((tm, d), lambda i: (i, 0)),
            pl.BlockSpec((tm, d), lambda i: (i, 0)),
            pl.BlockSpec((P_HEADS, N_KEYS, tm), lambda i: (0, 0, i)),
            pl.BlockSpec((P_HEADS, N_KEYS, tm), lambda i: (0, 0, i)),
        ],
        out_shape=[
            jax.ShapeDtypeStruct((t, d), F32),
            jax.ShapeDtypeStruct((t, d), F32),
            jax.ShapeDtypeStruct((P_HEADS, N_KEYS, t), F32),
            jax.ShapeDtypeStruct((P_HEADS, N_KEYS, t), F32),
        ],
        compiler_params=pltpu.CompilerParams(
            dimension_semantics=("parallel",),
            vmem_limit_bytes=VMEM_LIMIT),
        name="out_proj",
    )(x2, oa, ob, wo, fg, wq, k1, k2)


_PAIRS = [(i, j) for i in range(P_TOPK) for j in range(P_TOPK) if (i + 1) * (j + 1) <= P_TOPK]
_NCAND = -(-len(_PAIRS) // 8) * 8


def _topk_body(s1_ref, s2_ref, pos_ref, e_ref, g_ref, v_scr, i_scr, c_scr, ce_scr, b_scr, x_scr):
    tt = s1_ref.shape[2]
    kio = lax.broadcasted_iota(jnp.int32, (N_KEYS, tt), 0).astype(F32)
    neg = -jnp.inf
    for half, sref in ((0, s1_ref), (1, s2_ref)):
        s = sref[0]
        for r in range(P_TOPK):
            m = jnp.max(s, axis=0, keepdims=True)
            idx = jnp.min(jnp.where(s == m, kio, float(N_KEYS)), axis=0, keepdims=True)
            v_scr[half, r:r + 1, :] = m
            i_scr[half, r:r + 1, :] = idx
            s = jnp.where(kio == idx, neg, s)
    c_scr[...] = jnp.full(c_scr.shape, neg, F32)
    ce_scr[...] = jnp.zeros(ce_scr.shape, F32)
    for c, (i, j) in enumerate(_PAIRS):
        c_scr[c:c + 1, :] = v_scr[0, i:i + 1, :] + v_scr[1, j:j + 1, :]
        ce_scr[c:c + 1, :] = i_scr[0, i:i + 1, :] * float(N_KEYS) + i_scr[1, j:j + 1, :]
    cand = c_scr[...]
    ce = ce_scr[...]
    pos = jnp.broadcast_to(pos_ref[:, 0:1], cand.shape)
    for r in range(P_TOPK):
        m = jnp.max(cand, axis=0, keepdims=True)
        sel = jnp.min(jnp.where(cand == m, pos, 1e9), axis=0, keepdims=True)
        hit = pos == sel
        b_scr[r:r + 1, :] = m
        x_scr[r:r + 1, :] = jnp.max(jnp.where(hit, ce, -1.0), axis=0, keepdims=True)
        cand = jnp.where(hit, neg, cand)
    best = b_scr[...]
    ex = jnp.exp(best - best[0:1, :])
    g_ref[...] = ex / jnp.sum(ex, axis=0, keepdims=True)
    e_ref[...] = x_scr[...].astype(jnp.int32)


def _peer_topk(s1, s2, pos, tt):
    t = s1.shape[2]
    return pl.pallas_call(
        _topk_body,
        grid=(t // tt, P_HEADS),
        in_specs=[
            pl.BlockSpec((1, N_KEYS, tt), lambda i, h: (h, 0, i)),
            pl.BlockSpec((1, N_KEYS, tt), lambda i, h: (h, 0, i)),
            pl.BlockSpec((_NCAND, HD), lambda i, h: (0, 0)),
        ],
        out_specs=[
            pl.BlockSpec((P_TOPK, tt), lambda i, h: (h, i)),
            pl.BlockSpec((P_TOPK, tt), lambda i, h: (h, i)),
        ],
        out_shape=[
            jax.ShapeDtypeStruct((P_HEADS * P_TOPK, t), jnp.int32),
            jax.ShapeDtypeStruct((P_HEADS * P_TOPK, t), F32),
        ],
        scratch_shapes=[
            pltpu.VMEM((2, P_TOPK, tt), F32),
            pltpu.VMEM((2, P_TOPK, tt), F32),
            pltpu.VMEM((_NCAND, tt), F32),
            pltpu.VMEM((_NCAND, tt), F32),
            pltpu.VMEM((P_TOPK, tt), F32),
            pltpu.VMEM((P_TOPK, tt), F32),
        ],
        compiler_params=pltpu.CompilerParams(
            dimension_semantics=("parallel", "parallel"),
            vmem_limit_bytes=VMEM_LIMIT),
        name="peer_topk",
    )(s1, s2, pos)


PEER_TB = 128
PEER_G = 8
PEER_SETS = 3
NSLOT = P_HEADS * P_TOPK
SG = 8


def _gelu(x):
    return 0.5 * x * (1.0 + lax.erf(x * (2.0 ** -0.5)))


def _peer_body(idx_ref, gt_ref, hn_ref, h_ref, fg_ref, w_hbm, o_ref,
               buf, hid_scr, c_scr, y_scr, sem, *, final, nsteps):
    tb, d = hn_ref.shape
    ngroups = tb // PEER_G
    nlc = d // HD
    nsg = NSLOT // SG
    per_tile = NSLOT // 2 // nsg
    ahead = PEER_SETS - 1
    step = pl.program_id(0)

    def issue_tok(tok, set_, r, s0, s1):
        ids = idx_ref.at[tok]
        dst = buf.at[set_, pl.ds(r * nsg, nsg)]
        for s in range(s0, s1):
            pltpu.make_async_copy(w_hbm.at[ids[s], pl.ds(0, 1), :],
                                  dst.at[s // SG, pl.ds(s % SG, 1), :],
                                  sem.at[set_]).start()

    def wait_set(set_):
        pltpu.make_async_copy(w_hbm.at[pl.ds(0, PEER_G * nsg)], buf.at[set_],
                              sem.at[set_]).wait()

    @pl.when(step == 0)
    def _():
        def first(r, carry):
            for a in range(ahead):
                issue_tok(a * PEER_G + r, a, r, 0, NSLOT)
            return carry
        lax.fori_loop(0, PEER_G, first, 0)

    lane = lax.broadcasted_iota(jnp.int32, (SG, tb), 1)
    hid_scr[...] = jnp.zeros_like(hid_scr)

    def group_body(g, carry):
        gg = step * ngroups + g
        set_ = gg % PEER_SETS
        nxt = (gg + ahead) % PEER_SETS
        wait_set(set_)


        def hid_tok(r, c2):
            t = g * PEER_G + r
            xb = jnp.broadcast_to(hn_ref[pl.ds(t, 1), :], (SG, d))
            tiles = buf.at[set_, pl.ds(r * nsg, nsg)]
            for sg in range(nsg):
                issue_tok((g + ahead) * PEER_G + r, nxt, r, sg * per_tile, (sg + 1) * per_tile)
                w = tiles[sg]
                acc = None
                for lc in range(nlc):
                    u = lax.bitcast_convert_type(w[:, lc * HD:(lc + 1) * HD] << 16, F32)
                    term = u * xb[:, lc * HD:(lc + 1) * HD]
                    acc = term if acc is None else acc + term
                hs = jnp.sum(acc, axis=1, keepdims=True)
                sl = pl.ds(sg * SG, SG)
                hid_scr[sl, :] = jnp.where(lane == t, hs, hid_scr[sl, :])
            return c2
        lax.fori_loop(0, PEER_G, hid_tok, 0)

        c_scr[...] = gt_ref[...] * _gelu(hid_scr[...])

        def mix_tok(r, c2):
            t = g * PEER_G + r
            accs = [jnp.zeros((SG, HD), F32) for _ in range(nlc)]
            tiles = buf.at[set_, pl.ds(r * nsg, nsg)]
            for sg in range(nsg):
                issue_tok((g + ahead) * PEER_G + r, nxt, r,
                          NSLOT // 2 + sg * per_tile, NSLOT // 2 + (sg + 1) * per_tile)
                sl = pl.ds(sg * SG, SG)
                cc = jnp.sum(jnp.where(lane == t, c_scr[sl, :], 0.0), axis=1, keepdims=True)
                w = tiles[sg]
                for lc in range(nlc):
                    v = lax.bitcast_convert_type(
                        w[:, lc * HD:(lc + 1) * HD] & jnp.uint32(0xFFFF0000), F32)
                    accs[lc] = accs[lc] + v * cc
            yrow = jnp.concatenate(
                [jnp.sum(a, axis=0, keepdims=True) for a in accs], axis=1)
            y_scr[pl.ds(t, 1), :] = yrow
            return c2
        lax.fori_loop(0, PEER_G, mix_tok, 0)
        return carry

    lax.fori_loop(0, ngroups, group_body, 0)

    @pl.when(step == nsteps - 1)
    def _():
        for a in range(ahead):
            wait_set((nsteps * ngroups + a) % PEER_SETS)

    hh = h_ref[...] + y_scr[...]
    if final:
        hh = hh * lax.rsqrt(jnp.mean(hh * hh, axis=-1, keepdims=True) + EPS) * fg_ref[...]
    o_ref[...] = hh


def _peer_mix(idx, gt, hn, h, fg, w_tab, final):
    t, d = hn.shape
    tb = PEER_TB
    nsteps = t // tb
    extra = (PEER_SETS - 1) * PEER_G
    idx_pad = jnp.concatenate([idx, jnp.zeros((tb, NSLOT), idx.dtype)], axis=0)
    idx_ext = jnp.concatenate(
        [idx.reshape(nsteps, tb, NSLOT),
         idx_pad[tb:].reshape(nsteps, tb, NSLOT)[:, :extra]], axis=1)
    return pl.pallas_call(
        functools.partial(_peer_body, final=final, nsteps=nsteps),
        grid=(nsteps,),
        in_specs=[
            pl.BlockSpec((None, tb + extra, NSLOT), lambda i: (i, 0, 0),
                         memory_space=pltpu.SMEM),
            pl.BlockSpec((NSLOT, tb), lambda i: (0, i)),
            pl.BlockSpec((tb, d), lambda i: (i, 0)),
            pl.BlockSpec((tb, d), lambda i: (i, 0)),
            pl.BlockSpec((1, d), lambda i: (0, 0)),
            pl.BlockSpec(memory_space=pl.ANY),
        ],
        out_specs=pl.BlockSpec((tb, d), lambda i: (i, 0)),
        out_shape=jax.ShapeDtypeStruct((t, d), F32),
        scratch_shapes=[
            pltpu.VMEM((PEER_SETS, PEER_G * NSLOT // SG, SG, d), jnp.uint32),
            pltpu.VMEM((NSLOT, tb), F32),
            pltpu.VMEM((NSLOT, tb), F32),
            pltpu.VMEM((tb, d), F32),
            pltpu.SemaphoreType.DMA((PEER_SETS,)),
        ],
        compiler_params=pltpu.CompilerParams(
            dimension_semantics=("arbitrary",),
            vmem_limit_bytes=VMEM_LIMIT),
        name="peer_mix",
    )(idx_ext, gt, hn, h, fg, w_tab)


def _tile(n, pref):
    return pref if n % pref == 0 else n


def kernel(x, attn_norm_g, w_in, hgrn_lb_logits, hgrn_norm_g, gdn_conv_w, gdn_A_log, gdn_dt_bias,
           gdn_norm_g, w_out, ffn_norm_g, peer_w_query, peer_sub_keys, peer_u, peer_v,
           final_norm_g):
    b, s, d = x.shape
    t = b * s
    depth = w_in.shape[0]
    in_width = w_in.shape[2]
    n_small = 2 * B_V_HEADS
    n_main = in_width - n_small
    in_pad = n_main + HD
    lb_all = jnp.cumsum(jax.nn.softmax(hgrn_lb_logits.astype(F32), axis=0), axis=0)
    pos = jnp.array([i * P_TOPK + j for i, j in _PAIRS]
                    + [10 ** 6 + c for c in range(_NCAND - len(_PAIRS))], F32)
    pos = jnp.broadcast_to(pos[:, None], (_NCAND, HD))

    h2 = x.reshape(t, d)
    for l in range(depth):
        w_l = jnp.pad(w_in[l], ((0, 0), (0, in_pad - in_width))).astype(BF16)
        tn = in_pad // 3 if (in_pad // HD) % 3 == 0 else in_pad
        proj = _in_proj(h2, attn_norm_g[l][None, :], w_l, _tile(t, 512), tn)
        proj3 = proj.reshape(b, s, in_pad)

        o_a = _hgrn(proj3, lb_all[l], hgrn_norm_g[l][None, :], _tile(s, 256))

        ts = _tile(s, 256)
        small = proj3[:, :, n_main:n_main + n_small]
        smt = small.reshape(b, s // CHUNK, CHUNK, n_small).transpose(0, 1, 3, 2)
        zeros8 = jnp.zeros((B_V_HEADS,), F32)
        a_neg = jnp.exp(gdn_A_log[l].astype(F32))
        dtb = gdn_dt_bias[l].astype(F32)
        pcol = jnp.zeros((2, HD), F32)
        pcol = pcol.at[0, B_V_HEADS:n_small].set(a_neg).at[1, B_V_HEADS:n_small].set(dtb)
        prow = jnp.zeros((n_small, HD), F32)
        prow = prow.at[:, 0].set(jnp.concatenate([zeros8, a_neg]))
        prow = prow.at[:, 1].set(jnp.concatenate([zeros8, dtb]))
        o_b = _gdn(proj3, smt, gdn_conv_w[l], pcol, prow, gdn_norm_g[l][None, :], ts)

        wq = peer_w_query[l].reshape(d, P_HEADS * 2 * HD).astype(BF16)
        h2, hn, s1, s2 = _out_proj(
            h2, o_a.reshape(t, -1), o_b.reshape(t, -1), w_out[l].astype(BF16),
            ffn_norm_g[l][None, :], wq, peer_sub_keys[l, 0], peer_sub_keys[l, 1], _tile(t, 256))

        e_t, g_t = _peer_topk(s1, s2, pos, _tile(t, 256))
        ub = lax.bitcast_convert_type(peer_u[l].astype(BF16), jnp.uint16).astype(jnp.uint32)
        vb = lax.bitcast_convert_type(peer_v[l].astype(BF16), jnp.uint16).astype(jnp.uint32)
        w_tab = jnp.pad(((vb << 16) | ub)[:, None, :], ((0, 0), (0, SG - 1), (0, 0)))
        h2 = _peer_mix(e_t.T, g_t, hn, h2, final_norm_g[None, :], w_tab, l == depth - 1)
    return h2.reshape(b, s, d)
```

```python
import functools

import jax
import jax.numpy as jnp
from jax import lax
from jax.experimental import pallas as pl
from jax.experimental.pallas import tpu as pltpu

F32 = jnp.float32
BF16 = jnp.bfloat16

EPS = 1e-6
CHUNK = 64
SUB = 16
HD = 128
A_HEADS = 8
B_QK_HEADS = 4
B_V_HEADS = 8
CONV = 4
P_HEADS = 8
N_KEYS = 128
P_TOPK = 16
VMEM_LIMIT = 56 * 1024 * 1024


def _nt(a, b):
    return lax.dot_general(a, b, (((1,), (1,)), ((), ())), preferred_element_type=F32)


def _tn(a, b):
    return lax.dot_general(a, b, (((0,), (0,)), ((), ())), preferred_element_type=F32)


def _mm(a, b):
    return jnp.dot(a, b, preferred_element_type=F32)


def _split_bf16(x, n):
    parts = []
    for _ in range(n - 1):
        p = x.astype(BF16)
        parts.append(p)
        x = x - p.astype(F32)
    parts.append(x.astype(BF16))
    return parts


def _mm3(a, b):
    ah, al = _split_bf16(a, 2)
    bh, bl = _split_bf16(b, 2)
    return _mm(ah, bh) + (_mm(ah, bl) + _mm(al, bh))


def _cumsum_mm(tri, g, g_is_lhs):
    t16 = tri.astype(BF16)
    out = None
    for p in reversed(_split_bf16(g, 3)):
        term = _mm(p, t16) if g_is_lhs else _mm(t16, p)
        out = term if out is None else out + term
    return out


def _sigmoid(x):
    return 1.0 / (1.0 + jnp.exp(-x))


def _softplus(x):
    return jnp.maximum(x, 0.0) + jnp.log1p(jnp.exp(-jnp.abs(x)))


def _bcast_rows(x, idx):
    n = x.shape[1]
    return jnp.concatenate(
        [jnp.broadcast_to(x[r:r + 1, :], (SUB, n)) for r in idx], axis=0)


def _inproj_body(x_ref, g_ref, w_ref, o_ref):
    x = x_ref[...]
    ms = jnp.mean(x * x, axis=-1, keepdims=True)
    xn = (x * lax.rsqrt(ms + EPS) * g_ref[...]).astype(BF16)
    o_ref[...] = _mm(xn, w_ref[...])


def _in_proj(x2, g, w, tm, tn):
    t, d = x2.shape
    n = w.shape[1]
    return pl.pallas_call(
        _inproj_body,
        grid=(n // tn, t // tm),
        in_specs=[
            pl.BlockSpec((tm, d), lambda j, i: (i, 0)),
            pl.BlockSpec((1, d), lambda j, i: (0, 0)),
            pl.BlockSpec((d, tn), lambda j, i: (0, j)),
        ],
        out_specs=pl.BlockSpec((tm, tn), lambda j, i: (i, j)),
        out_shape=jax.ShapeDtypeStruct((t, n), F32),
        compiler_params=pltpu.CompilerParams(
            dimension_semantics=("parallel", "parallel"),
            vmem_limit_bytes=VMEM_LIMIT),
        name="in_proj",
    )(x2, g, w)


def _hgrn_body(q_ref, f_ref, i_ref, g_ref, lb_ref, gn_ref, o_ref, st_ref, *, ts):
    @pl.when(pl.program_id(1) == 0)
    def _():
        st_ref[...] = jnp.zeros_like(st_ref)

    row = lax.broadcasted_iota(jnp.int32, (CHUNK, CHUNK), 0)
    col = lax.broadcasted_iota(jnp.int32, (CHUNK, CHUNK), 1)
    tril_f = (col <= row).astype(F32)
    blk_r = row // SUB
    blk_c = col // SUB
    rblk = lax.broadcasted_iota(jnp.int32, (CHUNK, HD), 0) // SUB
    nsub = CHUNK // SUB
    gn = gn_ref[...]

    def chunk_body(c, carry):
        r0 = pl.multiple_of(c * CHUNK, CHUNK)
        rows = pl.ds(r0, CHUNK)
        hs = range(A_HEADS)
        cols = [slice(h * HD, (h + 1) * HD) for h in hs]
        q = [q_ref[0, rows, cols[h]] for h in hs]
        z = [f_ref[0, rows, cols[h]] for h in hs]
        v16 = [i_ref[0, rows, cols[h]].astype(BF16) for h in hs]
        lb = [lb_ref[h:h + 1, :] for h in hs]
        sts = [st_ref[h] for h in hs]
        k = [(1.0 - lb[h]) * _sigmoid(-z[h]) for h in hs]
        b = [_cumsum_mm(tril_f, jnp.log(lb[h] + (1.0 - lb[h]) * _sigmoid(z[h])), False)
             for h in hs]
        o = [_nt((q[h] * jnp.exp(b[h])).astype(BF16), sts[h].astype(BF16)) for h in hs]
        att = []
        for h in hs:
            bnd = [b[h][SUB * j + SUB - 1:SUB * j + SUB, :] for j in range(nsub)]
            k_off = k[h] * jnp.exp(
                _bcast_rows(b[h], [SUB * j + SUB - 1 for j in range(nsub)]) - b[h])
            qcat = jnp.concatenate(
                [q[h] * jnp.exp(jnp.minimum(b[h] - bnd[j], 0.0)) for j in range(nsub - 1)],
                axis=1)
            kcat = jnp.concatenate(
                [jnp.where(rblk == j, k_off, 0.0) for j in range(nsub - 1)], axis=1)
            att_off = _nt(qcat.astype(BF16), kcat.astype(BF16))
            ref_rows = _bcast_rows(b[h], [SUB * i for i in range(nsub)])
            q_d = q[h] * jnp.exp(b[h] - ref_rows)
            k_d = k[h] * jnp.exp(ref_rows - b[h])
            att_d = _nt(q_d.astype(BF16), k_d.astype(BF16))
            att.append(jnp.where(blk_r > blk_c, att_off,
                                 jnp.where((blk_r == blk_c) & (col <= row), att_d, 0.0)))
        o = [o[h] + _mm(att[h].astype(BF16), v16[h]) for h in hs]
        new_sts = []
        for h in hs:
            b_last = b[h][CHUNK - 1:CHUNK, :]
            kdec = k[h] * jnp.exp(b_last - b[h])
            new_sts.append(sts[h] * jnp.exp(b_last) + _tn(v16[h], kdec.astype(BF16)))
        for h in hs:
            st_ref[h] = new_sts[h]
        for h in hs:
            gate = g_ref[0, rows, cols[h]]
            y = o[h] * lax.rsqrt(jnp.mean(o[h] * o[h], axis=-1, keepdims=True) + EPS)
            o_ref[0, rows, cols[h]] = (y * gn * (gate * _sigmoid(gate))).astype(o_ref.dtype)
        return carry

    lax.fori_loop(0, ts // CHUNK, chunk_body, 0)


def _hgrn(proj3, lb, gn, ts):
    b, s, _ = proj3.shape
    w = A_HEADS * HD
    return pl.pallas_call(
        functools.partial(_hgrn_body, ts=ts),
        grid=(b, s // ts),
        in_specs=[
            pl.BlockSpec((1, ts, w), lambda bi, ti: (bi, ti, 0)),
            pl.BlockSpec((1, ts, w), lambda bi, ti: (bi, ti, 1)),
            pl.BlockSpec((1, ts, w), lambda bi, ti: (bi, ti, 2)),
            pl.BlockSpec((1, ts, w), lambda bi, ti: (bi, ti, 3)),
            pl.BlockSpec((A_HEADS, HD), lambda bi, ti: (0, 0)),
            pl.BlockSpec((1, HD), lambda bi, ti: (0, 0)),
        ],
        out_specs=pl.BlockSpec((1, ts, w), lambda bi, ti: (bi, ti, 0)),
        out_shape=jax.ShapeDtypeStruct((b, s, w), BF16),
        scratch_shapes=[pltpu.VMEM((A_HEADS, HD, HD), F32)],
        compiler_params=pltpu.CompilerParams(
            dimension_semantics=("parallel", "arbitrary"),
            vmem_limit_bytes=VMEM_LIMIT),
        name="hgrn",
    )(proj3, proj3, proj3, proj3, lb, gn)


def _gdn_body(bq_ref, bk_ref, bv_ref, bz_ref, sm_ref, smt_ref, cw_ref, pcol_ref, prow_ref,
              gn_ref, o_ref, s_ref, tail_ref, qn_ref, kn_ref, vv_ref,
              u_s, w_s, qg_s, kd_s, aqk_s, dl_s, *, ts):
    nqk = B_QK_HEADS * HD

    @pl.when(pl.program_id(1) == 0)
    def _():
        s_ref[...] = jnp.zeros_like(s_ref)
        tail_ref[...] = jnp.zeros_like(tail_ref)

    def conv_silu(x, c0):
        n = x.shape[1]
        xe = jnp.concatenate([tail_ref[:, c0:c0 + n], x], axis=0)
        w = cw_ref[:, c0:c0 + n]
        y = x * w[CONV - 1:CONV, :]
        for j in range(1, CONV):
            y = y + pltpu.roll(xe, j, axis=0)[8:, :] * w[CONV - 1 - j:CONV - j, :]
        tail_ref[:, c0:c0 + n] = x[ts - 8:, :]
        return y * _sigmoid(y)

    def l2n(x):
        return x * lax.rsqrt(jnp.sum(x * x, axis=-1, keepdims=True) + EPS)

    yq = conv_silu(bq_ref[0], 0)
    yk = conv_silu(bk_ref[0], nqk)
    vv_ref[...] = conv_silu(bv_ref[0], 2 * nqk)
    for h in range(B_QK_HEADS):
        cs = slice(h * HD, (h + 1) * HD)
        qn_ref[:, cs] = l2n(yq[:, cs]) * (HD ** -0.5)
        kn_ref[:, cs] = l2n(yk[:, cs])

    row = lax.broadcasted_iota(jnp.int32, (CHUNK, CHUNK), 0)
    col = lax.broadcasted_iota(jnp.int32, (CHUNK, CHUNK), 1)
    incl = col <= row
    strict = col < row
    tril_f = incl.astype(F32)
    triu_f = (row <= col).astype(F32)
    same_blk = (row // SUB) == (col // SUB)
    eye = (row == col).astype(F32)
    gn = gn_ref[...]
    a_col = pcol_ref[0:1, :]
    dt_col = pcol_ref[1:2, :]
    a_row = prow_ref[:, 0:1]
    dt_row = prow_ref[:, 1:2]
    rep = B_V_HEADS // B_QK_HEADS

    def prep_chunk(c, carry):
        r0 = pl.multiple_of(c * CHUNK, CHUNK)
        rows = pl.ds(r0, CHUNK)
        sm = sm_ref[0, rows, :]
        beta_c = _sigmoid(sm)
        gc_c = _cumsum_mm(tril_f, -a_col * _softplus(sm + dt_col), False)
        smt = smt_ref[0, c]
        gc_r = _cumsum_mm(triu_f, -a_row * _softplus(smt + dt_row), True)
        hs = range(B_V_HEADS)
        qq = [qn_ref[rows, (h // rep) * HD:(h // rep + 1) * HD] for h in hs]
        kk = [kn_ref[rows, (h // rep) * HD:(h // rep + 1) * HD] for h in hs]
        kk16 = [k.astype(BF16) for k in kk]
        bcol = [beta_c[:, h:h + 1] for h in hs]
        gcol = [gc_c[:, B_V_HEADS + h:B_V_HEADS + h + 1] for h in hs]
        decay = [jnp.where(incl, jnp.exp(jnp.minimum(
            gcol[h] - gc_r[B_V_HEADS + h:B_V_HEADS + h + 1, :], 0.0)), 0.0) for h in hs]
        kb = [kk[h] * bcol[h] for h in hs]
        a = [jnp.where(strict, _nt(kb[h].astype(BF16), kk16[h]) * decay[h], 0.0) for h in hs]
        dg = [jnp.where(same_blk, a[h], 0.0) for h in hs]
        off = [a[h] - dg[h] for h in hs]
        tinv = [eye - dg[h] for h in hs]
        p = dg
        for _ in range(3):
            p = [_mm3(p[h], p[h]) for h in hs]
            tinv = [tinv[h] + _mm3(tinv[h], p[h]) for h in hs]
        pm = [_mm3(tinv[h], off[h]) for h in hs]
        pm2 = [_mm3(pm[h], pm[h]) for h in hs]
        egc = [jnp.exp(gcol[h]) for h in hs]
        x = [_mm3(tinv[h], jnp.concatenate(
            [vv_ref[rows, h * HD:(h + 1) * HD] * bcol[h], kb[h] * egc[h]], axis=1)) for h in hs]
        x = [x[h] + _mm3(pm2[h], x[h]) for h in hs]
        x = [x[h] - _mm3(pm[h], x[h]) for h in hs]
        aqk = [jnp.where(incl, _nt(qq[h].astype(BF16), kk16[h]) * decay[h], 0.0) for h in hs]
        for h in hs:
            cs = slice(h * HD, (h + 1) * HD)
            g_last = gcol[h][CHUNK - 1:CHUNK, :]
            u_s[rows, cs] = x[h][:, :HD]
            w_s[rows, cs] = x[h][:, HD:].astype(BF16)
            qg_s[rows, cs] = (qq[h] * egc[h]).astype(BF16)
            kd_s[rows, cs] = (kk[h] * jnp.exp(g_last - gcol[h])).astype(BF16)
            aqk_s[c, h] = aqk[h].astype(BF16)
            dl_s[c, h:h + 1, :] = jnp.broadcast_to(jnp.exp(g_last), (1, HD))
        return carry

    lax.fori_loop(0, ts // CHUNK, prep_chunk, 0)

    def scan_chunk(c, carry):
        r0 = pl.multiple_of(c * CHUNK, CHUNK)
        rows = pl.ds(r0, CHUNK)
        hs = range(B_V_HEADS)
        cols = [slice(h * HD, (h + 1) * HD) for h in hs]
        sts = [s_ref[h] for h in hs]
        st16 = [s.astype(BF16) for s in sts]
        v16 = [(u_s[rows, cols[h]] - _mm(w_s[rows, cols[h]], st16[h])).astype(BF16) for h in hs]
        o = [_mm(qg_s[rows, cols[h]], st16[h]) + _mm(aqk_s[c, h], v16[h]) for h in hs]
        new_sts = [sts[h] * dl_s[c, h:h + 1, :] + _tn(kd_s[rows, cols[h]], v16[h]) for h in hs]
        for h in hs:
            s_ref[h] = new_sts[h]
        for h in hs:
            zz = bz_ref[0, rows, cols[h]]
            y = o[h] * lax.rsqrt(jnp.mean(o[h] * o[h], axis=-1, keepdims=True) + EPS)
            o_ref[0, rows, cols[h]] = (y * gn * (zz * _sigmoid(zz))).astype(o_ref.dtype)
        return carry

    lax.fori_loop(0, ts // CHUNK, scan_chunk, 0)


def _gdn(proj3, smt, conv_w, pcol, prow, gn, ts):
    b, s, _ = proj3.shape
    nqk = B_QK_HEADS * HD
    wv = B_V_HEADS * HD
    off_q = (4 * A_HEADS * HD) // nqk
    off_v = (4 * A_HEADS * HD + 2 * nqk) // wv
    off_s = (4 * A_HEADS * HD + 2 * nqk + 2 * wv) // HD
    nch = ts // CHUNK
    return pl.pallas_call(
        functools.partial(_gdn_body, ts=ts),
        grid=(b, s // ts),
        in_specs=[
            pl.BlockSpec((1, ts, nqk), lambda bi, ti: (bi, ti, off_q)),
            pl.BlockSpec((1, ts, nqk), lambda bi, ti: (bi, ti, off_q + 1)),
            pl.BlockSpec((1, ts, wv), lambda bi, ti: (bi, ti, off_v)),
            pl.BlockSpec((1, ts, wv), lambda bi, ti: (bi, ti, off_v + 1)),
            pl.BlockSpec((1, ts, HD), lambda bi, ti: (bi, ti, off_s)),
            pl.BlockSpec((1, nch, 2 * B_V_HEADS, CHUNK), lambda bi, ti: (bi, ti, 0, 0)),
            pl.BlockSpec((CONV, 2 * nqk + wv), lambda bi, ti: (0, 0)),
            pl.BlockSpec((2, HD), lambda bi, ti: (0, 0)),
            pl.BlockSpec((2 * B_V_HEADS, HD), lambda bi, ti: (0, 0)),
            pl.BlockSpec((1, HD), lambda bi, ti: (0, 0)),
        ],
        out_specs=pl.BlockSpec((1, ts, wv), lambda bi, ti: (bi, ti, 0)),
        out_shape=jax.ShapeDtypeStruct((b, s, wv), BF16),
        scratch_shapes=[
            pltpu.VMEM((B_V_HEADS, HD, HD), F32),
            pltpu.VMEM((8, 2 * nqk + wv), F32),
            pltpu.VMEM((ts, nqk), F32),
            pltpu.VMEM((ts, nqk), F32),
            pltpu.VMEM((ts, wv), F32),
            pltpu.VMEM((ts, wv), F32),
            pltpu.VMEM((ts, wv), BF16),
            pltpu.VMEM((ts, wv), BF16),
            pltpu.VMEM((ts, wv), BF16),
            pltpu.VMEM((nch, B_V_HEADS, CHUNK, CHUNK), BF16),
            pltpu.VMEM((nch, B_V_HEADS, HD), F32),
        ],
        compiler_params=pltpu.CompilerParams(
            dimension_semantics=("parallel", "arbitrary"),
            vmem_limit_bytes=VMEM_LIMIT),
        name="gdn",
    )(proj3, proj3, proj3, proj3, proj3, smt, conv_w, pcol, prow, gn)


def _outproj_body(x_ref, oa_ref, ob_ref, wo_ref, fg_ref, wq_ref, k1_ref, k2_ref,
                  h_ref, hn_ref, s1_ref, s2_ref):
    wa = oa_ref.shape[1]
    h = x_ref[...] + _mm(oa_ref[...], wo_ref[:wa, :]) + _mm(ob_ref[...], wo_ref[wa:, :])
    h_ref[...] = h
    hn = h * lax.rsqrt(jnp.mean(h * h, axis=-1, keepdims=True) + EPS) * fg_ref[...]
    hn_ref[...] = hn
    qry = _mm(hn.astype(BF16), wq_ref[...])
    for hh in range(P_HEADS):
        q1 = qry[:, hh * 2 * HD:hh * 2 * HD + HD].astype(BF16)
        q2 = qry[:, hh * 2 * HD + HD:(hh + 1) * 2 * HD].astype(BF16)
        s1_ref[hh] = _nt(k1_ref[hh].astype(BF16), q1)
        s2_ref[hh] = _nt(k2_ref[hh].astype(BF16), q2)


def _out_proj(x2, oa, ob, wo, fg, wq, k1, k2, tm):
    t, d = x2.shape
    wa = oa.shape[1]
    wb = ob.shape[1]
    const = dict(pipeline_mode=pl.Buffered(1))
    return pl.pallas_call(
        _outproj_body,
        grid=(t // tm,),
        in_specs=[
            pl.BlockSpec((tm, d), lambda i: (i, 0)),
            pl.BlockSpec((tm, wa), lambda i: (i, 0)),
            pl.BlockSpec((tm, wb), lambda i: (i, 0)),
            pl.BlockSpec((wa + wb, d), lambda i: (0, 0), **const),
            pl.BlockSpec((1, d), lambda i: (0, 0)),
            pl.BlockSpec((d, P_HEADS * 2 * HD), lambda i: (0, 0), **const),
            pl.BlockSpec((P_HEADS, N_KEYS, HD), lambda i: (0, 0, 0)),
            pl.BlockSpec((P_HEADS, N_KEYS, HD), lambda i: (0, 0, 0)),
        ],
        out_specs=[
            pl.BlockSpec((tm, d), lambda i: (i, 0)),
            pl.BlockSpec((tm, d), lambda i: (i, 0)),
            pl.BlockSpec((P_HEADS, N_KEYS, tm), lambda i: (0, 0, i)),
            pl.BlockSpec((P_HEADS, N_KEYS, tm), lambda i: (0, 0, i)),
        ],
        out_shape=[
            jax.ShapeDtypeStruct((t, d), F32),
            jax.ShapeDtypeStruct((t, d), F32),
            jax.ShapeDtypeStruct((P_HEADS, N_KEYS, t), F32),
            jax.ShapeDtypeStruct((P_HEADS, N_KEYS, t), F32),
        ],
        compiler_params=pltpu.CompilerParams(
            dimension_semantics=("parallel",),
            vmem_limit_bytes=VMEM_LIMIT),
        name="out_proj",
    )(x2, oa, ob, wo, fg, wq, k1, k2)


_PAIRS = [(i, j) for i in range(P_TOPK) for j in range(P_TOPK) if (i + 1) * (j + 1) <= P_TOPK]
_NCAND = -(-len(_PAIRS) // 8) * 8


def _topk_body(s1_ref, s2_ref, pos_ref, e_ref, g_ref, v_scr, i_scr, c_scr, ce_scr, b_scr, x_scr):
    tt = s1_ref.shape[2]
    kio = lax.broadcasted_iota(jnp.int32, (N_KEYS, tt), 0).astype(F32)
    neg = -jnp.inf
    for half, sref in ((0, s1_ref), (1, s2_ref)):
        s = sref[0]
        for r in range(P_TOPK):
            m = jnp.max(s, axis=0, keepdims=True)
            idx = jnp.min(jnp.where(s == m, kio, float(N_KEYS)), axis=0, keepdims=True)
            v_scr[half, r:r + 1, :] = m
            i_scr[half, r:r + 1, :] = idx
            s = jnp.where(kio == idx, neg, s)
    c_scr[...] = jnp.full(c_scr.shape, neg, F32)
    ce_scr[...] = jnp.zeros(ce_scr.shape, F32)
    for c, (i, j) in enumerate(_PAIRS):
        c_scr[c:c + 1, :] = v_scr[0, i:i + 1, :] + v_scr[1, j:j + 1, :]
        ce_scr[c:c + 1, :] = i_scr[0, i:i + 1, :] * float(N_KEYS) + i_scr[1, j:j + 1, :]
    cand = c_scr[...]
    ce = ce_scr[...]
    pos = jnp.broadcast_to(pos_ref[:, 0:1], cand.shape)
    for r in range(P_TOPK):
        m = jnp.max(cand, axis=0, keepdims=True)
        sel = jnp.min(jnp.where(cand == m, pos, 1e9), axis=0, keepdims=True)
        hit = pos == sel
        b_scr[r:r + 1, :] = m
        x_scr[r:r + 1, :] = jnp.max(jnp.where(hit, ce, -1.0), axis=0, keepdims=True)
        cand = jnp.where(hit, neg, cand)
    best = b_scr[...]
    ex = jnp.exp(best - best[0:1, :])
    g_ref[...] = ex / jnp.sum(ex, axis=0, keepdims=True)
    e_ref[...] = x_scr[...].astype(jnp.int32)


def _peer_topk(s1, s2, pos, tt):
    t = s1.shape[2]
    return pl.pallas_call(
        _topk_body,
        grid=(t // tt, P_HEADS),
        in_specs=[
            pl.BlockSpec((1, N_KEYS, tt), lambda i, h: (h, 0, i)),
            pl.BlockSpec((1, N_KEYS, tt), lambda i, h: (h, 0, i)),
            pl.BlockSpec((_NCAND, HD), lambda i, h: (0, 0)),
        ],
        out_specs=[
            pl.BlockSpec((P_TOPK, tt), lambda i, h: (h, i)),
            pl.BlockSpec((P_TOPK, tt), lambda i, h: (h, i)),
        ],
        out_shape=[
            jax.ShapeDtypeStruct((P_HEADS * P_TOPK, t), jnp.int32),
            jax.ShapeDtypeStruct((P_HEADS * P_TOPK, t), F32),
        ],
        scratch_shapes=[
            pltpu.VMEM((2, P_TOPK, tt), F32),
            pltpu.VMEM((2, P_TOPK, tt), F32),
            pltpu.VMEM((_NCAND, tt), F32),
            pltpu.VMEM((_NCAND, tt), F32),
            pltpu.VMEM((P_TOPK, tt), F32),
            pltpu.VMEM((P_TOPK, tt), F32),
        ],
        compiler_params=pltpu.CompilerParams(
            dimension_semantics=("parallel", "parallel"),
            vmem_limit_bytes=VMEM_LIMIT),
        name="peer_topk",
    )(s1, s2, pos)


PEER_TB = 128
PEER_G = 8
PEER_SETS = 3
NSLOT = P_HEADS * P_TOPK
SG = 8


def _gelu(x):
    return 0.5 * x * (1.0 + lax.erf(x * (2.0 ** -0.5)))


def _peer_body(idx_ref, gt_ref, hn_ref, h_ref, fg_ref, w_hbm, o_ref,
               buf, y_scr, sem, *, final, nsteps):
    tb, d = hn_ref.shape
    ngroups = tb // PEER_G
    nsg = NSLOT // SG
    ahead = PEER_SETS - 1
    step = pl.program_id(0)

    def issue_tok(tok, set_, r, s0, s1):
        ids = idx_ref.at[tok]
        dst = buf.at[set_, :, pl.ds(r * NSLOT, NSLOT), :]
        for s in range(s0, s1):
            pltpu.make_async_copy(w_hbm.at[ids[s]], dst.at[:, pl.ds(s, 1), :],
                                  sem.at[set_]).start()

    def wait_set(set_):
        pltpu.make_async_copy(buf.at[set_], buf.at[set_], sem.at[set_]).wait()

    @pl.when(step == 0)
    def _():
        def first(r, carry):
            for a in range(ahead):
                issue_tok(a * PEER_G + r, a, r, 0, NSLOT)
            return carry
        lax.fori_loop(0, PEER_G, first, 0)

    kc = HD
    nkc = d // kc
    per_chunk = NSLOT // 2 // nkc
    rows2 = 2 * NSLOT
    row_g = lax.broadcasted_iota(jnp.int32, (PEER_G, rows2), 0)
    odd_l = lax.broadcasted_iota(jnp.int32, (PEER_G, rows2), 1) % 2 == 1
    row_y = lax.broadcasted_iota(jnp.int32, (PEER_G, d), 0)
    zpad16 = jnp.zeros((PEER_G, rows2), BF16)

    def tok_rows(tiles, c):
        return pltpu.bitcast(tiles[c], BF16)

    def group_body(g, carry):
        gg = step * ngroups + g
        set_ = gg % PEER_SETS
        nxt = (gg + ahead) % PEER_SETS
        wait_set(set_)
        g0 = pl.multiple_of(g * PEER_G, PEER_G)


        x16 = hn_ref[pl.ds(g0, PEER_G), :].astype(BF16)
        x16 = jnp.concatenate([x16, jnp.zeros_like(x16)], axis=0)

        def hid_tok(r, hid):
            tiles = buf.at[set_, :, pl.ds(r * NSLOT, NSLOT), :]
            terms = []
            for c in range(nkc):
                issue_tok((g + ahead) * PEER_G + r, nxt, r, c * per_chunk, (c + 1) * per_chunk)
                terms.append(_nt(x16[:, c * kc:(c + 1) * kc], tok_rows(tiles, c)))
            while len(terms) > 1:
                terms = [a + b for a, b in zip(terms[::2], terms[1::2])]
            return jnp.where(row_g == r, terms[0][:PEER_G], hid)
        hid = jnp.zeros((PEER_G, rows2), F32)
        for r in range(PEER_G):
            hid = hid_tok(r, hid)

        coef = jnp.where(odd_l, gt_ref[pl.ds(g0, PEER_G), :] * pltpu.roll(_gelu(hid), 1, axis=1),
                         0.0).astype(BF16)
        coef = jnp.concatenate([coef, zpad16], axis=0)

        def mix_tok(r, y_acc):
            tiles = buf.at[set_, :, pl.ds(r * NSLOT, NSLOT), :]
            ycs = []
            for c in range(nkc):
                issue_tok((g + ahead) * PEER_G + r, nxt, r,
                          NSLOT // 2 + c * per_chunk, NSLOT // 2 + (c + 1) * per_chunk)
                ycs.append(_mm(coef, tok_rows(tiles, c))[:PEER_G])
            return jnp.where(row_y == r, jnp.concatenate(ycs, axis=1), y_acc)
        y_acc = jnp.zeros((PEER_G, d), F32)
        for r in range(PEER_G):
            y_acc = mix_tok(r, y_acc)
        y_scr[pl.ds(g0, PEER_G), :] = y_acc
        return carry

    lax.fori_loop(0, ngroups, group_body, 0)

    @pl.when(step == nsteps - 1)
    def _():
        for a in range(ahead):
            wait_set((nsteps * ngroups + a) % PEER_SETS)

    hh = h_ref[...] + y_scr[...]
    if final:
        hh = hh * lax.rsqrt(jnp.mean(hh * hh, axis=-1, keepdims=True) + EPS) * fg_ref[...]
    o_ref[...] = hh


def _peer_mix(idx, gt, hn, h, fg, w_tab, final):
    t, d = hn.shape
    tb = PEER_TB
    nsteps = t // tb
    extra = (PEER_SETS - 1) * PEER_G
    idx_pad = jnp.concatenate([idx, jnp.zeros((tb, NSLOT), idx.dtype)], axis=0)
    idx_ext = jnp.concatenate(
        [idx.reshape(nsteps, tb, NSLOT),
         idx_pad[tb:].reshape(nsteps, tb, NSLOT)[:, :extra]], axis=1)
    return pl.pallas_call(
        functools.partial(_peer_body, final=final, nsteps=nsteps),
        grid=(nsteps,),
        in_specs=[
            pl.BlockSpec((None, tb + extra, NSLOT), lambda i: (i, 0, 0),
                         memory_space=pltpu.SMEM),
            pl.BlockSpec((tb, 2 * NSLOT), lambda i: (i, 0)),
            pl.BlockSpec((tb, d), lambda i: (i, 0)),
            pl.BlockSpec((tb, d), lambda i: (i, 0)),
            pl.BlockSpec((1, d), lambda i: (0, 0)),
            pl.BlockSpec(memory_space=pl.ANY),
        ],
        out_specs=pl.BlockSpec((tb, d), lambda i: (i, 0)),
        out_shape=jax.ShapeDtypeStruct((t, d), F32),
        scratch_shapes=[
            pltpu.VMEM((PEER_SETS, d // HD, PEER_G * NSLOT, HD), jnp.uint32),
            pltpu.VMEM((tb, d), F32),
            pltpu.SemaphoreType.DMA((PEER_SETS,)),
        ],
        compiler_params=pltpu.CompilerParams(
            dimension_semantics=("arbitrary",),
            vmem_limit_bytes=VMEM_LIMIT),
        name="peer_mix",
    )(idx_ext, gt, hn, h, fg, w_tab)


def _tile(n, pref):
    return pref if n % pref == 0 else n


def kernel(x, attn_norm_g, w_in, hgrn_lb_logits, hgrn_norm_g, gdn_conv_w, gdn_A_log, gdn_dt_bias,
           gdn_norm_g, w_out, ffn_norm_g, peer_w_query, peer_sub_keys, peer_u, peer_v,
           final_norm_g):
    b, s, d = x.shape
    t = b * s
    depth = w_in.shape[0]
    in_width = w_in.shape[2]
    n_small = 2 * B_V_HEADS
    n_main = in_width - n_small
    in_pad = n_main + HD
    lb_all = jnp.cumsum(jax.nn.softmax(hgrn_lb_logits.astype(F32), axis=0), axis=0)
    pos = jnp.array([i * P_TOPK + j for i, j in _PAIRS]
                    + [10 ** 6 + c for c in range(_NCAND - len(_PAIRS))], F32)
    pos = jnp.broadcast_to(pos[:, None], (_NCAND, HD))

    h2 = x.reshape(t, d)
    for l in range(depth):
        w_l = jnp.pad(w_in[l], ((0, 0), (0, in_pad - in_width))).astype(BF16)
        tn = in_pad // 3 if (in_pad // HD) % 3 == 0 else in_pad
        proj = _in_proj(h2, attn_norm_g[l][None, :], w_l, _tile(t, 512), tn)
        proj3 = proj.reshape(b, s, in_pad)

        o_a = _hgrn(proj3, lb_all[l], hgrn_norm_g[l][None, :], _tile(s, 256))

        ts = _tile(s, 256)
        small = proj3[:, :, n_main:n_main + n_small]
        smt = small.reshape(b, s // CHUNK, CHUNK, n_small).transpose(0, 1, 3, 2)
        zeros8 = jnp.zeros((B_V_HEADS,), F32)
        a_neg = jnp.exp(gdn_A_log[l].astype(F32))
        dtb = gdn_dt_bias[l].astype(F32)
        pcol = jnp.zeros((2, HD), F32)
        pcol = pcol.at[0, B_V_HEADS:n_small].set(a_neg).at[1, B_V_HEADS:n_small].set(dtb)
        prow = jnp.zeros((n_small, HD), F32)
        prow = prow.at[:, 0].set(jnp.concatenate([zeros8, a_neg]))
        prow = prow.at[:, 1].set(jnp.concatenate([zeros8, dtb]))
        o_b = _gdn(proj3, smt, gdn_conv_w[l], pcol, prow, gdn_norm_g[l][None, :], ts)

        wq = peer_w_query[l].reshape(d, P_HEADS * 2 * HD).astype(BF16)
        h2, hn, s1, s2 = _out_proj(
            h2, o_a.reshape(t, -1), o_b.reshape(t, -1), w_out[l].astype(BF16),
            ffn_norm_g[l][None, :], wq, peer_sub_keys[l, 0], peer_sub_keys[l, 1], _tile(t, 256))

        e_t, g_t = _peer_topk(s1, s2, pos, _tile(t, 256))
        ub = lax.bitcast_convert_type(peer_u[l].astype(BF16), jnp.uint16).astype(jnp.uint32)
        vb = lax.bitcast_convert_type(peer_v[l].astype(BF16), jnp.uint16).astype(jnp.uint32)
        w_tab = ((vb << 16) | ub).reshape(-1, d // HD, 1, HD)
        gates2 = jnp.repeat(g_t.T, 2, axis=1)
        h2 = _peer_mix(e_t.T, gates2, hn, h2, final_norm_g[None, :], w_tab, l == depth - 1)
    return h2.reshape(b, s, d)
```

```python
import functools

import jax
import jax.numpy as jnp
from jax import lax
from jax.experimental import pallas as pl
from jax.experimental.pallas import tpu as pltpu

F32 = jnp.float32
BF16 = jnp.bfloat16

EPS = 1e-6
CHUNK = 64
SUB = 16
HD = 128
A_HEADS = 8
B_QK_HEADS = 4
B_V_HEADS = 8
CONV = 4
P_HEADS = 8
N_KEYS = 128
P_TOPK = 16
VMEM_LIMIT = 56 * 1024 * 1024


def _nt(a, b):
    return lax.dot_general(a, b, (((1,), (1,)), ((), ())), preferred_element_type=F32)


def _tn(a, b):
    return lax.dot_general(a, b, (((0,), (0,)), ((), ())), preferred_element_type=F32)


def _mm(a, b):
    return jnp.dot(a, b, preferred_element_type=F32)


def _split_bf16(x, n):
    parts = []
    for _ in range(n - 1):
        p = x.astype(BF16)
        parts.append(p)
        x = x - p.astype(F32)
    parts.append(x.astype(BF16))
    return parts


def _mm3(a, b):
    ah, al = _split_bf16(a, 2)
    bh, bl = _split_bf16(b, 2)
    return _mm(ah, bh) + (_mm(ah, bl) + _mm(al, bh))


def _cumsum_mm(tri, g, g_is_lhs):
    t16 = tri.astype(BF16)
    out = None
    for p in reversed(_split_bf16(g, 3)):
        term = _mm(p, t16) if g_is_lhs else _mm(t16, p)
        out = term if out is None else out + term
    return out


def _sigmoid(x):
    return 1.0 / (1.0 + jnp.exp(-x))


def _softplus(x):
    return jnp.maximum(x, 0.0) + jnp.log1p(jnp.exp(-jnp.abs(x)))


def _bcast_rows(x, idx):
    n = x.shape[1]
    return jnp.concatenate(
        [jnp.broadcast_to(x[r:r + 1, :], (SUB, n)) for r in idx], axis=0)


def _inproj_body(x_ref, g_ref, w_ref, o_ref):
    x = x_ref[...]
    ms = jnp.mean(x * x, axis=-1, keepdims=True)
    xn = (x * lax.rsqrt(ms + EPS) * g_ref[...]).astype(BF16)
    o_ref[...] = _mm(xn, w_ref[...])


def _in_proj(x2, g, w, tm, tn):
    t, d = x2.shape
    n = w.shape[1]
    return pl.pallas_call(
        _inproj_body,
        grid=(n // tn, t // tm),
        in_specs=[
            pl.BlockSpec((tm, d), lambda j, i: (i, 0)),
            pl.BlockSpec((1, d), lambda j, i: (0, 0)),
            pl.BlockSpec((d, tn), lambda j, i: (0, j)),
        ],
        out_specs=pl.BlockSpec((tm, tn), lambda j, i: (i, j)),
        out_shape=jax.ShapeDtypeStruct((t, n), F32),
        compiler_params=pltpu.CompilerParams(
            dimension_semantics=("parallel", "parallel"),
            vmem_limit_bytes=VMEM_LIMIT),
        name="in_proj",
    )(x2, g, w)


def _hgrn_body(q_ref, f_ref, i_ref, g_ref, lb_ref, gn_ref, o_ref, st_ref, *, ts):
    @pl.when(pl.program_id(1) == 0)
    def _():
        st_ref[...] = jnp.zeros_like(st_ref)

    row = lax.broadcasted_iota(jnp.int32, (CHUNK, CHUNK), 0)
    col = lax.broadcasted_iota(jnp.int32, (CHUNK, CHUNK), 1)
    tril_f = (col <= row).astype(F32)
    blk_r = row // SUB
    blk_c = col // SUB
    rblk = lax.broadcasted_iota(jnp.int32, (CHUNK, HD), 0) // SUB
    nsub = CHUNK // SUB
    gn = gn_ref[...]

    def chunk_body(c, carry):
        r0 = pl.multiple_of(c * CHUNK, CHUNK)
        rows = pl.ds(r0, CHUNK)
        hs = range(A_HEADS)
        cols = [slice(h * HD, (h + 1) * HD) for h in hs]
        q = [q_ref[0, rows, cols[h]] for h in hs]
        z = [f_ref[0, rows, cols[h]] for h in hs]
        v16 = [i_ref[0, rows, cols[h]].astype(BF16) for h in hs]
        lb = [lb_ref[h:h + 1, :] for h in hs]
        sts = [st_ref[h] for h in hs]
        k = [(1.0 - lb[h]) * _sigmoid(-z[h]) for h in hs]
        b = [_cumsum_mm(tril_f, jnp.log(lb[h] + (1.0 - lb[h]) * _sigmoid(z[h])), False)
             for h in hs]
        o = [_nt((q[h] * jnp.exp(b[h])).astype(BF16), sts[h].astype(BF16)) for h in hs]
        att = []
        for h in hs:
            bnd = [b[h][SUB * j + SUB - 1:SUB * j + SUB, :] for j in range(nsub)]
            k_off = k[h] * jnp.exp(
                _bcast_rows(b[h], [SUB * j + SUB - 1 for j in range(nsub)]) - b[h])
            qcat = jnp.concatenate(
                [q[h] * jnp.exp(jnp.minimum(b[h] - bnd[j], 0.0)) for j in range(nsub - 1)],
                axis=1)
            kcat = jnp.concatenate(
                [jnp.where(rblk == j, k_off, 0.0) for j in range(nsub - 1)], axis=1)
            att_off = _nt(qcat.astype(BF16), kcat.astype(BF16))
            ref_rows = _bcast_rows(b[h], [SUB * i for i in range(nsub)])
            q_d = q[h] * jnp.exp(b[h] - ref_rows)
            k_d = k[h] * jnp.exp(ref_rows - b[h])
            att_d = _nt(q_d.astype(BF16), k_d.astype(BF16))
            att.append(jnp.where(blk_r > blk_c, att_off,
                                 jnp.where((blk_r == blk_c) & (col <= row), att_d, 0.0)))
        o = [o[h] + _mm(att[h].astype(BF16), v16[h]) for h in hs]
        new_sts = []
        for h in hs:
            b_last = b[h][CHUNK - 1:CHUNK, :]
            kdec = k[h] * jnp.exp(b_last - b[h])
            new_sts.append(sts[h] * jnp.exp(b_last) + _tn(v16[h], kdec.astype(BF16)))
        for h in hs:
            st_ref[h] = new_sts[h]
        for h in hs:
            gate = g_ref[0, rows, cols[h]]
            y = o[h] * lax.rsqrt(jnp.mean(o[h] * o[h], axis=-1, keepdims=True) + EPS)
            o_ref[0, rows, cols[h]] = (y * gn * (gate * _sigmoid(gate))).astype(o_ref.dtype)
        return carry

    lax.fori_loop(0, ts // CHUNK, chunk_body, 0)


def _hgrn(proj3, lb, gn, ts):
    b, s, _ = proj3.shape
    w = A_HEADS * HD
    return pl.pallas_call(
        functools.partial(_hgrn_body, ts=ts),
        grid=(b, s // ts),
        in_specs=[
            pl.BlockSpec((1, ts, w), lambda bi, ti: (bi, ti, 0)),
            pl.BlockSpec((1, ts, w), lambda bi, ti: (bi, ti, 1)),
            pl.BlockSpec((1, ts, w), lambda bi, ti: (bi, ti, 2)),
            pl.BlockSpec((1, ts, w), lambda bi, ti: (bi, ti, 3)),
            pl.BlockSpec((A_HEADS, HD), lambda bi, ti: (0, 0)),
            pl.BlockSpec((1, HD), lambda bi, ti: (0, 0)),
        ],
        out_specs=pl.BlockSpec((1, ts, w), lambda bi, ti: (bi, ti, 0)),
        out_shape=jax.ShapeDtypeStruct((b, s, w), BF16),
        scratch_shapes=[pltpu.VMEM((A_HEADS, HD, HD), F32)],
        compiler_params=pltpu.CompilerParams(
            dimension_semantics=("parallel", "arbitrary"),
            vmem_limit_bytes=VMEM_LIMIT),
        name="hgrn",
    )(proj3, proj3, proj3, proj3, lb, gn)


def _gdn_body(bq_ref, bk_ref, bv_ref, bz_ref, sm_ref, smt_ref, cw_ref, pcol_ref, prow_ref,
              gn_ref, o_ref, s_ref, tail_ref, qn_ref, kn_ref, vv_ref,
              u_s, w_s, qg_s, kd_s, aqk_s, dl_s, *, ts):
    nqk = B_QK_HEADS * HD

    @pl.when(pl.program_id(1) == 0)
    def _():
        s_ref[...] = jnp.zeros_like(s_ref)
        tail_ref[...] = jnp.zeros_like(tail_ref)

    def conv_silu(x, c0):
        n = x.shape[1]
        xe = jnp.concatenate([tail_ref[:, c0:c0 + n], x], axis=0)
        w = cw_ref[:, c0:c0 + n]
        y = x * w[CONV - 1:CONV, :]
        for j in range(1, CONV):
            y = y + pltpu.roll(xe, j, axis=0)[8:, :] * w[CONV - 1 - j:CONV - j, :]
        tail_ref[:, c0:c0 + n] = x[ts - 8:, :]
        return y * _sigmoid(y)

    def l2n(x):
        return x * lax.rsqrt(jnp.sum(x * x, axis=-1, keepdims=True) + EPS)

    yq = conv_silu(bq_ref[0], 0)
    yk = conv_silu(bk_ref[0], nqk)
    vv_ref[...] = conv_silu(bv_ref[0], 2 * nqk)
    for h in range(B_QK_HEADS):
        cs = slice(h * HD, (h + 1) * HD)
        qn_ref[:, cs] = l2n(yq[:, cs]) * (HD ** -0.5)
        kn_ref[:, cs] = l2n(yk[:, cs])

    row = lax.broadcasted_iota(jnp.int32, (CHUNK, CHUNK), 0)
    col = lax.broadcasted_iota(jnp.int32, (CHUNK, CHUNK), 1)
    incl = col <= row
    strict = col < row
    tril_f = incl.astype(F32)
    triu_f = (row <= col).astype(F32)
    same_blk = (row // SUB) == (col // SUB)
    eye = (row == col).astype(F32)
    gn = gn_ref[...]
    a_col = pcol_ref[0:1, :]
    dt_col = pcol_ref[1:2, :]
    a_row = prow_ref[:, 0:1]
    dt_row = prow_ref[:, 1:2]
    rep = B_V_HEADS // B_QK_HEADS

    def prep_chunk(c, carry):
        r0 = pl.multiple_of(c * CHUNK, CHUNK)
        rows = pl.ds(r0, CHUNK)
        sm = sm_ref[0, rows, :]
        beta_c = _sigmoid(sm)
        gc_c = _cumsum_mm(tril_f, -a_col * _softplus(sm + dt_col), False)
        smt = smt_ref[0, c]
        gc_r = _cumsum_mm(triu_f, -a_row * _softplus(smt + dt_row), True)
        hs = range(B_V_HEADS)
        qq = [qn_ref[rows, (h // rep) * HD:(h // rep + 1) * HD] for h in hs]
        kk = [kn_ref[rows, (h // rep) * HD:(h // rep + 1) * HD] for h in hs]
        kk16 = [k.astype(BF16) for k in kk]
        bcol = [beta_c[:, h:h + 1] for h in hs]
        gcol = [gc_c[:, B_V_HEADS + h:B_V_HEADS + h + 1] for h in hs]
        decay = [jnp.where(incl, jnp.exp(jnp.minimum(
            gcol[h] - gc_r[B_V_HEADS + h:B_V_HEADS + h + 1, :], 0.0)), 0.0) for h in hs]
        kb = [kk[h] * bcol[h] for h in hs]
        a = [jnp.where(strict, _nt(kb[h].astype(BF16), kk16[h]) * decay[h], 0.0) for h in hs]
        dg = [jnp.where(same_blk, a[h], 0.0) for h in hs]
        off = [a[h] - dg[h] for h in hs]
        tinv = [eye - dg[h] for h in hs]
        p = dg
        for _ in range(3):
            p = [_mm3(p[h], p[h]) for h in hs]
            tinv = [tinv[h] + _mm3(tinv[h], p[h]) for h in hs]
        pm = [_mm3(tinv[h], off[h]) for h in hs]
        pm2 = [_mm3(pm[h], pm[h]) for h in hs]
        egc = [jnp.exp(gcol[h]) for h in hs]
        x = [_mm3(tinv[h], jnp.concatenate(
            [vv_ref[rows, h * HD:(h + 1) * HD] * bcol[h], kb[h] * egc[h]], axis=1)) for h in hs]
        x = [x[h] + _mm3(pm2[h], x[h]) for h in hs]
        x = [x[h] - _mm3(pm[h], x[h]) for h in hs]
        aqk = [jnp.where(incl, _nt(qq[h].astype(BF16), kk16[h]) * decay[h], 0.0) for h in hs]
        for h in hs:
            cs = slice(h * HD, (h + 1) * HD)
            g_last = gcol[h][CHUNK - 1:CHUNK, :]
            u_s[rows, cs] = x[h][:, :HD]
            w_s[rows, cs] = x[h][:, HD:].astype(BF16)
            qg_s[rows, cs] = (qq[h] * egc[h]).astype(BF16)
            kd_s[rows, cs] = (kk[h] * jnp.exp(g_last - gcol[h])).astype(BF16)
            aqk_s[c, h] = aqk[h].astype(BF16)
            dl_s[c, h:h + 1, :] = jnp.broadcast_to(jnp.exp(g_last), (1, HD))
        return carry

    lax.fori_loop(0, ts // CHUNK, prep_chunk, 0)

    def scan_chunk(c, carry):
        r0 = pl.multiple_of(c * CHUNK, CHUNK)
        rows = pl.ds(r0, CHUNK)
        hs = range(B_V_HEADS)
        cols = [slice(h * HD, (h + 1) * HD) for h in hs]
        sts = [s_ref[h] for h in hs]
        st16 = [s.astype(BF16) for s in sts]
        v16 = [(u_s[rows, cols[h]] - _mm(w_s[rows, cols[h]], st16[h])).astype(BF16) for h in hs]
        o = [_mm(qg_s[rows, cols[h]], st16[h]) + _mm(aqk_s[c, h], v16[h]) for h in hs]
        new_sts = [sts[h] * dl_s[c, h:h + 1, :] + _tn(kd_s[rows, cols[h]], v16[h]) for h in hs]
        for h in hs:
            s_ref[h] = new_sts[h]
        for h in hs:
            zz = bz_ref[0, rows, cols[h]]
            y = o[h] * lax.rsqrt(jnp.mean(o[h] * o[h], axis=-1, keepdims=True) + EPS)
            o_ref[0, rows, cols[h]] = (y * gn * (zz * _sigmoid(zz))).astype(o_ref.dtype)
        return carry

    lax.fori_loop(0, ts // CHUNK, scan_chunk, 0)


def _gdn(proj3, smt, conv_w, pcol, prow, gn, ts):
    b, s, _ = proj3.shape
    nqk = B_QK_HEADS * HD
    wv = B_V_HEADS * HD
    off_q = (4 * A_HEADS * HD) // nqk
    off_v = (4 * A_HEADS * HD + 2 * nqk) // wv
    off_s = (4 * A_HEADS * HD + 2 * nqk + 2 * wv) // HD
    nch = ts // CHUNK
    return pl.pallas_call(
        functools.partial(_gdn_body, ts=ts),
        grid=(b, s // ts),
        in_specs=[
            pl.BlockSpec((1, ts, nqk), lambda bi, ti: (bi, ti, off_q)),
            pl.BlockSpec((1, ts, nqk), lambda bi, ti: (bi, ti, off_q + 1)),
            pl.BlockSpec((1, ts, wv), lambda bi, ti: (bi, ti, off_v)),
            pl.BlockSpec((1, ts, wv), lambda bi, ti: (bi, ti, off_v + 1)),
            pl.BlockSpec((1, ts, HD), lambda bi, ti: (bi, ti, off_s)),
            pl.BlockSpec((1, nch, 2 * B_V_HEADS, CHUNK), lambda bi, ti: (bi, ti, 0, 0)),
            pl.BlockSpec((CONV, 2 * nqk + wv), lambda bi, ti: (0, 0)),
            pl.BlockSpec((2, HD), lambda bi, ti: (0, 0)),
            pl.BlockSpec((2 * B_V_HEADS, HD), lambda bi, ti: (0, 0)),
            pl.BlockSpec((1, HD), lambda bi, ti: (0, 0)),
        ],
        out_specs=pl.BlockSpec((1, ts, wv), lambda bi, ti: (bi, ti, 0)),
        out_shape=jax.ShapeDtypeStruct((b, s, wv), BF16),
        scratch_shapes=[
            pltpu.VMEM((B_V_HEADS, HD, HD), F32),
            pltpu.VMEM((8, 2 * nqk + wv), F32),
            pltpu.VMEM((ts, nqk), F32),
            pltpu.VMEM((ts, nqk), F32),
            pltpu.VMEM((ts, wv), F32),
            pltpu.VMEM((ts, wv), F32),
            pltpu.VMEM((ts, wv), BF16),
            pltpu.VMEM((ts, wv), BF16),
            pltpu.VMEM((ts, wv), BF16),
            pltpu.VMEM((nch, B_V_HEADS, CHUNK, CHUNK), BF16),
            pltpu.VMEM((nch, B_V_HEADS, HD), F32),
        ],
        compiler_params=pltpu.CompilerParams(
            dimension_semantics=("parallel", "arbitrary"),
            vmem_limit_bytes=VMEM_LIMIT),
        name="gdn",
    )(proj3, proj3, proj3, proj3, proj3, smt, conv_w, pcol, prow, gn)


def _outproj_body(x_ref, oa_ref, ob_ref, wo_ref, fg_ref, wq_ref, k1_ref, k2_ref,
                  h_ref, hn_ref, s1_ref, s2_ref):
    wa = oa_ref.shape[1]
    h = x_ref[...] + _mm(oa_ref[...], wo_ref[:wa, :]) + _mm(ob_ref[...], wo_ref[wa:, :])
    h_ref[...] = h
    hn = h * lax.rsqrt(jnp.mean(h * h, axis=-1, keepdims=True) + EPS) * fg_ref[...]
    hn_ref[...] = hn
    qry = _mm(hn.astype(BF16), wq_ref[...])
    for hh in range(P_HEADS):
        q1 = qry[:, hh * 2 * HD:hh * 2 * HD + HD].astype(BF16)
        q2 = qry[:, hh * 2 * HD + HD:(hh + 1) * 2 * HD].astype(BF16)
        s1_ref[hh] = _nt(k1_ref[hh].astype(BF16), q1)
        s2_ref[hh] = _nt(k2_ref[hh].astype(BF16), q2)


def _out_proj(x2, oa, ob, wo, fg, wq, k1, k2, tm):
    t, d = x2.shape
    wa = oa.shape[1]
    wb = ob.shape[1]
    const = dict(pipeline_mode=pl.Buffered(1))
    return pl.pallas_call(
        _outproj_body,
        grid=(t // tm,),
        in_specs=[
            pl.BlockSpec((tm, d), lambda i: (i, 0)),
            pl.BlockSpec((tm, wa), lambda i: (i, 0)),
            pl.BlockSpec((tm, wb), lambda i: (i, 0)),
            pl.BlockSpec((wa + wb, d), lambda i: (0, 0), **const),
            pl.BlockSpec((1, d), lambda i: (0, 0)),
            pl.BlockSpec((d, P_HEADS * 2 * HD), lambda i: (0, 0), **const),
            pl.BlockSpec((P_HEADS, N_KEYS, HD), lambda i: (0, 0, 0)),
            pl.BlockSpec((P_HEADS, N_KEYS, HD), lambda i: (0, 0, 0)),
        ],
        out_specs=[
            pl.BlockSpec((tm, d), lambda i: (i, 0)),
            pl.BlockSpec((tm, d), lambda i: (i, 0)),
            pl.BlockSpec((P_HEADS, N_KEYS, tm), lambda i: (0, 0, i)),
            pl.BlockSpec((P_HEADS, N_KEYS, tm), lambda i: (0, 0, i)),
        ],
        out_shape=[
            jax.ShapeDtypeStruct((t, d), F32),
            jax.ShapeDtypeStruct((t, d), F32),
            jax.ShapeDtypeStruct((P_HEADS, N_KEYS, t), F32),
            jax.ShapeDtypeStruct((P_HEADS, N_KEYS, t), F32),
        ],
        compiler_params=pltpu.CompilerParams(
            dimension_semantics=("parallel",),
            vmem_limit_bytes=VMEM_LIMIT),
        name="out_proj",
    )(x2, oa, ob, wo, fg, wq, k1, k2)


_PAIRS = [(i, j) for i in range(P_TOPK) for j in range(P_TOPK) if (i + 1) * (j + 1) <= P_TOPK]
_NCAND = -(-len(_PAIRS) // 8) * 8


def _topk_body(s1_ref, s2_ref, pos_ref, e_ref, g_ref, v_scr, i_scr, c_scr, ce_scr, b_scr, x_scr):
    tt = s1_ref.shape[2]
    kio = lax.broadcasted_iota(jnp.int32, (N_KEYS, tt), 0).astype(F32)
    neg = -jnp.inf
    for half, sref in ((0, s1_ref), (1, s2_ref)):
        s = sref[0]
        for r in range(P_TOPK):
            m = jnp.max(s, axis=0, keepdims=True)
            idx = jnp.min(jnp.where(s == m, kio, float(N_KEYS)), axis=0, keepdims=True)
            v_scr[half, r:r + 1, :] = m
            i_scr[half, r:r + 1, :] = idx
            s = jnp.where(kio == idx, neg, s)
    c_scr[...] = jnp.full(c_scr.shape, neg, F32)
    ce_scr[...] = jnp.zeros(ce_scr.shape, F32)
    for c, (i, j) in enumerate(_PAIRS):
        c_scr[c:c + 1, :] = v_scr[0, i:i + 1, :] + v_scr[1, j:j + 1, :]
        ce_scr[c:c + 1, :] = i_scr[0, i:i + 1, :] * float(N_KEYS) + i_scr[1, j:j + 1, :]
    cand = c_scr[...]
    ce = ce_scr[...]
    pos = jnp.broadcast_to(pos_ref[:, 0:1], cand.shape)
    for r in range(P_TOPK):
        m = jnp.max(cand, axis=0, keepdims=True)
        sel = jnp.min(jnp.where(cand == m, pos, 1e9), axis=0, keepdims=True)
        hit = pos == sel
        b_scr[r:r + 1, :] = m
        x_scr[r:r + 1, :] = jnp.max(jnp.where(hit, ce, -1.0), axis=0, keepdims=True)
        cand = jnp.where(hit, neg, cand)
    best = b_scr[...]
    ex = jnp.exp(best - best[0:1, :])
    g_ref[...] = ex / jnp.sum(ex, axis=0, keepdims=True)
    e_ref[...] = x_scr[...].astype(jnp.int32)


def _peer_topk(s1, s2, pos, tt):
    t = s1.shape[2]
    return pl.pallas_call(
        _topk_body,
        grid=(t // tt, P_HEADS),
        in_specs=[
            pl.BlockSpec((1, N_KEYS, tt), lambda i, h: (h, 0, i)),
            pl.BlockSpec((1, N_KEYS, tt), lambda i, h: (h, 0, i)),
            pl.BlockSpec((_NCAND, HD), lambda i, h: (0, 0)),
        ],
        out_specs=[
            pl.BlockSpec((P_TOPK, tt), lambda i, h: (h, i)),
            pl.BlockSpec((P_TOPK, tt), lambda i, h: (h, i)),
        ],
        out_shape=[
            jax.ShapeDtypeStruct((P_HEADS * P_TOPK, t), jnp.int32),
            jax.ShapeDtypeStruct((P_HEADS * P_TOPK, t), F32),
        ],
        scratch_shapes=[
            pltpu.VMEM((2, P_TOPK, tt), F32),
            pltpu.VMEM((2, P_TOPK, tt), F32),
            pltpu.VMEM((_NCAND, tt), F32),
            pltpu.VMEM((_NCAND, tt), F32),
            pltpu.VMEM((P_TOPK, tt), F32),
            pltpu.VMEM((P_TOPK, tt), F32),
        ],
        compiler_params=pltpu.CompilerParams(
            dimension_semantics=("parallel", "parallel"),
            vmem_limit_bytes=VMEM_LIMIT),
        name="peer_topk",
    )(s1, s2, pos)


PEER_TB = 128
PEER_G = 8
PEER_SETS = 4
DMA_QUEUES = 2
NSLOT = P_HEADS * P_TOPK
SG = 8
CC_AHEAD = 3


def _gelu(x):
    return 0.5 * x * (1.0 + lax.erf(x * (2.0 ** -0.5)))


def _peer_body(idx_ref, gt_ref, hn_ref, h_ref, fg_ref, w_hbm, o_ref,
               hid_scr, c_scr, y_scr, sem, *bufs, final, nsteps):
    tb, d = hn_ref.shape
    ngroups = tb // PEER_G
    nsg = NSLOT // SG
    ahead = PEER_SETS - 1
    step = pl.program_id(0)

    def issue_tok(tok, set_, r, s0, s1):
        ids = idx_ref.at[tok]
        dst = bufs[set_].at[:, pl.ds(r * NSLOT, NSLOT), :]
        for s in range(s0, s1):
            pltpu.make_async_copy(w_hbm.at[ids[s]], dst.at[:, pl.ds(s, 1), :],
                                  sem.at[set_]).start(priority=s % DMA_QUEUES)

    def wait_set(set_):
        pltpu.make_async_copy(bufs[set_], bufs[set_], sem.at[set_]).wait()

    @pl.when(step == 0)
    def _():
        def first(r, carry):
            for a in range(ahead):
                issue_tok(a * PEER_G + r, a, r, 0, NSLOT)
            return carry
        lax.fori_loop(0, PEER_G, first, 0)

    nlc = d // HD
    nsg = NSLOT // SG
    per_tile = NSLOT // 2 // nsg
    lane = lax.broadcasted_iota(jnp.int32, (SG, tb), 1)
    hid_scr[...] = jnp.zeros_like(hid_scr)

    def group_body(g, set_):
        nxt = (set_ + ahead) % PEER_SETS
        wait_set(set_)


        def hid_tok(r, c2):
            t = g * PEER_G + r
            xrow = hn_ref[pl.ds(t, 1), :]
            xb = [jnp.broadcast_to(xrow[:, lc * HD:(lc + 1) * HD], (SG, HD)) for lc in range(nlc)]
            tiles = bufs[set_].at[:, pl.ds(r * NSLOT, NSLOT), :]
            hs = []
            for sg in range(nsg):
                issue_tok((g + ahead) * PEER_G + r, nxt, r, sg * per_tile, (sg + 1) * per_tile)
                acc = None
                for lc in range(nlc):
                    u = lax.bitcast_convert_type(tiles[lc, sg * SG:(sg + 1) * SG, :] << 16, F32)
                    term = u * xb[lc]
                    acc = term if acc is None else acc + term
                hs.append(jnp.sum(acc, axis=1, keepdims=True))
            for sg in range(nsg):
                pltpu.store(hid_scr.at[pl.ds(sg * SG, SG), :],
                            jnp.broadcast_to(hs[sg], (SG, tb)), mask=lane == t)
            return c2
        lax.fori_loop(0, PEER_G, hid_tok, 0)

        c_scr[...] = gt_ref[...] * _gelu(hid_scr[...])

        def mix_tok(r, c2):
            t = g * PEER_G + r
            tiles = bufs[set_].at[:, pl.ds(r * NSLOT, NSLOT), :]
            def coef_col(sg):
                return jnp.sum(jnp.where(lane == t, c_scr[pl.ds(sg * SG, SG), :], 0.0),
                               axis=1, keepdims=True)
            cc = {sg: coef_col(sg) for sg in range(CC_AHEAD)}
            accs = [None] * nlc
            for sg in range(nsg):
                issue_tok((g + ahead) * PEER_G + r, nxt, r,
                          NSLOT // 2 + sg * per_tile, NSLOT // 2 + (sg + 1) * per_tile)
                if sg + CC_AHEAD < nsg:
                    cc[sg + CC_AHEAD] = coef_col(sg + CC_AHEAD)
                for lc in range(nlc):
                    v = lax.bitcast_convert_type(
                        tiles[lc, sg * SG:(sg + 1) * SG, :] & jnp.uint32(0xFFFF0000), F32)
                    term = v * cc[sg]
                    accs[lc] = term if accs[lc] is None else accs[lc] + term
            yrow = jnp.concatenate(
                [jnp.sum(a, axis=0, keepdims=True) for a in accs], axis=1)
            y_scr[pl.ds(t, 1), :] = yrow
            return c2
        lax.fori_loop(0, PEER_G, mix_tok, 0)

    def group_round(q, carry):
        for j in range(PEER_SETS):
            group_body(q * PEER_SETS + j, j)
        return carry

    lax.fori_loop(0, ngroups // PEER_SETS, group_round, 0)

    @pl.when(step == nsteps - 1)
    def _():
        for a in range(ahead):
            wait_set((nsteps * ngroups + a) % PEER_SETS)

    hh = h_ref[...] + y_scr[...]
    if final:
        hh = hh * lax.rsqrt(jnp.mean(hh * hh, axis=-1, keepdims=True) + EPS) * fg_ref[...]
    o_ref[...] = hh


def _peer_mix(idx, gt, hn, h, fg, w_tab, final):
    t, d = hn.shape
    tb = PEER_TB
    nsteps = t // tb
    extra = (PEER_SETS - 1) * PEER_G
    assert (tb // PEER_G) % PEER_SETS == 0
    idx_pad = jnp.concatenate([idx, jnp.zeros((tb, NSLOT), idx.dtype)], axis=0)
    idx_ext = jnp.concatenate(
        [idx.reshape(nsteps, tb, NSLOT),
         idx_pad[tb:].reshape(nsteps, tb, NSLOT)[:, :extra]], axis=1)
    return pl.pallas_call(
        functools.partial(_peer_body, final=final, nsteps=nsteps),
        grid=(nsteps,),
        in_specs=[
            pl.BlockSpec((None, tb + extra, NSLOT), lambda i: (i, 0, 0),
                         memory_space=pltpu.SMEM),
            pl.BlockSpec((NSLOT, tb), lambda i: (0, i)),
            pl.BlockSpec((tb, d), lambda i: (i, 0)),
            pl.BlockSpec((tb, d), lambda i: (i, 0)),
            pl.BlockSpec((1, d), lambda i: (0, 0)),
            pl.BlockSpec(memory_space=pl.ANY),
        ],
        out_specs=pl.BlockSpec((tb, d), lambda i: (i, 0)),
        out_shape=jax.ShapeDtypeStruct((t, d), F32),
        scratch_shapes=[
            pltpu.VMEM((NSLOT, tb), F32),
            pltpu.VMEM((NSLOT, tb), F32),
            pltpu.VMEM((tb, d), F32),
            pltpu.SemaphoreType.DMA((PEER_SETS,)),
        ] + [pltpu.VMEM((d // HD, PEER_G * NSLOT, HD), jnp.uint32) for _ in range(PEER_SETS)],
        compiler_params=pltpu.CompilerParams(
            dimension_semantics=("arbitrary",),
            vmem_limit_bytes=VMEM_LIMIT),
        name="peer_mix",
    )(idx_ext, gt, hn, h, fg, w_tab)


def _tile(n, pref):
    return pref if n % pref == 0 else n


def kernel(x, attn_norm_g, w_in, hgrn_lb_logits, hgrn_norm_g, gdn_conv_w, gdn_A_log, gdn_dt_bias,
           gdn_norm_g, w_out, ffn_norm_g, peer_w_query, peer_sub_keys, peer_u, peer_v,
           final_norm_g):
    b, s, d = x.shape
    t = b * s
    depth = w_in.shape[0]
    in_width = w_in.shape[2]
    n_small = 2 * B_V_HEADS
    n_main = in_width - n_small
    in_pad = n_main + HD
    lb_all = jnp.cumsum(jax.nn.softmax(hgrn_lb_logits.astype(F32), axis=0), axis=0)
    pos = jnp.array([i * P_TOPK + j for i, j in _PAIRS]
                    + [10 ** 6 + c for c in range(_NCAND - len(_PAIRS))], F32)
    pos = jnp.broadcast_to(pos[:, None], (_NCAND, HD))

    h2 = x.reshape(t, d)
    for l in range(depth):
        w_l = jnp.pad(w_in[l], ((0, 0), (0, in_pad - in_width))).astype(BF16)
        tn = in_pad // 3 if (in_pad // HD) % 3 == 0 else in_pad
        proj = _in_proj(h2, attn_norm_g[l][None, :], w_l, _tile(t, 512), tn)
        proj3 = proj.reshape(b, s, in_pad)

        o_a = _hgrn(proj3, lb_all[l], hgrn_norm_g[l][None, :], _tile(s, 256))

        ts = _tile(s, 256)
        small = proj3[:, :, n_main:n_main + n_small]
        smt = small.reshape(b, s // CHUNK, CHUNK, n_small).transpose(0, 1, 3, 2)
        zeros8 = jnp.zeros((B_V_HEADS,), F32)
        a_neg = jnp.exp(gdn_A_log[l].astype(F32))
        dtb = gdn_dt_bias[l].astype(F32)
        pcol = jnp.zeros((2, HD), F32)
        pcol = pcol.at[0, B_V_HEADS:n_small].set(a_neg).at[1, B_V_HEADS:n_small].set(dtb)
        prow = jnp.zeros((n_small, HD), F32)
        prow = prow.at[:, 0].set(jnp.concatenate([zeros8, a_neg]))
        prow = prow.at[:, 1].set(jnp.concatenate([zeros8, dtb]))
        o_b = _gdn(proj3, smt, gdn_conv_w[l], pcol, prow, gdn_norm_g[l][None, :], ts)

        wq = peer_w_query[l].reshape(d, P_HEADS * 2 * HD).astype(BF16)
        h2, hn, s1, s2 = _out_proj(
            h2, o_a.reshape(t, -1), o_b.reshape(t, -1), w_out[l].astype(BF16),
            ffn_norm_g[l][None, :], wq, peer_sub_keys[l, 0], peer_sub_keys[l, 1], _tile(t, 256))

        e_t, g_t = _peer_topk(s1, s2, pos, _tile(t, 256))
        ub = lax.bitcast_convert_type(peer_u[l].astype(BF16), jnp.uint16).astype(jnp.uint32)
        vb = lax.bitcast_convert_type(peer_v[l].astype(BF16), jnp.uint16).astype(jnp.uint32)
        w_tab = ((vb << 16) | ub).reshape(-1, d // HD, 1, HD)
        h2 = _peer_mix(e_t.T, g_t, hn, h2, final_norm_g[None, :], w_tab, l == depth - 1)
    return h2.reshape(b, s, d)
```

```python
import functools

import jax
import jax.numpy as jnp
from jax import lax
from jax.experimental import pallas as pl
from jax.experimental.pallas import tpu as pltpu

F32 = jnp.float32
BF16 = jnp.bfloat16

EPS = 1e-6
CHUNK = 64
SUB = 16
HD = 128
A_HEADS = 8
B_QK_HEADS = 4
B_V_HEADS = 8
CONV = 4
P_HEADS = 8
N_KEYS = 128
P_TOPK = 16
VMEM_LIMIT = 56 * 1024 * 1024


def _nt(a, b):
    return lax.dot_general(a, b, (((1,), (1,)), ((), ())), preferred_element_type=F32)


def _tn(a, b):
    return lax.dot_general(a, b, (((0,), (0,)), ((), ())), preferred_element_type=F32)


def _mm(a, b):
    return jnp.dot(a, b, preferred_element_type=F32)


def _split_bf16(x, n):
    parts = []
    for _ in range(n - 1):
        p = x.astype(BF16)
        parts.append(p)
        x = x - p.astype(F32)
    parts.append(x.astype(BF16))
    return parts


def _mm3(a, b):
    ah, al = _split_bf16(a, 2)
    bh, bl = _split_bf16(b, 2)
    return _mm(ah, bh) + (_mm(ah, bl) + _mm(al, bh))


def _cumsum_mm(tri, g, g_is_lhs):
    t16 = tri.astype(BF16)
    out = None
    for p in reversed(_split_bf16(g, 3)):
        term = _mm(p, t16) if g_is_lhs else _mm(t16, p)
        out = term if out is None else out + term
    return out


def _sigmoid(x):
    return 1.0 / (1.0 + jnp.exp(-x))


def _softplus(x):
    return jnp.maximum(x, 0.0) + jnp.log1p(jnp.exp(-jnp.abs(x)))


def _bcast_rows(x, idx):
    n = x.shape[1]
    return jnp.concatenate(
        [jnp.broadcast_to(x[r:r + 1, :], (SUB, n)) for r in idx], axis=0)


def _inproj_body(x_ref, g_ref, w_ref, o_ref):
    x = x_ref[...]
    ms = jnp.mean(x * x, axis=-1, keepdims=True)
    xn = (x * lax.rsqrt(ms + EPS) * g_ref[...]).astype(BF16)
    o_ref[...] = _mm(xn, w_ref[...])


def _in_proj(x2, g, w, tm, tn):
    t, d = x2.shape
    n = w.shape[1]
    return pl.pallas_call(
        _inproj_body,
        grid=(n // tn, t // tm),
        in_specs=[
            pl.BlockSpec((tm, d), lambda j, i: (i, 0)),
            pl.BlockSpec((1, d), lambda j, i: (0, 0)),
            pl.BlockSpec((d, tn), lambda j, i: (0, j)),
        ],
        out_specs=pl.BlockSpec((tm, tn), lambda j, i: (i, j)),
        out_shape=jax.ShapeDtypeStruct((t, n), F32),
        compiler_params=pltpu.CompilerParams(
            dimension_semantics=("parallel", "parallel"),
            vmem_limit_bytes=VMEM_LIMIT),
        name="in_proj",
    )(x2, g, w)


def _hgrn_body(q_ref, f_ref, i_ref, g_ref, lb_ref, gn_ref, o_ref, st_ref, *, ts):
    @pl.when(pl.program_id(1) == 0)
    def _():
        st_ref[...] = jnp.zeros_like(st_ref)

    row = lax.broadcasted_iota(jnp.int32, (CHUNK, CHUNK), 0)
    col = lax.broadcasted_iota(jnp.int32, (CHUNK, CHUNK), 1)
    tril_f = (col <= row).astype(F32)
    blk_r = row // SUB
    blk_c = col // SUB
    rblk = lax.broadcasted_iota(jnp.int32, (CHUNK, HD), 0) // SUB
    nsub = CHUNK // SUB
    gn = gn_ref[...]

    def chunk_body(c, carry):
        r0 = pl.multiple_of(c * CHUNK, CHUNK)
        rows = pl.ds(r0, CHUNK)
        hs = range(A_HEADS)
        cols = [slice(h * HD, (h + 1) * HD) for h in hs]
        q = [q_ref[0, rows, cols[h]] for h in hs]
        z = [f_ref[0, rows, cols[h]] for h in hs]
        v16 = [i_ref[0, rows, cols[h]].astype(BF16) for h in hs]
        lb = [lb_ref[h:h + 1, :] for h in hs]
        sts = [st_ref[h] for h in hs]
        k = [(1.0 - lb[h]) * _sigmoid(-z[h]) for h in hs]
        b = [_cumsum_mm(tril_f, jnp.log(lb[h] + (1.0 - lb[h]) * _sigmoid(z[h])), False)
             for h in hs]
        o = [_nt((q[h] * jnp.exp(b[h])).astype(BF16), sts[h].astype(BF16)) for h in hs]
        att = []
        for h in hs:
            bnd = [b[h][SUB * j + SUB - 1:SUB * j + SUB, :] for j in range(nsub)]
            k_off = k[h] * jnp.exp(
                _bcast_rows(b[h], [SUB * j + SUB - 1 for j in range(nsub)]) - b[h])
            qcat = jnp.concatenate(
                [q[h] * jnp.exp(jnp.minimum(b[h] - bnd[j], 0.0)) for j in range(nsub - 1)],
                axis=1)
            kcat = jnp.concatenate(
                [jnp.where(rblk == j, k_off, 0.0) for j in range(nsub - 1)], axis=1)
            att_off = _nt(qcat.astype(BF16), kcat.astype(BF16))
            ref_rows = _bcast_rows(b[h], [SUB * i for i in range(nsub)])
            q_d = q[h] * jnp.exp(b[h] - ref_rows)
            k_d = k[h] * jnp.exp(ref_rows - b[h])
            att_d = _nt(q_d.astype(BF16), k_d.astype(BF16))
            att.append(jnp.where(blk_r > blk_c, att_off,
                                 jnp.where((blk_r == blk_c) & (col <= row), att_d, 0.0)))
        o = [o[h] + _mm(att[h].astype(BF16), v16[h]) for h in hs]
        new_sts = []
        for h in hs:
            b_last = b[h][CHUNK - 1:CHUNK, :]
            kdec = k[h] * jnp.exp(b_last - b[h])
            new_sts.append(sts[h] * jnp.exp(b_last) + _tn(v16[h], kdec.astype(BF16)))
        for h in hs:
            st_ref[h] = new_sts[h]
        for h in hs:
            gate = g_ref[0, rows, cols[h]]
            y = o[h] * lax.rsqrt(jnp.mean(o[h] * o[h], axis=-1, keepdims=True) + EPS)
            o_ref[0, rows, cols[h]] = (y * gn * (gate * _sigmoid(gate))).astype(o_ref.dtype)
        return carry

    lax.fori_loop(0, ts // CHUNK, chunk_body, 0)


def _hgrn(proj3, lb, gn, ts):
    b, s, _ = proj3.shape
    w = A_HEADS * HD
    return pl.pallas_call(
        functools.partial(_hgrn_body, ts=ts),
        grid=(b, s // ts),
        in_specs=[
            pl.BlockSpec((1, ts, w), lambda bi, ti: (bi, ti, 0)),
            pl.BlockSpec((1, ts, w), lambda bi, ti: (bi, ti, 1)),
            pl.BlockSpec((1, ts, w), lambda bi, ti: (bi, ti, 2)),
            pl.BlockSpec((1, ts, w), lambda bi, ti: (bi, ti, 3)),
            pl.BlockSpec((A_HEADS, HD), lambda bi, ti: (0, 0)),
            pl.BlockSpec((1, HD), lambda bi, ti: (0, 0)),
        ],
        out_specs=pl.BlockSpec((1, ts, w), lambda bi, ti: (bi, ti, 0)),
        out_shape=jax.ShapeDtypeStruct((b, s, w), BF16),
        scratch_shapes=[pltpu.VMEM((A_HEADS, HD, HD), F32)],
        compiler_params=pltpu.CompilerParams(
            dimension_semantics=("parallel", "arbitrary"),
            vmem_limit_bytes=VMEM_LIMIT),
        name="hgrn",
    )(proj3, proj3, proj3, proj3, lb, gn)


def _gdn_body(bq_ref, bk_ref, bv_ref, bz_ref, sm_ref, smt_ref, cw_ref, pcol_ref, prow_ref,
              gn_ref, o_ref, s_ref, tail_ref, qn_ref, kn_ref, vv_ref,
              u_s, w_s, qg_s, kd_s, aqk_s, dl_s, *, ts):
    nqk = B_QK_HEADS * HD

    @pl.when(pl.program_id(1) == 0)
    def _():
        s_ref[...] = jnp.zeros_like(s_ref)
        tail_ref[...] = jnp.zeros_like(tail_ref)

    def conv_silu(x, c0):
        n = x.shape[1]
        xe = jnp.concatenate([tail_ref[:, c0:c0 + n], x], axis=0)
        w = cw_ref[:, c0:c0 + n]
        y = x * w[CONV - 1:CONV, :]
        for j in range(1, CONV):
            y = y + pltpu.roll(xe, j, axis=0)[8:, :] * w[CONV - 1 - j:CONV - j, :]
        tail_ref[:, c0:c0 + n] = x[ts - 8:, :]
        return y * _sigmoid(y)

    def l2n(x):
        return x * lax.rsqrt(jnp.sum(x * x, axis=-1, keepdims=True) + EPS)

    yq = conv_silu(bq_ref[0], 0)
    yk = conv_silu(bk_ref[0], nqk)
    vv_ref[...] = conv_silu(bv_ref[0], 2 * nqk)
    for h in range(B_QK_HEADS):
        cs = slice(h * HD, (h + 1) * HD)
        qn_ref[:, cs] = l2n(yq[:, cs]) * (HD ** -0.5)
        kn_ref[:, cs] = l2n(yk[:, cs])

    row = lax.broadcasted_iota(jnp.int32, (CHUNK, CHUNK), 0)
    col = lax.broadcasted_iota(jnp.int32, (CHUNK, CHUNK), 1)
    incl = col <= row
    strict = col < row
    tril_f = incl.astype(F32)
    triu_f = (row <= col).astype(F32)
    same_blk = (row // SUB) == (col // SUB)
    eye = (row == col).astype(F32)
    gn = gn_ref[...]
    a_col = pcol_ref[0:1, :]
    dt_col = pcol_ref[1:2, :]
    a_row = prow_ref[:, 0:1]
    dt_row = prow_ref[:, 1:2]
    rep = B_V_HEADS // B_QK_HEADS

    def prep_chunk(c, carry):
        r0 = pl.multiple_of(c * CHUNK, CHUNK)
        rows = pl.ds(r0, CHUNK)
        sm = sm_ref[0, rows, :]
        beta_c = _sigmoid(sm)
        gc_c = _cumsum_mm(tril_f, -a_col * _softplus(sm + dt_col), False)
        smt = smt_ref[0, c]
        gc_r = _cumsum_mm(triu_f, -a_row * _softplus(smt + dt_row), True)
        hs = range(B_V_HEADS)
        qq = [qn_ref[rows, (h // rep) * HD:(h // rep + 1) * HD] for h in hs]
        kk = [kn_ref[rows, (h // rep) * HD:(h // rep + 1) * HD] for h in hs]
        kk16 = [k.astype(BF16) for k in kk]
        bcol = [beta_c[:, h:h + 1] for h in hs]
        gcol = [gc_c[:, B_V_HEADS + h:B_V_HEADS + h + 1] for h in hs]
        decay = [jnp.where(incl, jnp.exp(jnp.minimum(
            gcol[h] - gc_r[B_V_HEADS + h:B_V_HEADS + h + 1, :], 0.0)), 0.0) for h in hs]
        kb = [kk[h] * bcol[h] for h in hs]
        a = [jnp.where(strict, _nt(kb[h].astype(BF16), kk16[h]) * decay[h], 0.0) for h in hs]
        dg = [jnp.where(same_blk, a[h], 0.0) for h in hs]
        off = [a[h] - dg[h] for h in hs]
        tinv = [eye - dg[h] for h in hs]
        p = dg
        for _ in range(3):
            p = [_mm3(p[h], p[h]) for h in hs]
            tinv = [tinv[h] + _mm3(tinv[h], p[h]) for h in hs]
        pm = [_mm3(tinv[h], off[h]) for h in hs]
        pm2 = [_mm3(pm[h], pm[h]) for h in hs]
        egc = [jnp.exp(gcol[h]) for h in hs]
        x = [_mm3(tinv[h], jnp.concatenate(
            [vv_ref[rows, h * HD:(h + 1) * HD] * bcol[h], kb[h] * egc[h]], axis=1)) for h in hs]
        x = [x[h] + _mm3(pm2[h], x[h]) for h in hs]
        x = [x[h] - _mm3(pm[h], x[h]) for h in hs]
        aqk = [jnp.where(incl, _nt(qq[h].astype(BF16), kk16[h]) * decay[h], 0.0) for h in hs]
        for h in hs:
            cs = slice(h * HD, (h + 1) * HD)
            g_last = gcol[h][CHUNK - 1:CHUNK, :]
            u_s[rows, cs] = x[h][:, :HD]
            w_s[rows, cs] = x[h][:, HD:].astype(BF16)
            qg_s[rows, cs] = (qq[h] * egc[h]).astype(BF16)
            kd_s[rows, cs] = (kk[h] * jnp.exp(g_last - gcol[h])).astype(BF16)
            aqk_s[c, h] = aqk[h].astype(BF16)
            dl_s[c, h:h + 1, :] = jnp.broadcast_to(jnp.exp(g_last), (1, HD))
        return carry

    lax.fori_loop(0, ts // CHUNK, prep_chunk, 0)

    def scan_chunk(c, carry):
        r0 = pl.multiple_of(c * CHUNK, CHUNK)
        rows = pl.ds(r0, CHUNK)
        hs = range(B_V_HEADS)
        cols = [slice(h * HD, (h + 1) * HD) for h in hs]
        sts = [s_ref[h] for h in hs]
        st16 = [s.astype(BF16) for s in sts]
        v16 = [(u_s[rows, cols[h]] - _mm(w_s[rows, cols[h]], st16[h])).astype(BF16) for h in hs]
        o = [_mm(qg_s[rows, cols[h]], st16[h]) + _mm(aqk_s[c, h], v16[h]) for h in hs]
        new_sts = [sts[h] * dl_s[c, h:h + 1, :] + _tn(kd_s[rows, cols[h]], v16[h]) for h in hs]
        for h in hs:
            s_ref[h] = new_sts[h]
        for h in hs:
            zz = bz_ref[0, rows, cols[h]]
            y = o[h] * lax.rsqrt(jnp.mean(o[h] * o[h], axis=-1, keepdims=True) + EPS)
            o_ref[0, rows, cols[h]] = (y * gn * (zz * _sigmoid(zz))).astype(o_ref.dtype)
        return carry

    lax.fori_loop(0, ts // CHUNK, scan_chunk, 0)


def _gdn(proj3, smt, conv_w, pcol, prow, gn, ts):
    b, s, _ = proj3.shape
    nqk = B_QK_HEADS * HD
    wv = B_V_HEADS * HD
    off_q = (4 * A_HEADS * HD) // nqk
    off_v = (4 * A_HEADS * HD + 2 * nqk) // wv
    off_s = (4 * A_HEADS * HD + 2 * nqk + 2 * wv) // HD
    nch = ts // CHUNK
    return pl.pallas_call(
        functools.partial(_gdn_body, ts=ts),
        grid=(b, s // ts),
        in_specs=[
            pl.BlockSpec((1, ts, nqk), lambda bi, ti: (bi, ti, off_q)),
            pl.BlockSpec((1, ts, nqk), lambda bi, ti: (bi, ti, off_q + 1)),
            pl.BlockSpec((1, ts, wv), lambda bi, ti: (bi, ti, off_v)),
            pl.BlockSpec((1, ts, wv), lambda bi, ti: (bi, ti, off_v + 1)),
            pl.BlockSpec((1, ts, HD), lambda bi, ti: (bi, ti, off_s)),
            pl.BlockSpec((1, nch, 2 * B_V_HEADS, CHUNK), lambda bi, ti: (bi, ti, 0, 0)),
            pl.BlockSpec((CONV, 2 * nqk + wv), lambda bi, ti: (0, 0)),
            pl.BlockSpec((2, HD), lambda bi, ti: (0, 0)),
            pl.BlockSpec((2 * B_V_HEADS, HD), lambda bi, ti: (0, 0)),
            pl.BlockSpec((1, HD), lambda bi, ti: (0, 0)),
        ],
        out_specs=pl.BlockSpec((1, ts, wv), lambda bi, ti: (bi, ti, 0)),
        out_shape=jax.ShapeDtypeStruct((b, s, wv), BF16),
        scratch_shapes=[
            pltpu.VMEM((B_V_HEADS, HD, HD), F32),
            pltpu.VMEM((8, 2 * nqk + wv), F32),
            pltpu.VMEM((ts, nqk), F32),
            pltpu.VMEM((ts, nqk), F32),
            pltpu.VMEM((ts, wv), F32),
            pltpu.VMEM((ts, wv), F32),
            pltpu.VMEM((ts, wv), BF16),
            pltpu.VMEM((ts, wv), BF16),
            pltpu.VMEM((ts, wv), BF16),
            pltpu.VMEM((nch, B_V_HEADS, CHUNK, CHUNK), BF16),
            pltpu.VMEM((nch, B_V_HEADS, HD), F32),
        ],
        compiler_params=pltpu.CompilerParams(
            dimension_semantics=("parallel", "arbitrary"),
            vmem_limit_bytes=VMEM_LIMIT),
        name="gdn",
    )(proj3, proj3, proj3, proj3, proj3, smt, conv_w, pcol, prow, gn)


def _outproj_body(x_ref, oa_ref, ob_ref, wo_ref, fg_ref, wq_ref, k1_ref, k2_ref,
                  h_ref, hn_ref, s1_ref, s2_ref):
    wa = oa_ref.shape[1]
    h = x_ref[...] + _mm(oa_ref[...], wo_ref[:wa, :]) + _mm(ob_ref[...], wo_ref[wa:, :])
    h_ref[...] = h
    hn = h * lax.rsqrt(jnp.mean(h * h, axis=-1, keepdims=True) + EPS) * fg_ref[...]
    hn_ref[...] = hn
    qry = _mm(hn.astype(BF16), wq_ref[...])
    for hh in range(P_HEADS):
        q1 = qry[:, hh * 2 * HD:hh * 2 * HD + HD].astype(BF16)
        q2 = qry[:, hh * 2 * HD + HD:(hh + 1) * 2 * HD].astype(BF16)
        s1_ref[hh] = _nt(k1_ref[hh].astype(BF16), q1)
        s2_ref[hh] = _nt(k2_ref[hh].astype(BF16), q2)


def _out_proj(x2, oa, ob, wo, fg, wq, k1, k2, tm):
    t, d = x2.shape
    wa = oa.shape[1]
    wb = ob.shape[1]
    const = dict(pipeline_mode=pl.Buffered(1))
    return pl.pallas_call(
        _outproj_body,
        grid=(t // tm,),
        in_specs=[
            pl.BlockSpec((tm, d), lambda i: (i, 0)),
            pl.BlockSpec((tm, wa), lambda i: (i, 0)),
            pl.BlockSpec((tm, wb), lambda i: (i, 0)),
            pl.BlockSpec((wa + wb, d), lambda i: (0, 0), **const),
            pl.BlockSpec((1, d), lambda i: (0, 0)),
            pl.BlockSpec((d, P_HEADS * 2 * HD), lambda i: (0, 0), **const),
            pl.BlockSpec((P_HEADS, N_KEYS, HD), lambda i: (0, 0, 0)),
            pl.BlockSpec((P_HEADS, N_KEYS, HD), lambda i: (0, 0, 0)),
        ],
        out_specs=[
            pl.BlockSpec((tm, d), lambda i: (i, 0)),
            pl.BlockSpec((tm, d), lambda i: (i, 0)),
            pl.BlockSpec((P_HEADS, N_KEYS, tm), lambda i: (0, 0, i)),
            pl.BlockSpec((P_HEADS, N_KEYS, tm), lambda i: (0, 0, i)),
        ],
        out_shape=[
            jax.ShapeDtypeStruct((t, d), F32),
            jax.ShapeDtypeStruct((t, d), F32),
            jax.ShapeDtypeStruct((P_HEADS, N_KEYS, t), F32),
            jax.ShapeDtypeStruct((P_HEADS, N_KEYS, t), F32),
        ],
        compiler_params=pltpu.CompilerParams(
            dimension_semantics=("parallel",),
            vmem_limit_bytes=VMEM_LIMIT),
        name="out_proj",
    )(x2, oa, ob, wo, fg, wq, k1, k2)


_PAIRS = [(i, j) for i in range(P_TOPK) for j in range(P_TOPK) if (i + 1) * (j + 1) <= P_TOPK]
_NCAND = -(-len(_PAIRS) // 8) * 8


def _topk_body(s1_ref, s2_ref, pos_ref, e_ref, g_ref, v_scr, i_scr, c_scr, ce_scr, b_scr, x_scr):
    tt = s1_ref.shape[2]
    kio = lax.broadcasted_iota(jnp.int32, (N_KEYS, tt), 0).astype(F32)
    neg = -jnp.inf
    for half, sref in ((0, s1_ref), (1, s2_ref)):
        s = sref[0]
        for r in range(P_TOPK):
            m = jnp.max(s, axis=0, keepdims=True)
            idx = jnp.min(jnp.where(s == m, kio, float(N_KEYS)), axis=0, keepdims=True)
            v_scr[half, r:r + 1, :] = m
            i_scr[half, r:r + 1, :] = idx
            s = jnp.where(kio == idx, neg, s)
    c_scr[...] = jnp.full(c_scr.shape, neg, F32)
    ce_scr[...] = jnp.zeros(ce_scr.shape, F32)
    for c, (i, j) in enumerate(_PAIRS):
        c_scr[c:c + 1, :] = v_scr[0, i:i + 1, :] + v_scr[1, j:j + 1, :]
        ce_scr[c:c + 1, :] = i_scr[0, i:i + 1, :] * float(N_KEYS) + i_scr[1, j:j + 1, :]
    cand = c_scr[...]
    ce = ce_scr[...]
    pos = jnp.broadcast_to(pos_ref[:, 0:1], cand.shape)
    for r in range(P_TOPK):
        m = jnp.max(cand, axis=0, keepdims=True)
        sel = jnp.min(jnp.where(cand == m, pos, 1e9), axis=0, keepdims=True)
        hit = pos == sel
        b_scr[r:r + 1, :] = m
        x_scr[r:r + 1, :] = jnp.max(jnp.where(hit, ce, -1.0), axis=0, keepdims=True)
        cand = jnp.where(hit, neg, cand)
    best = b_scr[...]
    ex = jnp.exp(best - best[0:1, :])
    g_ref[...] = ex / jnp.sum(ex, axis=0, keepdims=True)
    e_ref[...] = x_scr[...].astype(jnp.int32)


def _peer_topk(s1, s2, pos, tt):
    t = s1.shape[2]
    return pl.pallas_call(
        _topk_body,
        grid=(t // tt, P_HEADS),
        in_specs=[
            pl.BlockSpec((1, N_KEYS, tt), lambda i, h: (h, 0, i)),
            pl.BlockSpec((1, N_KEYS, tt), lambda i, h: (h, 0, i)),
            pl.BlockSpec((_NCAND, HD), lambda i, h: (0, 0)),
        ],
        out_specs=[
            pl.BlockSpec((P_TOPK, tt), lambda i, h: (h, i)),
            pl.BlockSpec((P_TOPK, tt), lambda i, h: (h, i)),
        ],
        out_shape=[
            jax.ShapeDtypeStruct((P_HEADS * P_TOPK, t), jnp.int32),
            jax.ShapeDtypeStruct((P_HEADS * P_TOPK, t), F32),
        ],
        scratch_shapes=[
            pltpu.VMEM((2, P_TOPK, tt), F32),
            pltpu.VMEM((2, P_TOPK, tt), F32),
            pltpu.VMEM((_NCAND, tt), F32),
            pltpu.VMEM((_NCAND, tt), F32),
            pltpu.VMEM((P_TOPK, tt), F32),
            pltpu.VMEM((P_TOPK, tt), F32),
        ],
        compiler_params=pltpu.CompilerParams(
            dimension_semantics=("parallel", "parallel"),
            vmem_limit_bytes=VMEM_LIMIT),
        name="peer_topk",
    )(s1, s2, pos)


PEER_TB = 128
PEER_G = 8
PEER_SETS = 4
DMA_QUEUES = 2
NSLOT = P_HEADS * P_TOPK
SG = 8
CC_AHEAD = 3


HID_UNROLL = 1
MIX_UNROLL = 2


def _gelu(x):
    return 0.5 * x * (1.0 + lax.erf(x * (2.0 ** -0.5)))


def _unrolled_loop(n, unroll, body):
    def trip(q, carry):
        for j in range(unroll):
            body(q * unroll + j, 0)
        return carry
    lax.fori_loop(0, n // unroll, trip, 0)


def _pack_body(u_ref, v_ref, o_ref):
    ub = lax.bitcast_convert_type(u_ref[...].astype(BF16).astype(F32), jnp.uint32) >> 16
    vb = lax.bitcast_convert_type(v_ref[...].astype(BF16).astype(F32), jnp.uint32)
    w = (vb & jnp.uint32(0xFFFF0000)) | ub
    for c in range(w.shape[1] // HD):
        o_ref[:, c, 0, :] = w[:, c * HD:(c + 1) * HD]


def _peer_pack(u_tab, v_tab, te):
    e, d = u_tab.shape
    return pl.pallas_call(
        _pack_body,
        grid=(e // te,),
        in_specs=[pl.BlockSpec((te, d), lambda i: (i, 0)),
                  pl.BlockSpec((te, d), lambda i: (i, 0))],
        out_specs=pl.BlockSpec((te, d // HD, 1, HD), lambda i: (i, 0, 0, 0)),
        out_shape=jax.ShapeDtypeStruct((e, d // HD, 1, HD), jnp.uint32),
        compiler_params=pltpu.CompilerParams(
            dimension_semantics=("parallel",),
            vmem_limit_bytes=VMEM_LIMIT),
        name="peer_pack",
    )(u_tab, v_tab)


def _peer_body(idx_ref, gt_ref, hn_ref, h_ref, fg_ref, w_hbm, o_ref,
               hid_scr, c_scr, y_scr, sem, *bufs, final, nsteps):
    tb, d = hn_ref.shape
    ngroups = tb // PEER_G
    nsg = NSLOT // SG
    ahead = PEER_SETS - 1
    step = pl.program_id(0)

    def issue_tok(tok, set_, r, s0, s1):
        ids = idx_ref.at[tok]
        dst = bufs[set_].at[:, pl.ds(r * NSLOT, NSLOT), :]
        for s in range(s0, s1):
            pltpu.make_async_copy(w_hbm.at[ids[s]], dst.at[:, pl.ds(s, 1), :],
                                  sem.at[set_]).start(priority=s % DMA_QUEUES)

    def wait_set(set_):
        pltpu.make_async_copy(bufs[set_], bufs[set_], sem.at[set_]).wait()

    @pl.when(step == 0)
    def _():
        def first(r, carry):
            for a in range(ahead):
                issue_tok(a * PEER_G + r, a, r, 0, NSLOT)
            return carry
        lax.fori_loop(0, PEER_G, first, 0)

    nlc = d // HD
    nsg = NSLOT // SG
    per_tile = NSLOT // 2 // nsg
    lane = lax.broadcasted_iota(jnp.int32, (SG, tb), 1)
    hid_scr[...] = jnp.zeros_like(hid_scr)

    def group_body(g, set_):
        nxt = (set_ + ahead) % PEER_SETS
        wait_set(set_)


        def hid_tok(r, c2):
            t = g * PEER_G + r
            xrow = hn_ref[pl.ds(t, 1), :]
            xb = [jnp.broadcast_to(xrow[:, lc * HD:(lc + 1) * HD], (SG, HD)) for lc in range(nlc)]
            tiles = bufs[set_].at[:, pl.ds(r * NSLOT, NSLOT), :]
            hs = []
            for sg in range(nsg):
                issue_tok((g + ahead) * PEER_G + r, nxt, r, sg * per_tile, (sg + 1) * per_tile)
                acc = None
                for lc in range(nlc):
                    u = lax.bitcast_convert_type(tiles[lc, sg * SG:(sg + 1) * SG, :] << 16, F32)
                    term = u * xb[lc]
                    acc = term if acc is None else acc + term
                hs.append(jnp.sum(acc, axis=1, keepdims=True))
            for sg in range(nsg):
                pltpu.store(hid_scr.at[pl.ds(sg * SG, SG), :],
                            jnp.broadcast_to(hs[sg], (SG, tb)), mask=lane == t)
            return c2
        _unrolled_loop(PEER_G, HID_UNROLL, hid_tok)

        c_scr[...] = gt_ref[...] * _gelu(hid_scr[...])

        def mix_tok(r, c2):
            t = g * PEER_G + r
            tiles = bufs[set_].at[:, pl.ds(r * NSLOT, NSLOT), :]
            def coef_col(sg):
                return jnp.sum(jnp.where(lane == t, c_scr[pl.ds(sg * SG, SG), :], 0.0),
                               axis=1, keepdims=True)
            cc = {sg: coef_col(sg) for sg in range(CC_AHEAD)}
            accs = [None] * nlc
            for sg in range(nsg):
                issue_tok((g + ahead) * PEER_G + r, nxt, r,
                          NSLOT // 2 + sg * per_tile, NSLOT // 2 + (sg + 1) * per_tile)
                if sg + CC_AHEAD < nsg:
                    cc[sg + CC_AHEAD] = coef_col(sg + CC_AHEAD)
                for lc in range(nlc):
                    v = lax.bitcast_convert_type(
                        tiles[lc, sg * SG:(sg + 1) * SG, :] & jnp.uint32(0xFFFF0000), F32)
                    term = v * cc[sg]
                    accs[lc] = term if accs[lc] is None else accs[lc] + term
            yrow = jnp.concatenate(
                [jnp.sum(a, axis=0, keepdims=True) for a in accs], axis=1)
            y_scr[pl.ds(t, 1), :] = yrow
            return c2
        _unrolled_loop(PEER_G, MIX_UNROLL, mix_tok)

    def group_round(q, carry):
        for j in range(PEER_SETS):
            group_body(q * PEER_SETS + j, j)
        return carry

    lax.fori_loop(0, ngroups // PEER_SETS, group_round, 0)

    @pl.when(step == nsteps - 1)
    def _():
        for a in range(ahead):
            wait_set((nsteps * ngroups + a) % PEER_SETS)

    hh = h_ref[...] + y_scr[...]
    if final:
        hh = hh * lax.rsqrt(jnp.mean(hh * hh, axis=-1, keepdims=True) + EPS) * fg_ref[...]
    o_ref[...] = hh


def _peer_mix(idx, gt, hn, h, fg, w_tab, final):
    t, d = hn.shape
    tb = PEER_TB
    nsteps = t // tb
    extra = (PEER_SETS - 1) * PEER_G
    assert (tb // PEER_G) % PEER_SETS == 0
    idx_pad = jnp.concatenate([idx, jnp.zeros((tb, NSLOT), idx.dtype)], axis=0)
    idx_ext = jnp.concatenate(
        [idx.reshape(nsteps, tb, NSLOT),
         idx_pad[tb:].reshape(nsteps, tb, NSLOT)[:, :extra]], axis=1)
    return pl.pallas_call(
        functools.partial(_peer_body, final=final, nsteps=nsteps),
        grid=(nsteps,),
        in_specs=[
            pl.BlockSpec((None, tb + extra, NSLOT), lambda i: (i, 0, 0),
                         memory_space=pltpu.SMEM),
            pl.BlockSpec((NSLOT, tb), lambda i: (0, i)),
            pl.BlockSpec((tb, d), lambda i: (i, 0)),
            pl.BlockSpec((tb, d), lambda i: (i, 0)),
            pl.BlockSpec((1, d), lambda i: (0, 0)),
            pl.BlockSpec(memory_space=pl.ANY),
        ],
        out_specs=pl.BlockSpec((tb, d), lambda i: (i, 0)),
        out_shape=jax.ShapeDtypeStruct((t, d), F32),
        scratch_shapes=[
            pltpu.VMEM((NSLOT, tb), F32),
            pltpu.VMEM((NSLOT, tb), F32),
            pltpu.VMEM((tb, d), F32),
            pltpu.SemaphoreType.DMA((PEER_SETS,)),
        ] + [pltpu.VMEM((d // HD, PEER_G * NSLOT, HD), jnp.uint32) for _ in range(PEER_SETS)],
        compiler_params=pltpu.CompilerParams(
            dimension_semantics=("arbitrary",),
            vmem_limit_bytes=VMEM_LIMIT),
        name="peer_mix",
    )(idx_ext, gt, hn, h, fg, w_tab)


def _tile(n, pref):
    return pref if n % pref == 0 else n


def kernel(x, attn_norm_g, w_in, hgrn_lb_logits, hgrn_norm_g, gdn_conv_w, gdn_A_log, gdn_dt_bias,
           gdn_norm_g, w_out, ffn_norm_g, peer_w_query, peer_sub_keys, peer_u, peer_v,
           final_norm_g):
    b, s, d = x.shape
    t = b * s
    depth = w_in.shape[0]
    in_width = w_in.shape[2]
    n_small = 2 * B_V_HEADS
    n_main = in_width - n_small
    in_pad = n_main + HD
    lb_all = jnp.cumsum(jax.nn.softmax(hgrn_lb_logits.astype(F32), axis=0), axis=0)
    pos = jnp.array([i * P_TOPK + j for i, j in _PAIRS]
                    + [10 ** 6 + c for c in range(_NCAND - len(_PAIRS))], F32)
    pos = jnp.broadcast_to(pos[:, None], (_NCAND, HD))

    h2 = x.reshape(t, d)
    for l in range(depth):
        w_l = jnp.pad(w_in[l], ((0, 0), (0, in_pad - in_width))).astype(BF16)
        tn = in_pad // 3 if (in_pad // HD) % 3 == 0 else in_pad
        proj = _in_proj(h2, attn_norm_g[l][None, :], w_l, _tile(t, 512), tn)
        proj3 = proj.reshape(b, s, in_pad)

        o_a = _hgrn(proj3, lb_all[l], hgrn_norm_g[l][None, :], _tile(s, 256))

        ts = _tile(s, 256)
        small = proj3[:, :, n_main:n_main + n_small]
        smt = small.reshape(b, s // CHUNK, CHUNK, n_small).transpose(0, 1, 3, 2)
        zeros8 = jnp.zeros((B_V_HEADS,), F32)
        a_neg = jnp.exp(gdn_A_log[l].astype(F32))
        dtb = gdn_dt_bias[l].astype(F32)
        pcol = jnp.zeros((2, HD), F32)
        pcol = pcol.at[0, B_V_HEADS:n_small].set(a_neg).at[1, B_V_HEADS:n_small].set(dtb)
        prow = jnp.zeros((n_small, HD), F32)
        prow = prow.at[:, 0].set(jnp.concatenate([zeros8, a_neg]))
        prow = prow.at[:, 1].set(jnp.concatenate([zeros8, dtb]))
        o_b = _gdn(proj3, smt, gdn_conv_w[l], pcol, prow, gdn_norm_g[l][None, :], ts)

        wq = peer_w_query[l].reshape(d, P_HEADS * 2 * HD).astype(BF16)
        h2, hn, s1, s2 = _out_proj(
            h2, o_a.reshape(t, -1), o_b.reshape(t, -1), w_out[l].astype(BF16),
            ffn_norm_g[l][None, :], wq, peer_sub_keys[l, 0], peer_sub_keys[l, 1], _tile(t, 256))

        e_t, g_t = _peer_topk(s1, s2, pos, _tile(t, 256))
        w_tab = _peer_pack(peer_u[l], peer_v[l], _tile(peer_u.shape[1], 256))
        h2 = _peer_mix(e_t.T, g_t, hn, h2, final_norm_g[None, :], w_tab, l == depth - 1)
    return h2.reshape(b, s, d)
```

```python
import functools

import jax
import jax.numpy as jnp
from jax import lax
from jax.experimental import pallas as pl
from jax.experimental.pallas import tpu as pltpu

F32 = jnp.float32
BF16 = jnp.bfloat16

EPS = 1e-6
CHUNK = 64
SUB = 16
HD = 128
A_HEADS = 8
B_QK_HEADS = 4
B_V_HEADS = 8
CONV = 4
P_HEADS = 8
N_KEYS = 128
P_TOPK = 16
VMEM_LIMIT = 56 * 1024 * 1024


def _nt(a, b):
    return lax.dot_general(a, b, (((1,), (1,)), ((), ())), preferred_element_type=F32)


def _tn(a, b):
    return lax.dot_general(a, b, (((0,), (0,)), ((), ())), preferred_element_type=F32)


def _mm(a, b):
    return jnp.dot(a, b, preferred_element_type=F32)


def _split_bf16(x, n):
    parts = []
    for _ in range(n - 1):
        p = x.astype(BF16)
        parts.append(p)
        x = x - p.astype(F32)
    parts.append(x.astype(BF16))
    return parts


def _mm3(a, b):
    ah, al = _split_bf16(a, 2)
    bh, bl = _split_bf16(b, 2)
    return _mm(ah, bh) + (_mm(ah, bl) + _mm(al, bh))


def _cumsum_mm(tri, g, g_is_lhs):
    t16 = tri.astype(BF16)
    out = None
    for p in reversed(_split_bf16(g, 3)):
        term = _mm(p, t16) if g_is_lhs else _mm(t16, p)
        out = term if out is None else out + term
    return out


def _sigmoid(x):
    return 1.0 / (1.0 + jnp.exp(-x))


def _softplus(x):
    return jnp.maximum(x, 0.0) + jnp.log1p(jnp.exp(-jnp.abs(x)))


def _bcast_rows(x, idx):
    n = x.shape[1]
    return jnp.concatenate(
        [jnp.broadcast_to(x[r:r + 1, :], (SUB, n)) for r in idx], axis=0)


def _inproj_body(x_ref, g_ref, w_ref, o_ref):
    x = x_ref[...]
    ms = jnp.mean(x * x, axis=-1, keepdims=True)
    xn = (x * lax.rsqrt(ms + EPS) * g_ref[...]).astype(BF16)
    o_ref[...] = _mm(xn, w_ref[...])


def _in_proj(x2, g, w, tm, tn):
    t, d = x2.shape
    n = w.shape[1]
    return pl.pallas_call(
        _inproj_body,
        grid=(n // tn, t // tm),
        in_specs=[
            pl.BlockSpec((tm, d), lambda j, i: (i, 0)),
            pl.BlockSpec((1, d), lambda j, i: (0, 0)),
            pl.BlockSpec((d, tn), lambda j, i: (0, j)),
        ],
        out_specs=pl.BlockSpec((tm, tn), lambda j, i: (i, j)),
        out_shape=jax.ShapeDtypeStruct((t, n), F32),
        compiler_params=pltpu.CompilerParams(
            dimension_semantics=("parallel", "parallel"),
            vmem_limit_bytes=VMEM_LIMIT),
        name="in_proj",
    )(x2, g, w)


def _hgrn_body(q_ref, f_ref, i_ref, g_ref, lb_ref, gn_ref, o_ref, st_ref, *, ts):
    @pl.when(pl.program_id(1) == 0)
    def _():
        st_ref[...] = jnp.zeros_like(st_ref)

    row = lax.broadcasted_iota(jnp.int32, (CHUNK, CHUNK), 0)
    col = lax.broadcasted_iota(jnp.int32, (CHUNK, CHUNK), 1)
    tril_f = (col <= row).astype(F32)
    blk_r = row // SUB
    blk_c = col // SUB
    rblk = lax.broadcasted_iota(jnp.int32, (CHUNK, HD), 0) // SUB
    nsub = CHUNK // SUB
    gn = gn_ref[...]

    def chunk_body(c, carry):
        r0 = pl.multiple_of(c * CHUNK, CHUNK)
        rows = pl.ds(r0, CHUNK)
        hs = range(A_HEADS)
        cols = [slice(h * HD, (h + 1) * HD) for h in hs]
        q = [q_ref[0, rows, cols[h]] for h in hs]
        z = [f_ref[0, rows, cols[h]] for h in hs]
        v16 = [i_ref[0, rows, cols[h]].astype(BF16) for h in hs]
        lb = [lb_ref[h:h + 1, :] for h in hs]
        sts = [st_ref[h] for h in hs]
        k = [(1.0 - lb[h]) * _sigmoid(-z[h]) for h in hs]
        b = [_cumsum_mm(tril_f, jnp.log(lb[h] + (1.0 - lb[h]) * _sigmoid(z[h])), False)
             for h in hs]
        o = [_nt((q[h] * jnp.exp(b[h])).astype(BF16), sts[h].astype(BF16)) for h in hs]
        att = []
        for h in hs:
            bnd = [b[h][SUB * j + SUB - 1:SUB * j + SUB, :] for j in range(nsub)]
            k_off = k[h] * jnp.exp(
                _bcast_rows(b[h], [SUB * j + SUB - 1 for j in range(nsub)]) - b[h])
            qcat = jnp.concatenate(
                [q[h] * jnp.exp(jnp.minimum(b[h] - bnd[j], 0.0)) for j in range(nsub - 1)],
                axis=1)
            kcat = jnp.concatenate(
                [jnp.where(rblk == j, k_off, 0.0) for j in range(nsub - 1)], axis=1)
            att_off = _nt(qcat.astype(BF16), kcat.astype(BF16))
            ref_rows = _bcast_rows(b[h], [SUB * i for i in range(nsub)])
            q_d = q[h] * jnp.exp(b[h] - ref_rows)
            k_d = k[h] * jnp.exp(ref_rows - b[h])
            att_d = _nt(q_d.astype(BF16), k_d.astype(BF16))
            att.append(jnp.where(blk_r > blk_c, att_off,
                                 jnp.where((blk_r == blk_c) & (col <= row), att_d, 0.0)))
        o = [o[h] + _mm(att[h].astype(BF16), v16[h]) for h in hs]
        new_sts = []
        for h in hs:
            b_last = b[h][CHUNK - 1:CHUNK, :]
            kdec = k[h] * jnp.exp(b_last - b[h])
            new_sts.append(sts[h] * jnp.exp(b_last) + _tn(v16[h], kdec.astype(BF16)))
        for h in hs:
            st_ref[h] = new_sts[h]
        for h in hs:
            gate = g_ref[0, rows, cols[h]]
            y = o[h] * lax.rsqrt(jnp.mean(o[h] * o[h], axis=-1, keepdims=True) + EPS)
            o_ref[0, rows, cols[h]] = (y * gn * (gate * _sigmoid(gate))).astype(o_ref.dtype)
        return carry

    lax.fori_loop(0, ts // CHUNK, chunk_body, 0)


def _hgrn(proj3, lb, gn, ts):
    b, s, _ = proj3.shape
    w = A_HEADS * HD
    return pl.pallas_call(
        functools.partial(_hgrn_body, ts=ts),
        grid=(b, s // ts),
        in_specs=[
            pl.BlockSpec((1, ts, w), lambda bi, ti: (bi, ti, 0)),
            pl.BlockSpec((1, ts, w), lambda bi, ti: (bi, ti, 1)),
            pl.BlockSpec((1, ts, w), lambda bi, ti: (bi, ti, 2)),
            pl.BlockSpec((1, ts, w), lambda bi, ti: (bi, ti, 3)),
            pl.BlockSpec((A_HEADS, HD), lambda bi, ti: (0, 0)),
            pl.BlockSpec((1, HD), lambda bi, ti: (0, 0)),
        ],
        out_specs=pl.BlockSpec((1, ts, w), lambda bi, ti: (bi, ti, 0)),
        out_shape=jax.ShapeDtypeStruct((b, s, w), BF16),
        scratch_shapes=[pltpu.VMEM((A_HEADS, HD, HD), F32)],
        compiler_params=pltpu.CompilerParams(
            dimension_semantics=("parallel", "arbitrary"),
            vmem_limit_bytes=VMEM_LIMIT),
        name="hgrn",
    )(proj3, proj3, proj3, proj3, lb, gn)


def _gdn_body(bq_ref, bk_ref, bv_ref, bz_ref, sm_ref, smt_ref, cw_ref, pcol_ref, prow_ref,
              gn_ref, o_ref, s_ref, tail_ref, qn_ref, kn_ref, vv_ref,
              u_s, w_s, qg_s, kd_s, aqk_s, dl_s, *, ts):
    nqk = B_QK_HEADS * HD

    @pl.when(pl.program_id(1) == 0)
    def _():
        s_ref[...] = jnp.zeros_like(s_ref)
        tail_ref[...] = jnp.zeros_like(tail_ref)

    def conv_silu(x_ref, h, c0):
        cs = slice(h * HD, (h + 1) * HD)
        cc = slice(c0 + h * HD, c0 + (h + 1) * HD)
        x = x_ref[0, :, cs]
        xe = jnp.concatenate([tail_ref[:, cc], x], axis=0)
        w = cw_ref[:, cc]
        y = x * w[CONV - 1:CONV, :]
        for j in range(1, CONV):
            y = y + pltpu.roll(xe, j, axis=0)[8:, :] * w[CONV - 1 - j:CONV - j, :]
        tail_ref[:, cc] = x[ts - 8:, :]
        return y * _sigmoid(y)

    def l2n(x):
        return x * lax.rsqrt(jnp.sum(x * x, axis=-1, keepdims=True) + EPS)

    for h in range(B_QK_HEADS):
        cs = slice(h * HD, (h + 1) * HD)
        qn_ref[:, cs] = l2n(conv_silu(bq_ref, h, 0)) * (HD ** -0.5)
        kn_ref[:, cs] = l2n(conv_silu(bk_ref, h, nqk))
    for h in range(B_V_HEADS):
        vv_ref[:, h * HD:(h + 1) * HD] = conv_silu(bv_ref, h, 2 * nqk)

    row = lax.broadcasted_iota(jnp.int32, (CHUNK, CHUNK), 0)
    col = lax.broadcasted_iota(jnp.int32, (CHUNK, CHUNK), 1)
    incl = col <= row
    strict = col < row
    tril_f = incl.astype(F32)
    triu_f = (row <= col).astype(F32)
    same_blk = (row // SUB) == (col // SUB)
    eye = (row == col).astype(F32)
    gn = gn_ref[...]
    a_col = pcol_ref[0:1, :]
    dt_col = pcol_ref[1:2, :]
    a_row = prow_ref[:, 0:1]
    dt_row = prow_ref[:, 1:2]
    rep = B_V_HEADS // B_QK_HEADS

    def prep_chunk(c, carry):
        r0 = pl.multiple_of(c * CHUNK, CHUNK)
        rows = pl.ds(r0, CHUNK)
        sm = sm_ref[0, rows, :]
        beta_c = _sigmoid(sm)
        gc_c = _cumsum_mm(tril_f, -a_col * _softplus(sm + dt_col), False)
        smt = smt_ref[0, c]
        gc_r = _cumsum_mm(triu_f, -a_row * _softplus(smt + dt_row), True)
        hs = range(B_V_HEADS)
        qq = [qn_ref[rows, (h // rep) * HD:(h // rep + 1) * HD] for h in hs]
        kk = [kn_ref[rows, (h // rep) * HD:(h // rep + 1) * HD] for h in hs]
        kk16 = [k.astype(BF16) for k in kk]
        bcol = [beta_c[:, h:h + 1] for h in hs]
        gcol = [gc_c[:, B_V_HEADS + h:B_V_HEADS + h + 1] for h in hs]
        decay = [jnp.where(incl, jnp.exp(jnp.minimum(
            gcol[h] - gc_r[B_V_HEADS + h:B_V_HEADS + h + 1, :], 0.0)), 0.0) for h in hs]
        kb = [kk[h] * bcol[h] for h in hs]
        a = [jnp.where(strict, _nt(kb[h].astype(BF16), kk16[h]) * decay[h], 0.0) for h in hs]
        dg = [jnp.where(same_blk, a[h], 0.0) for h in hs]
        off = [a[h] - dg[h] for h in hs]
        tinv = [eye - dg[h] for h in hs]
        p = dg
        for _ in range(3):
            p = [_mm3(p[h], p[h]) for h in hs]
            tinv = [tinv[h] + _mm3(tinv[h], p[h]) for h in hs]
        pm = [_mm3(tinv[h], off[h]) for h in hs]
        pm2 = [_mm3(pm[h], pm[h]) for h in hs]
        egc = [jnp.exp(gcol[h]) for h in hs]
        x = [_mm3(tinv[h], jnp.concatenate(
            [vv_ref[rows, h * HD:(h + 1) * HD] * bcol[h], kb[h] * egc[h]], axis=1)) for h in hs]
        x = [x[h] + _mm3(pm2[h], x[h]) for h in hs]
        x = [x[h] - _mm3(pm[h], x[h]) for h in hs]
        aqk = [jnp.where(incl, _nt(qq[h].astype(BF16), kk16[h]) * decay[h], 0.0) for h in hs]
        for h in hs:
            cs = slice(h * HD, (h + 1) * HD)
            g_last = gcol[h][CHUNK - 1:CHUNK, :]
            u_s[rows, cs] = x[h][:, :HD]
            w_s[rows, cs] = x[h][:, HD:].astype(BF16)
            qg_s[rows, cs] = (qq[h] * egc[h]).astype(BF16)
            kd_s[rows, cs] = (kk[h] * jnp.exp(g_last - gcol[h])).astype(BF16)
            aqk_s[c, h] = aqk[h].astype(BF16)
            dl_s[c, h:h + 1, :] = jnp.broadcast_to(jnp.exp(g_last), (1, HD))
        return carry

    lax.fori_loop(0, ts // CHUNK, prep_chunk, 0)

    def scan_chunk(c, carry):
        r0 = pl.multiple_of(c * CHUNK, CHUNK)
        rows = pl.ds(r0, CHUNK)
        hs = range(B_V_HEADS)
        cols = [slice(h * HD, (h + 1) * HD) for h in hs]
        sts = [s_ref[h] for h in hs]
        st16 = [s.astype(BF16) for s in sts]
        v16 = [(u_s[rows, cols[h]] - _mm(w_s[rows, cols[h]], st16[h])).astype(BF16) for h in hs]
        o = [_mm(qg_s[rows, cols[h]], st16[h]) + _mm(aqk_s[c, h], v16[h]) for h in hs]
        new_sts = [sts[h] * dl_s[c, h:h + 1, :] + _tn(kd_s[rows, cols[h]], v16[h]) for h in hs]
        for h in hs:
            s_ref[h] = new_sts[h]
        for h in hs:
            zz = bz_ref[0, rows, cols[h]]
            y = o[h] * lax.rsqrt(jnp.mean(o[h] * o[h], axis=-1, keepdims=True) + EPS)
            o_ref[0, rows, cols[h]] = (y * gn * (zz * _sigmoid(zz))).astype(o_ref.dtype)
        return carry

    lax.fori_loop(0, ts // CHUNK, scan_chunk, 0)


def _gdn(proj3, smt, conv_w, pcol, prow, gn, ts):
    b, s, _ = proj3.shape
    nqk = B_QK_HEADS * HD
    wv = B_V_HEADS * HD
    off_q = (4 * A_HEADS * HD) // nqk
    off_v = (4 * A_HEADS * HD + 2 * nqk) // wv
    off_s = (4 * A_HEADS * HD + 2 * nqk + 2 * wv) // HD
    nch = ts // CHUNK
    return pl.pallas_call(
        functools.partial(_gdn_body, ts=ts),
        grid=(b, s // ts),
        in_specs=[
            pl.BlockSpec((1, ts, nqk), lambda bi, ti: (bi, ti, off_q)),
            pl.BlockSpec((1, ts, nqk), lambda bi, ti: (bi, ti, off_q + 1)),
            pl.BlockSpec((1, ts, wv), lambda bi, ti: (bi, ti, off_v)),
            pl.BlockSpec((1, ts, wv), lambda bi, ti: (bi, ti, off_v + 1)),
            pl.BlockSpec((1, ts, HD), lambda bi, ti: (bi, ti, off_s)),
            pl.BlockSpec((1, nch, 2 * B_V_HEADS, CHUNK), lambda bi, ti: (bi, ti, 0, 0)),
            pl.BlockSpec((CONV, 2 * nqk + wv), lambda bi, ti: (0, 0)),
            pl.BlockSpec((2, HD), lambda bi, ti: (0, 0)),
            pl.BlockSpec((2 * B_V_HEADS, HD), lambda bi, ti: (0, 0)),
            pl.BlockSpec((1, HD), lambda bi, ti: (0, 0)),
        ],
        out_specs=pl.BlockSpec((1, ts, wv), lambda bi, ti: (bi, ti, 0)),
        out_shape=jax.ShapeDtypeStruct((b, s, wv), BF16),
        scratch_shapes=[
            pltpu.VMEM((B_V_HEADS, HD, HD), F32),
            pltpu.VMEM((8, 2 * nqk + wv), F32),
            pltpu.VMEM((ts, nqk), F32),
            pltpu.VMEM((ts, nqk), F32),
            pltpu.VMEM((ts, wv), F32),
            pltpu.VMEM((ts, wv), F32),
            pltpu.VMEM((ts, wv), BF16),
            pltpu.VMEM((ts, wv), BF16),
            pltpu.VMEM((ts, wv), BF16),
            pltpu.VMEM((nch, B_V_HEADS, CHUNK, CHUNK), BF16),
            pltpu.VMEM((nch, B_V_HEADS, HD), F32),
        ],
        compiler_params=pltpu.CompilerParams(
            dimension_semantics=("parallel", "arbitrary"),
            vmem_limit_bytes=VMEM_LIMIT),
        name="gdn",
    )(proj3, proj3, proj3, proj3, proj3, smt, conv_w, pcol, prow, gn)


def _outproj_body(x_ref, oa_ref, ob_ref, wo_ref, fg_ref, wq_ref, k1_ref, k2_ref,
                  h_ref, hn_ref, s1_ref, s2_ref):
    wa = oa_ref.shape[1]
    h = x_ref[...] + _mm(oa_ref[...], wo_ref[:wa, :]) + _mm(ob_ref[...], wo_ref[wa:, :])
    h_ref[...] = h
    hn = h * lax.rsqrt(jnp.mean(h * h, axis=-1, keepdims=True) + EPS) * fg_ref[...]
    hn_ref[...] = hn
    qry = _mm(hn.astype(BF16), wq_ref[...])
    for hh in range(P_HEADS):
        q1 = qry[:, hh * 2 * HD:hh * 2 * HD + HD].astype(BF16)
        q2 = qry[:, hh * 2 * HD + HD:(hh + 1) * 2 * HD].astype(BF16)
        s1_ref[hh] = _nt(k1_ref[hh].astype(BF16), q1)
        s2_ref[hh] = _nt(k2_ref[hh].astype(BF16), q2)


def _out_proj(x2, oa, ob, wo, fg, wq, k1, k2, tm):
    t, d = x2.shape
    wa = oa.shape[1]
    wb = ob.shape[1]
    const = dict(pipeline_mode=pl.Buffered(1))
    return pl.pallas_call(
        _outproj_body,
        grid=(t // tm,),
        in_specs=[
            pl.BlockSpec((tm, d), lambda i: (i, 0)),
            pl.BlockSpec((tm, wa), lambda i: (i, 0)),
            pl.BlockSpec((tm, wb), lambda i: (i, 0)),
            pl.BlockSpec((wa + wb, d), lambda i: (0, 0), **const),
            pl.BlockSpec((1, d), lambda i: (0, 0)),
            pl.BlockSpec((d, P_HEADS * 2 * HD), lambda i: (0, 0), **const),
            pl.BlockSpec((P_HEADS, N_KEYS, HD), lambda i: (0, 0, 0)),
            pl.BlockSpec((P_HEADS, N_KEYS, HD), lambda i: (0, 0, 0)),
        ],
        out_specs=[
            pl.BlockSpec((tm, d), lambda i: (i, 0)),
            pl.BlockSpec((tm, d), lambda i: (i, 0)),
            pl.BlockSpec((P_HEADS, N_KEYS, tm), lambda i: (0, 0, i)),
            pl.BlockSpec((P_HEADS, N_KEYS, tm), lambda i: (0, 0, i)),
        ],
        out_shape=[
            jax.ShapeDtypeStruct((t, d), F32),
            jax.ShapeDtypeStruct((t, d), F32),
            jax.ShapeDtypeStruct((P_HEADS, N_KEYS, t), F32),
            jax.ShapeDtypeStruct((P_HEADS, N_KEYS, t), F32),
        ],
        compiler_params=pltpu.CompilerParams(
            dimension_semantics=("parallel",),
            vmem_limit_bytes=VMEM_LIMIT),
        name="out_proj",
    )(x2, oa, ob, wo, fg, wq, k1, k2)


_PAIRS = [(i, j) for i in range(P_TOPK) for j in range(P_TOPK) if (i + 1) * (j + 1) <= P_TOPK]
_NCAND = -(-len(_PAIRS) // 8) * 8


def _topk_body(s1_ref, s2_ref, pos_ref, e_ref, g_ref, v_scr, i_scr, c_scr, ce_scr, b_scr, x_scr):
    tt = s1_ref.shape[2]
    kio = lax.broadcasted_iota(jnp.int32, (N_KEYS, tt), 0).astype(F32)
    neg = -jnp.inf
    for half, sref in ((0, s1_ref), (1, s2_ref)):
        s = sref[0]
        for r in range(P_TOPK):
            m = jnp.max(s, axis=0, keepdims=True)
            idx = jnp.min(jnp.where(s == m, kio, float(N_KEYS)), axis=0, keepdims=True)
            v_scr[half, r:r + 1, :] = m
            i_scr[half, r:r + 1, :] = idx
            s = jnp.where(kio == idx, neg, s)
    c_scr[...] = jnp.full(c_scr.shape, neg, F32)
    ce_scr[...] = jnp.zeros(ce_scr.shape, F32)
    for c, (i, j) in enumerate(_PAIRS):
        c_scr[c:c + 1, :] = v_scr[0, i:i + 1, :] + v_scr[1, j:j + 1, :]
        ce_scr[c:c + 1, :] = i_scr[0, i:i + 1, :] * float(N_KEYS) + i_scr[1, j:j + 1, :]
    cand = c_scr[...]
    ce = ce_scr[...]
    pos = jnp.broadcast_to(pos_ref[:, 0:1], cand.shape)
    for r in range(P_TOPK):
        m = jnp.max(cand, axis=0, keepdims=True)
        sel = jnp.min(jnp.where(cand == m, pos, 1e9), axis=0, keepdims=True)
        hit = pos == sel
        b_scr[r:r + 1, :] = m
        x_scr[r:r + 1, :] = jnp.max(jnp.where(hit, ce, -1.0), axis=0, keepdims=True)
        cand = jnp.where(hit, neg, cand)
    best = b_scr[...]
    ex = jnp.exp(best - best[0:1, :])
    g_ref[...] = ex / jnp.sum(ex, axis=0, keepdims=True)
    e_ref[...] = x_scr[...].astype(jnp.int32)


def _peer_topk(s1, s2, pos, tt):
    t = s1.shape[2]
    return pl.pallas_call(
        _topk_body,
        grid=(t // tt, P_HEADS),
        in_specs=[
            pl.BlockSpec((1, N_KEYS, tt), lambda i, h: (h, 0, i)),
            pl.BlockSpec((1, N_KEYS, tt), lambda i, h: (h, 0, i)),
            pl.BlockSpec((_NCAND, HD), lambda i, h: (0, 0)),
        ],
        out_specs=[
            pl.BlockSpec((P_TOPK, tt), lambda i, h: (h, i)),
            pl.BlockSpec((P_TOPK, tt), lambda i, h: (h, i)),
        ],
        out_shape=[
            jax.ShapeDtypeStruct((P_HEADS * P_TOPK, t), jnp.int32),
            jax.ShapeDtypeStruct((P_HEADS * P_TOPK, t), F32),
        ],
        scratch_shapes=[
            pltpu.VMEM((2, P_TOPK, tt), F32),
            pltpu.VMEM((2, P_TOPK, tt), F32),
            pltpu.VMEM((_NCAND, tt), F32),
            pltpu.VMEM((_NCAND, tt), F32),
            pltpu.VMEM((P_TOPK, tt), F32),
            pltpu.VMEM((P_TOPK, tt), F32),
        ],
        compiler_params=pltpu.CompilerParams(
            dimension_semantics=("parallel", "parallel"),
            vmem_limit_bytes=VMEM_LIMIT),
        name="peer_topk",
    )(s1, s2, pos)


PEER_TB = 128
PEER_G = 8
PEER_SETS = 4
DMA_QUEUES = 2
NSLOT = P_HEADS * P_TOPK
SG = 8
CC_AHEAD = 3
ROW_PITCH = 17


HID_UNROLL = 1
MIX_UNROLL = 2


def _gelu(x):
    return 0.5 * x * (1.0 + lax.erf(x * (2.0 ** -0.5)))


def _unrolled_loop(n, unroll, body):
    def trip(q, carry):
        for j in range(unroll):
            body(q * unroll + j, 0)
        return carry
    lax.fori_loop(0, n // unroll, trip, 0)


def _pack_body(u_ref, v_ref, o_ref):
    ub = lax.bitcast_convert_type(u_ref[...].astype(BF16).astype(F32), jnp.uint32) >> 16
    vb = lax.bitcast_convert_type(v_ref[...].astype(BF16).astype(F32), jnp.uint32)
    w = (vb & jnp.uint32(0xFFFF0000)) | ub
    nlc = w.shape[1] // HD
    for c in range(nlc):
        o_ref[pl.ds(c, w.shape[0], stride=nlc), :] = w[:, c * HD:(c + 1) * HD]


def _peer_pack(u_tab, v_tab, te):
    e, d = u_tab.shape
    return pl.pallas_call(
        _pack_body,
        grid=(e // te,),
        in_specs=[pl.BlockSpec((te, d), lambda i: (i, 0)),
                  pl.BlockSpec((te, d), lambda i: (i, 0))],
        out_specs=pl.BlockSpec((te * (d // HD), HD), lambda i: (i, 0)),
        out_shape=jax.ShapeDtypeStruct((e * (d // HD), HD), jnp.uint32),
        compiler_params=pltpu.CompilerParams(
            dimension_semantics=("parallel",),
            vmem_limit_bytes=VMEM_LIMIT),
        name="peer_pack",
    )(u_tab, v_tab)


def _peer_body(idx_ref, gt_ref, hn_ref, h_ref, fg_ref, w_hbm, o_ref,
               hid_scr, c_scr, y_scr, sem, *bufs, final, nsteps):
    tb, d = hn_ref.shape
    ngroups = tb // PEER_G
    nsg = NSLOT // SG
    ahead = PEER_SETS - 1
    nlc = d // HD
    step = pl.program_id(0)

    def issue_tok(tok, set_, r, s0, s1):
        ids = idx_ref.at[tok]
        dst = bufs[set_].at[pl.ds(r * NSLOT * ROW_PITCH, NSLOT * ROW_PITCH)]
        for s in range(s0, s1):
            e0 = pl.multiple_of(ids[s] * nlc, nlc)
            pltpu.make_async_copy(w_hbm.at[pl.ds(e0, nlc)], dst.at[pl.ds(s * ROW_PITCH, nlc)],
                                  sem.at[set_]).start(priority=s % DMA_QUEUES)

    def wait_set(set_):
        full = bufs[set_].at[pl.ds(0, PEER_G * NSLOT * nlc)]
        pltpu.make_async_copy(full, full, sem.at[set_]).wait()

    @pl.when(step == 0)
    def _():
        def first(r, carry):
            for a in range(ahead):
                issue_tok(a * PEER_G + r, a, r, 0, NSLOT)
            return carry
        lax.fori_loop(0, PEER_G, first, 0)

    nsg = NSLOT // SG
    per_tile = NSLOT // 2 // nsg
    lane = lax.broadcasted_iota(jnp.int32, (SG, tb), 1)
    hid_scr[...] = jnp.zeros_like(hid_scr)

    def group_body(g, set_):
        nxt = (set_ + ahead) % PEER_SETS
        wait_set(set_)


        def hid_tok(r, c2):
            t = g * PEER_G + r
            xrow = hn_ref[pl.ds(t, 1), :]
            xb = [jnp.broadcast_to(xrow[:, lc * HD:(lc + 1) * HD], (SG, HD)) for lc in range(nlc)]
            tiles = bufs[set_].at[pl.ds(r * NSLOT * ROW_PITCH, NSLOT * ROW_PITCH)]
            hs = []
            for sg in range(nsg):
                issue_tok((g + ahead) * PEER_G + r, nxt, r, sg * per_tile, (sg + 1) * per_tile)
                acc = None
                for lc in range(nlc):
                    w = tiles[pl.ds(sg * SG * ROW_PITCH + lc, SG, stride=ROW_PITCH), :]
                    u = lax.bitcast_convert_type(w << 16, F32)
                    term = u * xb[lc]
                    acc = term if acc is None else acc + term
                hs.append(jnp.sum(acc, axis=1, keepdims=True))
            for sg in range(nsg):
                pltpu.store(hid_scr.at[pl.ds(sg * SG, SG), :],
                            jnp.broadcast_to(hs[sg], (SG, tb)), mask=lane == t)
            return c2
        _unrolled_loop(PEER_G, HID_UNROLL, hid_tok)

        c_scr[...] = gt_ref[...] * _gelu(hid_scr[...])

        def mix_tok(r, c2):
            t = g * PEER_G + r
            tiles = bufs[set_].at[pl.ds(r * NSLOT * ROW_PITCH, NSLOT * ROW_PITCH)]
            def coef_col(sg):
                return jnp.sum(jnp.where(lane == t, c_scr[pl.ds(sg * SG, SG), :], 0.0),
                               axis=1, keepdims=True)
            cc = {sg: coef_col(sg) for sg in range(CC_AHEAD)}
            accs = [None] * nlc
            for sg in range(nsg):
                issue_tok((g + ahead) * PEER_G + r, nxt, r,
                          NSLOT // 2 + sg * per_tile, NSLOT // 2 + (sg + 1) * per_tile)
                if sg + CC_AHEAD < nsg:
                    cc[sg + CC_AHEAD] = coef_col(sg + CC_AHEAD)
                for lc in range(nlc):
                    w = tiles[pl.ds(sg * SG * ROW_PITCH + lc, SG, stride=ROW_PITCH), :]
                    v = lax.bitcast_convert_type(w & jnp.uint32(0xFFFF0000), F32)
                    term = v * cc[sg]
                    accs[lc] = term if accs[lc] is None else accs[lc] + term
            yrow = jnp.concatenate(
                [jnp.sum(a, axis=0, keepdims=True) for a in accs], axis=1)
            y_scr[pl.ds(t, 1), :] = yrow
            return c2
        _unrolled_loop(PEER_G, MIX_UNROLL, mix_tok)

    def group_round(q, carry):
        for j in range(PEER_SETS):
            group_body(q * PEER_SETS + j, j)
        return carry

    lax.fori_loop(0, ngroups // PEER_SETS, group_round, 0)

    @pl.when(step == nsteps - 1)
    def _():
        for a in range(ahead):
            wait_set((nsteps * ngroups + a) % PEER_SETS)

    hh = h_ref[...] + y_scr[...]
    if final:
        hh = hh * lax.rsqrt(jnp.mean(hh * hh, axis=-1, keepdims=True) + EPS) * fg_ref[...]
    o_ref[...] = hh


def _peer_mix(idx, gt, hn, h, fg, w_tab, final):
    t, d = hn.shape
    tb = PEER_TB
    nsteps = t // tb
    extra = (PEER_SETS - 1) * PEER_G
    assert (tb // PEER_G) % PEER_SETS == 0
    idx_pad = jnp.concatenate([idx, jnp.zeros((tb, NSLOT), idx.dtype)], axis=0)
    idx_ext = jnp.concatenate(
        [idx.reshape(nsteps, tb, NSLOT),
         idx_pad[tb:].reshape(nsteps, tb, NSLOT)[:, :extra]], axis=1)
    return pl.pallas_call(
        functools.partial(_peer_body, final=final, nsteps=nsteps),
        grid=(nsteps,),
        in_specs=[
            pl.BlockSpec((None, tb + extra, NSLOT), lambda i: (i, 0, 0),
                         memory_space=pltpu.SMEM),
            pl.BlockSpec((NSLOT, tb), lambda i: (0, i)),
            pl.BlockSpec((tb, d), lambda i: (i, 0)),
            pl.BlockSpec((tb, d), lambda i: (i, 0)),
            pl.BlockSpec((1, d), lambda i: (0, 0)),
            pl.BlockSpec(memory_space=pl.ANY),
        ],
        out_specs=pl.BlockSpec((tb, d), lambda i: (i, 0)),
        out_shape=jax.ShapeDtypeStruct((t, d), F32),
        scratch_shapes=[
            pltpu.VMEM((NSLOT, tb), F32),
            pltpu.VMEM((NSLOT, tb), F32),
            pltpu.VMEM((tb, d), F32),
            pltpu.SemaphoreType.DMA((PEER_SETS,)),
        ] + [pltpu.VMEM((PEER_G * NSLOT * ROW_PITCH, HD), jnp.uint32) for _ in range(PEER_SETS)],
        compiler_params=pltpu.CompilerParams(
            dimension_semantics=("arbitrary",),
            vmem_limit_bytes=VMEM_LIMIT),
        name="peer_mix",
    )(idx_ext, gt, hn, h, fg, w_tab)


def _tile(n, pref):
    return pref if n % pref == 0 else n


def kernel(x, attn_norm_g, w_in, hgrn_lb_logits, hgrn_norm_g, gdn_conv_w, gdn_A_log, gdn_dt_bias,
           gdn_norm_g, w_out, ffn_norm_g, peer_w_query, peer_sub_keys, peer_u, peer_v,
           final_norm_g):
    b, s, d = x.shape
    t = b * s
    depth = w_in.shape[0]
    in_width = w_in.shape[2]
    n_small = 2 * B_V_HEADS
    n_main = in_width - n_small
    in_pad = n_main + HD
    lb_all = jnp.cumsum(jax.nn.softmax(hgrn_lb_logits.astype(F32), axis=0), axis=0)
    pos = jnp.array([i * P_TOPK + j for i, j in _PAIRS]
                    + [10 ** 6 + c for c in range(_NCAND - len(_PAIRS))], F32)
    pos = jnp.broadcast_to(pos[:, None], (_NCAND, HD))

    h2 = x.reshape(t, d)
    for l in range(depth):
        w_l = jnp.pad(w_in[l], ((0, 0), (0, in_pad - in_width))).astype(BF16)
        tn = in_pad // 3 if (in_pad // HD) % 3 == 0 else in_pad
        proj = _in_proj(h2, attn_norm_g[l][None, :], w_l, _tile(t, 512), tn)
        proj3 = proj.reshape(b, s, in_pad)

        o_a = _hgrn(proj3, lb_all[l], hgrn_norm_g[l][None, :], _tile(s, 256))

        ts = _tile(s, 256)
        small = proj3[:, :, n_main:n_main + n_small]
        smt = small.reshape(b, s // CHUNK, CHUNK, n_small).transpose(0, 1, 3, 2)
        zeros8 = jnp.zeros((B_V_HEADS,), F32)
        a_neg = jnp.exp(gdn_A_log[l].astype(F32))
        dtb = gdn_dt_bias[l].astype(F32)
        pcol = jnp.zeros((2, HD), F32)
        pcol = pcol.at[0, B_V_HEADS:n_small].set(a_neg).at[1, B_V_HEADS:n_small].set(dtb)
        prow = jnp.zeros((n_small, HD), F32)
        prow = prow.at[:, 0].set(jnp.concatenate([zeros8, a_neg]))
        prow = prow.at[:, 1].set(jnp.concatenate([zeros8, dtb]))
        o_b = _gdn(proj3, smt, gdn_conv_w[l], pcol, prow, gdn_norm_g[l][None, :], ts)

        wq = peer_w_query[l].reshape(d, P_HEADS * 2 * HD).astype(BF16)
        h2, hn, s1, s2 = _out_proj(
            h2, o_a.reshape(t, -1), o_b.reshape(t, -1), w_out[l].astype(BF16),
            ffn_norm_g[l][None, :], wq, peer_sub_keys[l, 0], peer_sub_keys[l, 1], _tile(t, 256))

        e_t, g_t = _peer_topk(s1, s2, pos, _tile(t, 256))
        w_tab = _peer_pack(peer_u[l], peer_v[l], _tile(peer_u.shape[1], 256))
        h2 = _peer_mix(e_t.T, g_t, hn, h2, final_norm_g[None, :], w_tab, l == depth - 1)
    return h2.reshape(b, s, d)
```

```python
import functools

import jax
import jax.numpy as jnp
from jax import lax
from jax.experimental import pallas as pl
from jax.experimental.pallas import tpu as pltpu

F32 = jnp.float32
BF16 = jnp.bfloat16

EPS = 1e-6
CHUNK = 64
SUB = 16
HD = 128
A_HEADS = 8
B_QK_HEADS = 4
B_V_HEADS = 8
CONV = 4
P_HEADS = 8
N_KEYS = 128
P_TOPK = 16
VMEM_LIMIT = 56 * 1024 * 1024


def _nt(a, b):
    return lax.dot_general(a, b, (((1,), (1,)), ((), ())), preferred_element_type=F32)


def _tn(a, b):
    return lax.dot_general(a, b, (((0,), (0,)), ((), ())), preferred_element_type=F32)


def _mm(a, b):
    return jnp.dot(a, b, preferred_element_type=F32)


def _split_bf16(x, n):
    parts = []
    for _ in range(n - 1):
        p = x.astype(BF16)
        parts.append(p)
        x = x - p.astype(F32)
    parts.append(x.astype(BF16))
    return parts


def _mm3(a, b):
    ah, al = _split_bf16(a, 2)
    bh, bl = _split_bf16(b, 2)
    return _mm(ah, bh) + (_mm(ah, bl) + _mm(al, bh))


def _cumsum_mm(tri, g, g_is_lhs):
    t16 = tri.astype(BF16)
    out = None
    for p in reversed(_split_bf16(g, 3)):
        term = _mm(p, t16) if g_is_lhs else _mm(t16, p)
        out = term if out is None else out + term
    return out


def _sigmoid(x):
    return 1.0 / (1.0 + jnp.exp(-x))


def _softplus(x):
    return jnp.maximum(x, 0.0) + jnp.log1p(jnp.exp(-jnp.abs(x)))


def _bcast_rows(x, idx):
    n = x.shape[1]
    return jnp.concatenate(
        [jnp.broadcast_to(x[r:r + 1, :], (SUB, n)) for r in idx], axis=0)


def _inproj_body(x_ref, g_ref, w_ref, o_ref):
    x = x_ref[...]
    ms = jnp.mean(x * x, axis=-1, keepdims=True)
    xn = (x * lax.rsqrt(ms + EPS) * g_ref[...]).astype(BF16)
    o_ref[...] = _mm(xn, w_ref[...])


def _in_proj(x2, g, w, tm, tn):
    t, d = x2.shape
    n = w.shape[1]
    return pl.pallas_call(
        _inproj_body,
        grid=(n // tn, t // tm),
        in_specs=[
            pl.BlockSpec((tm, d), lambda j, i: (i, 0)),
            pl.BlockSpec((1, d), lambda j, i: (0, 0)),
            pl.BlockSpec((d, tn), lambda j, i: (0, j)),
        ],
        out_specs=pl.BlockSpec((tm, tn), lambda j, i: (i, j)),
        out_shape=jax.ShapeDtypeStruct((t, n), F32),
        compiler_params=pltpu.CompilerParams(
            dimension_semantics=("parallel", "parallel"),
            vmem_limit_bytes=VMEM_LIMIT),
        name="in_proj",
    )(x2, g, w)


def _hgrn_body(q_ref, f_ref, i_ref, g_ref, lb_ref, gn_ref, o_ref, st_ref, *, ts):
    @pl.when(pl.program_id(1) == 0)
    def _():
        st_ref[...] = jnp.zeros_like(st_ref)

    row = lax.broadcasted_iota(jnp.int32, (CHUNK, CHUNK), 0)
    col = lax.broadcasted_iota(jnp.int32, (CHUNK, CHUNK), 1)
    tril_f = (col <= row).astype(F32)
    blk_r = row // SUB
    blk_c = col // SUB
    rblk = lax.broadcasted_iota(jnp.int32, (CHUNK, HD), 0) // SUB
    nsub = CHUNK // SUB
    gn = gn_ref[...]

    def chunk_body(c, carry):
        r0 = pl.multiple_of(c * CHUNK, CHUNK)
        rows = pl.ds(r0, CHUNK)
        hs = range(A_HEADS)
        cols = [slice(h * HD, (h + 1) * HD) for h in hs]
        q = [q_ref[0, rows, cols[h]] for h in hs]
        z = [f_ref[0, rows, cols[h]] for h in hs]
        v16 = [i_ref[0, rows, cols[h]].astype(BF16) for h in hs]
        lb = [lb_ref[h:h + 1, :] for h in hs]
        sts = [st_ref[h] for h in hs]
        k = [(1.0 - lb[h]) * _sigmoid(-z[h]) for h in hs]
        b = [_cumsum_mm(tril_f, jnp.log(lb[h] + (1.0 - lb[h]) * _sigmoid(z[h])), False)
             for h in hs]
        o = [_nt((q[h] * jnp.exp(b[h])).astype(BF16), sts[h].astype(BF16)) for h in hs]
        att = []
        for h in hs:
            bnd = [b[h][SUB * j + SUB - 1:SUB * j + SUB, :] for j in range(nsub)]
            k_off = k[h] * jnp.exp(
                _bcast_rows(b[h], [SUB * j + SUB - 1 for j in range(nsub)]) - b[h])
            qcat = jnp.concatenate(
                [q[h] * jnp.exp(jnp.minimum(b[h] - bnd[j], 0.0)) for j in range(nsub - 1)],
                axis=1)
            kcat = jnp.concatenate(
                [jnp.where(rblk == j, k_off, 0.0) for j in range(nsub - 1)], axis=1)
            att_off = _nt(qcat.astype(BF16), kcat.astype(BF16))
            ref_rows = _bcast_rows(b[h], [SUB * i for i in range(nsub)])
            q_d = q[h] * jnp.exp(b[h] - ref_rows)
            k_d = k[h] * jnp.exp(ref_rows - b[h])
            att_d = _nt(q_d.astype(BF16), k_d.astype(BF16))
            att.append(jnp.where(blk_r > blk_c, att_off,
                                 jnp.where((blk_r == blk_c) & (col <= row), att_d, 0.0)))
        o = [o[h] + _mm(att[h].astype(BF16), v16[h]) for h in hs]
        new_sts = []
        for h in hs:
            b_last = b[h][CHUNK - 1:CHUNK, :]
            kdec = k[h] * jnp.exp(b_last - b[h])
            new_sts.append(sts[h] * jnp.exp(b_last) + _tn(v16[h], kdec.astype(BF16)))
        for h in hs:
            st_ref[h] = new_sts[h]
        for h in hs:
            gate = g_ref[0, rows, cols[h]]
            y = o[h] * lax.rsqrt(jnp.mean(o[h] * o[h], axis=-1, keepdims=True) + EPS)
            o_ref[0, rows, cols[h]] = (y * gn * (gate * _sigmoid(gate))).astype(o_ref.dtype)
        return carry

    lax.fori_loop(0, ts // CHUNK, chunk_body, 0)


def _hgrn(proj3, lb, gn, ts):
    b, s, _ = proj3.shape
    w = A_HEADS * HD
    return pl.pallas_call(
        functools.partial(_hgrn_body, ts=ts),
        grid=(b, s // ts),
        in_specs=[
            pl.BlockSpec((1, ts, w), lambda bi, ti: (bi, ti, 0)),
            pl.BlockSpec((1, ts, w), lambda bi, ti: (bi, ti, 1)),
            pl.BlockSpec((1, ts, w), lambda bi, ti: (bi, ti, 2)),
            pl.BlockSpec((1, ts, w), lambda bi, ti: (bi, ti, 3)),
            pl.BlockSpec((A_HEADS, HD), lambda bi, ti: (0, 0)),
            pl.BlockSpec((1, HD), lambda bi, ti: (0, 0)),
        ],
        out_specs=pl.BlockSpec((1, ts, w), lambda bi, ti: (bi, ti, 0)),
        out_shape=jax.ShapeDtypeStruct((b, s, w), BF16),
        scratch_shapes=[pltpu.VMEM((A_HEADS, HD, HD), F32)],
        compiler_params=pltpu.CompilerParams(
            dimension_semantics=("parallel", "arbitrary"),
            vmem_limit_bytes=VMEM_LIMIT),
        name="hgrn",
    )(proj3, proj3, proj3, proj3, lb, gn)


def _gdn_body(bq_ref, bk_ref, bv_ref, bz_ref, sm_ref, smt_ref, cw_ref, pcol_ref, prow_ref,
              gn_ref, o_ref, s_ref, tail_ref, qn_ref, kn_ref, vv_ref,
              u_s, w_s, qg_s, kd_s, aqk_s, dl_s, *, ts):
    nqk = B_QK_HEADS * HD

    @pl.when(pl.program_id(1) == 0)
    def _():
        s_ref[...] = jnp.zeros_like(s_ref)
        tail_ref[...] = jnp.zeros_like(tail_ref)

    def conv_silu(x_ref, h, c0):
        cs = slice(h * HD, (h + 1) * HD)
        cc = slice(c0 + h * HD, c0 + (h + 1) * HD)
        x = x_ref[0, :, cs]
        xe = jnp.concatenate([tail_ref[:, cc], x], axis=0)
        w = cw_ref[:, cc]
        y = x * w[CONV - 1:CONV, :]
        for j in range(1, CONV):
            y = y + pltpu.roll(xe, j, axis=0)[8:, :] * w[CONV - 1 - j:CONV - j, :]
        tail_ref[:, cc] = x[ts - 8:, :]
        return y * _sigmoid(y)

    def l2n(x):
        return x * lax.rsqrt(jnp.sum(x * x, axis=-1, keepdims=True) + EPS)

    for h in range(B_QK_HEADS):
        cs = slice(h * HD, (h + 1) * HD)
        qn_ref[:, cs] = l2n(conv_silu(bq_ref, h, 0)) * (HD ** -0.5)
        kn_ref[:, cs] = l2n(conv_silu(bk_ref, h, nqk))
    for h in range(B_V_HEADS):
        vv_ref[:, h * HD:(h + 1) * HD] = conv_silu(bv_ref, h, 2 * nqk)

    row = lax.broadcasted_iota(jnp.int32, (CHUNK, CHUNK), 0)
    col = lax.broadcasted_iota(jnp.int32, (CHUNK, CHUNK), 1)
    incl = col <= row
    strict = col < row
    tril_f = incl.astype(F32)
    triu_f = (row <= col).astype(F32)
    same_blk = (row // SUB) == (col // SUB)
    eye = (row == col).astype(F32)
    gn = gn_ref[...]
    a_col = pcol_ref[0:1, :]
    dt_col = pcol_ref[1:2, :]
    a_row = prow_ref[:, 0:1]
    dt_row = prow_ref[:, 1:2]
    rep = B_V_HEADS // B_QK_HEADS

    def prep_chunk(c, carry):
        r0 = pl.multiple_of(c * CHUNK, CHUNK)
        rows = pl.ds(r0, CHUNK)
        sm = sm_ref[0, rows, :]
        beta_c = _sigmoid(sm)
        gc_c = _cumsum_mm(tril_f, -a_col * _softplus(sm + dt_col), False)
        smt = smt_ref[0, c]
        gc_r = _cumsum_mm(triu_f, -a_row * _softplus(smt + dt_row), True)
        hs = range(B_V_HEADS)
        qq = [qn_ref[rows, (h // rep) * HD:(h // rep + 1) * HD] for h in hs]
        kk = [kn_ref[rows, (h // rep) * HD:(h // rep + 1) * HD] for h in hs]
        kk16 = [k.astype(BF16) for k in kk]
        bcol = [beta_c[:, h:h + 1] for h in hs]
        gcol = [gc_c[:, B_V_HEADS + h:B_V_HEADS + h + 1] for h in hs]
        decay = [jnp.where(incl, jnp.exp(jnp.minimum(
            gcol[h] - gc_r[B_V_HEADS + h:B_V_HEADS + h + 1, :], 0.0)), 0.0) for h in hs]
        kb = [kk[h] * bcol[h] for h in hs]
        a = [jnp.where(strict, _nt(kb[h].astype(BF16), kk16[h]) * decay[h], 0.0) for h in hs]
        dg = [jnp.where(same_blk, a[h], 0.0) for h in hs]
        off = [a[h] - dg[h] for h in hs]
        tinv = [eye - dg[h] for h in hs]
        p = dg
        for _ in range(3):
            p = [_mm3(p[h], p[h]) for h in hs]
            tinv = [tinv[h] + _mm3(tinv[h], p[h]) for h in hs]
        pm = [_mm3(tinv[h], off[h]) for h in hs]
        pm2 = [_mm3(pm[h], pm[h]) for h in hs]
        egc = [jnp.exp(gcol[h]) for h in hs]
        x = [_mm3(tinv[h], jnp.concatenate(
            [vv_ref[rows, h * HD:(h + 1) * HD] * bcol[h], kb[h] * egc[h]], axis=1)) for h in hs]
        x = [x[h] + _mm3(pm2[h], x[h]) for h in hs]
        x = [x[h] - _mm3(pm[h], x[h]) for h in hs]
        aqk = [jnp.where(incl, _nt(qq[h].astype(BF16), kk16[h]) * decay[h], 0.0) for h in hs]
        for h in hs:
            cs = slice(h * HD, (h + 1) * HD)
            g_last = gcol[h][CHUNK - 1:CHUNK, :]
            u_s[rows, cs] = x[h][:, :HD]
            w_s[rows, cs] = x[h][:, HD:].astype(BF16)
            qg_s[rows, cs] = (qq[h] * egc[h]).astype(BF16)
            kd_s[rows, cs] = (kk[h] * jnp.exp(g_last - gcol[h])).astype(BF16)
            aqk_s[c, h] = aqk[h].astype(BF16)
            dl_s[c, h:h + 1, :] = jnp.broadcast_to(jnp.exp(g_last), (1, HD))
        return carry

    lax.fori_loop(0, ts // CHUNK, prep_chunk, 0)

    def scan_chunk(c, carry):
        r0 = pl.multiple_of(c * CHUNK, CHUNK)
        rows = pl.ds(r0, CHUNK)
        hs = range(B_V_HEADS)
        cols = [slice(h * HD, (h + 1) * HD) for h in hs]
        sts = [s_ref[h] for h in hs]
        st16 = [s.astype(BF16) for s in sts]
        v16 = [(u_s[rows, cols[h]] - _mm(w_s[rows, cols[h]], st16[h])).astype(BF16) for h in hs]
        o = [_mm(qg_s[rows, cols[h]], st16[h]) + _mm(aqk_s[c, h], v16[h]) for h in hs]
        new_sts = [sts[h] * dl_s[c, h:h + 1, :] + _tn(kd_s[rows, cols[h]], v16[h]) for h in hs]
        for h in hs:
            s_ref[h] = new_sts[h]
        for h in hs:
            zz = bz_ref[0, rows, cols[h]]
            y = o[h] * lax.rsqrt(jnp.mean(o[h] * o[h], axis=-1, keepdims=True) + EPS)
            o_ref[0, rows, cols[h]] = (y * gn * (zz * _sigmoid(zz))).astype(o_ref.dtype)
        return carry

    lax.fori_loop(0, ts // CHUNK, scan_chunk, 0)


def _gdn(proj3, smt, conv_w, pcol, prow, gn, ts):
    b, s, _ = proj3.shape
    nqk = B_QK_HEADS * HD
    wv = B_V_HEADS * HD
    off_q = (4 * A_HEADS * HD) // nqk
    off_v = (4 * A_HEADS * HD + 2 * nqk) // wv
    off_s = (4 * A_HEADS * HD + 2 * nqk + 2 * wv) // HD
    nch = ts // CHUNK
    return pl.pallas_call(
        functools.partial(_gdn_body, ts=ts),
        grid=(b, s // ts),
        in_specs=[
            pl.BlockSpec((1, ts, nqk), lambda bi, ti: (bi, ti, off_q)),
            pl.BlockSpec((1, ts, nqk), lambda bi, ti: (bi, ti, off_q + 1)),
            pl.BlockSpec((1, ts, wv), lambda bi, ti: (bi, ti, off_v)),
            pl.BlockSpec((1, ts, wv), lambda bi, ti: (bi, ti, off_v + 1)),
            pl.BlockSpec((1, ts, HD), lambda bi, ti: (bi, ti, off_s)),
            pl.BlockSpec((1, nch, 2 * B_V_HEADS, CHUNK), lambda bi, ti: (bi, ti, 0, 0)),
            pl.BlockSpec((CONV, 2 * nqk + wv), lambda bi, ti: (0, 0)),
            pl.BlockSpec((2, HD), lambda bi, ti: (0, 0)),
            pl.BlockSpec((2 * B_V_HEADS, HD), lambda bi, ti: (0, 0)),
            pl.BlockSpec((1, HD), lambda bi, ti: (0, 0)),
        ],
        out_specs=pl.BlockSpec((1, ts, wv), lambda bi, ti: (bi, ti, 0)),
        out_shape=jax.ShapeDtypeStruct((b, s, wv), BF16),
        scratch_shapes=[
            pltpu.VMEM((B_V_HEADS, HD, HD), F32),
            pltpu.VMEM((8, 2 * nqk + wv), F32),
            pltpu.VMEM((ts, nqk), F32),
            pltpu.VMEM((ts, nqk), F32),
            pltpu.VMEM((ts, wv), F32),
            pltpu.VMEM((ts, wv), F32),
            pltpu.VMEM((ts, wv), BF16),
            pltpu.VMEM((ts, wv), BF16),
            pltpu.VMEM((ts, wv), BF16),
            pltpu.VMEM((nch, B_V_HEADS, CHUNK, CHUNK), BF16),
            pltpu.VMEM((nch, B_V_HEADS, HD), F32),
        ],
        compiler_params=pltpu.CompilerParams(
            dimension_semantics=("parallel", "arbitrary"),
            vmem_limit_bytes=VMEM_LIMIT),
        name="gdn",
    )(proj3, proj3, proj3, proj3, proj3, smt, conv_w, pcol, prow, gn)


def _outproj_body(x_ref, oa_ref, ob_ref, wo_ref, fg_ref, wq_ref, k1_ref, k2_ref,
                  h_ref, hn_ref, s1_ref, s2_ref):
    wa = oa_ref.shape[1]
    h = x_ref[...] + _mm(oa_ref[...], wo_ref[:wa, :]) + _mm(ob_ref[...], wo_ref[wa:, :])
    h_ref[...] = h
    hn = h * lax.rsqrt(jnp.mean(h * h, axis=-1, keepdims=True) + EPS) * fg_ref[...]
    hn_ref[...] = hn
    qry = _mm(hn.astype(BF16), wq_ref[...])
    for hh in range(P_HEADS):
        q1 = qry[:, hh * 2 * HD:hh * 2 * HD + HD].astype(BF16)
        q2 = qry[:, hh * 2 * HD + HD:(hh + 1) * 2 * HD].astype(BF16)
        s1_ref[hh] = _nt(k1_ref[hh].astype(BF16), q1)
        s2_ref[hh] = _nt(k2_ref[hh].astype(BF16), q2)


def _out_proj(x2, oa, ob, wo, fg, wq, k1, k2, tm):
    t, d = x2.shape
    wa = oa.shape[1]
    wb = ob.shape[1]
    const = dict(pipeline_mode=pl.Buffered(1))
    return pl.pallas_call(
        _outproj_body,
        grid=(t // tm,),
        in_specs=[
            pl.BlockSpec((tm, d), lambda i: (i, 0)),
            pl.BlockSpec((tm, wa), lambda i: (i, 0)),
            pl.BlockSpec((tm, wb), lambda i: (i, 0)),
            pl.BlockSpec((wa + wb, d), lambda i: (0, 0), **const),
            pl.BlockSpec((1, d), lambda i: (0, 0)),
            pl.BlockSpec((d, P_HEADS * 2 * HD), lambda i: (0, 0), **const),
            pl.BlockSpec((P_HEADS, N_KEYS, HD), lambda i: (0, 0, 0)),
            pl.BlockSpec((P_HEADS, N_KEYS, HD), lambda i: (0, 0, 0)),
        ],
        out_specs=[
            pl.BlockSpec((tm, d), lambda i: (i, 0)),
            pl.BlockSpec((tm, d), lambda i: (i, 0)),
            pl.BlockSpec((P_HEADS, N_KEYS, tm), lambda i: (0, 0, i)),
            pl.BlockSpec((P_HEADS, N_KEYS, tm), lambda i: (0, 0, i)),
        ],
        out_shape=[
            jax.ShapeDtypeStruct((t, d), F32),
            jax.ShapeDtypeStruct((t, d), F32),
            jax.ShapeDtypeStruct((P_HEADS, N_KEYS, t), F32),
            jax.ShapeDtypeStruct((P_HEADS, N_KEYS, t), F32),
        ],
        compiler_params=pltpu.CompilerParams(
            dimension_semantics=("parallel",),
            vmem_limit_bytes=VMEM_LIMIT),
        name="out_proj",
    )(x2, oa, ob, wo, fg, wq, k1, k2)


_PAIRS = [(i, j) for i in range(P_TOPK) for j in range(P_TOPK) if (i + 1) * (j + 1) <= P_TOPK]
_NCAND = -(-len(_PAIRS) // 8) * 8


def _topk_body(s1_ref, s2_ref, pos_ref, e_ref, g_ref, v_scr, i_scr, c_scr, ce_scr, b_scr, x_scr):
    tt = s1_ref.shape[2]
    kio = lax.broadcasted_iota(jnp.int32, (N_KEYS, tt), 0).astype(F32)
    neg = -jnp.inf
    for half, sref in ((0, s1_ref), (1, s2_ref)):
        s = sref[0]
        for r in range(P_TOPK):
            m = jnp.max(s, axis=0, keepdims=True)
            idx = jnp.min(jnp.where(s == m, kio, float(N_KEYS)), axis=0, keepdims=True)
            v_scr[half, r:r + 1, :] = m
            i_scr[half, r:r + 1, :] = idx
            s = jnp.where(kio == idx, neg, s)
    c_scr[...] = jnp.full(c_scr.shape, neg, F32)
    ce_scr[...] = jnp.zeros(ce_scr.shape, F32)
    for c, (i, j) in enumerate(_PAIRS):
        c_scr[c:c + 1, :] = v_scr[0, i:i + 1, :] + v_scr[1, j:j + 1, :]
        ce_scr[c:c + 1, :] = i_scr[0, i:i + 1, :] * float(N_KEYS) + i_scr[1, j:j + 1, :]
    cand = c_scr[...]
    ce = ce_scr[...]
    pos = jnp.broadcast_to(pos_ref[:, 0:1], cand.shape)
    for r in range(P_TOPK):
        m = jnp.max(cand, axis=0, keepdims=True)
        sel = jnp.min(jnp.where(cand == m, pos, 1e9), axis=0, keepdims=True)
        hit = pos == sel
        b_scr[r:r + 1, :] = m
        x_scr[r:r + 1, :] = jnp.max(jnp.where(hit, ce, -1.0), axis=0, keepdims=True)
        cand = jnp.where(hit, neg, cand)
    best = b_scr[...]
    ex = jnp.exp(best - best[0:1, :])
    g_ref[...] = ex / jnp.sum(ex, axis=0, keepdims=True)
    e_ref[...] = x_scr[...].astype(jnp.int32)


def _peer_topk(s1, s2, pos, tt):
    t = s1.shape[2]
    return pl.pallas_call(
        _topk_body,
        grid=(t // tt, P_HEADS),
        in_specs=[
            pl.BlockSpec((1, N_KEYS, tt), lambda i, h: (h, 0, i)),
            pl.BlockSpec((1, N_KEYS, tt), lambda i, h: (h, 0, i)),
            pl.BlockSpec((_NCAND, HD), lambda i, h: (0, 0)),
        ],
        out_specs=[
            pl.BlockSpec((P_TOPK, tt), lambda i, h: (h, i)),
            pl.BlockSpec((P_TOPK, tt), lambda i, h: (h, i)),
        ],
        out_shape=[
            jax.ShapeDtypeStruct((P_HEADS * P_TOPK, t), jnp.int32),
            jax.ShapeDtypeStruct((P_HEADS * P_TOPK, t), F32),
        ],
        scratch_shapes=[
            pltpu.VMEM((2, P_TOPK, tt), F32),
            pltpu.VMEM((2, P_TOPK, tt), F32),
            pltpu.VMEM((_NCAND, tt), F32),
            pltpu.VMEM((_NCAND, tt), F32),
            pltpu.VMEM((P_TOPK, tt), F32),
            pltpu.VMEM((P_TOPK, tt), F32),
        ],
        compiler_params=pltpu.CompilerParams(
            dimension_semantics=("parallel", "parallel"),
            vmem_limit_bytes=VMEM_LIMIT),
        name="peer_topk",
    )(s1, s2, pos)


PEER_TB = 128
PEER_G = 8
PEER_SETS = 4
DMA_QUEUES = 2
NSLOT = P_HEADS * P_TOPK
SG = 8
CC_AHEAD = 3
ROW_PITCH = 17
HID_UNROLL = 1
MIX_UNROLL = 2


def _gelu(x):
    return 0.5 * x * (1.0 + lax.erf(x * (2.0 ** -0.5)))


def _unrolled_loop(n, unroll, body):
    def trip(q, carry):
        for j in range(unroll):
            body(q * unroll + j, 0)
        return carry
    lax.fori_loop(0, n // unroll, trip, 0)


def _pack_body(u_ref, v_ref, o_ref):
    ub = lax.bitcast_convert_type(u_ref[...].astype(BF16).astype(F32), jnp.uint32) >> 16
    vb = lax.bitcast_convert_type(v_ref[...].astype(BF16).astype(F32), jnp.uint32)
    w = (vb & jnp.uint32(0xFFFF0000)) | ub
    nlc = w.shape[1] // HD
    for c in range(nlc):
        o_ref[pl.ds(c, w.shape[0], stride=nlc), :] = w[:, c * HD:(c + 1) * HD]


def _peer_pack(u_tab, v_tab, te):
    e, d = u_tab.shape
    return pl.pallas_call(
        _pack_body,
        grid=(e // te,),
        in_specs=[pl.BlockSpec((te, d), lambda i: (i, 0)),
                  pl.BlockSpec((te, d), lambda i: (i, 0))],
        out_specs=pl.BlockSpec((te * (d // HD), HD), lambda i: (i, 0)),
        out_shape=jax.ShapeDtypeStruct((e * (d // HD), HD), jnp.uint32),
        compiler_params=pltpu.CompilerParams(
            dimension_semantics=("parallel",),
            vmem_limit_bytes=VMEM_LIMIT),
        name="peer_pack",
    )(u_tab, v_tab)


def _peer_body(idx_ref, gt_ref, hn_ref, h_ref, fg_ref, w_hbm, o_ref,
               hid_scr, c_scr, y_scr, sem, *bufs, final, nsteps):
    tb, d = hn_ref.shape
    ngroups = tb // PEER_G
    nsg = NSLOT // SG
    ahead = PEER_SETS - 1
    nlc = d // HD
    step = pl.program_id(0)

    def issue_tok(tok, set_, r, s0, s1):
        ids = idx_ref.at[tok]
        dst = bufs[set_].at[pl.ds(r * NSLOT * ROW_PITCH, NSLOT * ROW_PITCH)]
        for s in range(s0, s1):
            e0 = pl.multiple_of(ids[s] * nlc, nlc)
            pltpu.make_async_copy(w_hbm.at[pl.ds(e0, nlc)], dst.at[pl.ds(s * ROW_PITCH, nlc)],
                                  sem.at[set_]).start(priority=s % DMA_QUEUES)

    def wait_set(set_):
        full = bufs[set_].at[pl.ds(0, PEER_G * NSLOT * nlc)]
        pltpu.make_async_copy(full, full, sem.at[set_]).wait()

    @pl.when(step == 0)
    def _():
        def first(r, carry):
            for a in range(ahead):
                issue_tok(a * PEER_G + r, a, r, 0, NSLOT)
            return carry
        lax.fori_loop(0, PEER_G, first, 0)

    per_tile = NSLOT // 2 // nsg
    lane = lax.broadcasted_iota(jnp.int32, (SG, tb), 1)
    hid_scr[...] = jnp.zeros_like(hid_scr)

    def group_body(g, set_):
        nxt = (set_ + ahead) % PEER_SETS
        wait_set(set_)


        def hid_tok(r, c2):
            t = g * PEER_G + r
            xrow = hn_ref[pl.ds(t, 1), :]
            xb = [jnp.broadcast_to(xrow[:, lc * HD:(lc + 1) * HD], (SG, HD)) for lc in range(nlc)]
            tiles = bufs[set_].at[pl.ds(r * NSLOT * ROW_PITCH, NSLOT * ROW_PITCH)]
            hs = []
            for sg in range(nsg):
                issue_tok((g + ahead) * PEER_G + r, nxt, r, sg * per_tile, (sg + 1) * per_tile)
                acc = None
                for lc in range(nlc):
                    w = tiles[pl.ds(sg * SG * ROW_PITCH + lc, SG, stride=ROW_PITCH), :]
                    u = lax.bitcast_convert_type(w << 16, F32)
                    term = u * xb[lc]
                    acc = term if acc is None else acc + term
                hs.append(jnp.sum(acc, axis=1, keepdims=True))
            for sg in range(nsg):
                pltpu.store(hid_scr.at[pl.ds(sg * SG, SG), :],
                            jnp.broadcast_to(hs[sg], (SG, tb)), mask=lane == t)
            return c2
        _unrolled_loop(PEER_G, HID_UNROLL, hid_tok)

        c_scr[...] = gt_ref[...] * _gelu(hid_scr[...])

        def mix_tok(r, c2):
            t = g * PEER_G + r
            tiles = bufs[set_].at[pl.ds(r * NSLOT * ROW_PITCH, NSLOT * ROW_PITCH)]
            def coef_col(sg):
                return jnp.sum(jnp.where(lane == t, c_scr[pl.ds(sg * SG, SG), :], 0.0),
                               axis=1, keepdims=True)
            cc = {sg: coef_col(sg) for sg in range(CC_AHEAD)}
            accs = [None] * nlc
            for sg in range(nsg):
                issue_tok((g + ahead) * PEER_G + r, nxt, r,
                          NSLOT // 2 + sg * per_tile, NSLOT // 2 + (sg + 1) * per_tile)
                if sg + CC_AHEAD < nsg:
                    cc[sg + CC_AHEAD] = coef_col(sg + CC_AHEAD)
                for lc in range(nlc):
                    w = tiles[pl.ds(sg * SG * ROW_PITCH + lc, SG, stride=ROW_PITCH), :]
                    v = lax.bitcast_convert_type(w & jnp.uint32(0xFFFF0000), F32)
                    term = v * cc[sg]
                    accs[lc] = term if accs[lc] is None else accs[lc] + term
            yrow = jnp.concatenate(
                [jnp.sum(a, axis=0, keepdims=True) for a in accs], axis=1)
            y_scr[pl.ds(t, 1), :] = yrow
            return c2
        _unrolled_loop(PEER_G, MIX_UNROLL, mix_tok)

    def group_round(q, carry):
        for j in range(PEER_SETS):
            group_body(q * PEER_SETS + j, j)
        return carry

    lax.fori_loop(0, ngroups // PEER_SETS, group_round, 0)

    @pl.when(step == nsteps - 1)
    def _():
        for a in range(ahead):
            wait_set((nsteps * ngroups + a) % PEER_SETS)

    hh = h_ref[...] + y_scr[...]
    if final:
        hh = hh * lax.rsqrt(jnp.mean(hh * hh, axis=-1, keepdims=True) + EPS) * fg_ref[...]
    o_ref[...] = hh


def _peer_mix(idx, gt, hn, h, fg, w_tab, final):
    t, d = hn.shape
    tb = PEER_TB
    nsteps = t // tb
    extra = (PEER_SETS - 1) * PEER_G
    assert (tb // PEER_G) % PEER_SETS == 0
    idx_pad = jnp.concatenate([idx, jnp.zeros((tb, NSLOT), idx.dtype)], axis=0)
    idx_ext = jnp.concatenate(
        [idx.reshape(nsteps, tb, NSLOT),
         idx_pad[tb:].reshape(nsteps, tb, NSLOT)[:, :extra]], axis=1)
    return pl.pallas_call(
        functools.partial(_peer_body, final=final, nsteps=nsteps),
        grid=(nsteps,),
        in_specs=[
            pl.BlockSpec((None, tb + extra, NSLOT), lambda i: (i, 0, 0),
                         memory_space=pltpu.SMEM),
            pl.BlockSpec((NSLOT, tb), lambda i: (0, i)),
            pl.BlockSpec((tb, d), lambda i: (i, 0)),
            pl.BlockSpec((tb, d), lambda i: (i, 0)),
            pl.BlockSpec((1, d), lambda i: (0, 0)),
            pl.BlockSpec(memory_space=pl.ANY),
        ],
        out_specs=pl.BlockSpec((tb, d), lambda i: (i, 0)),
        out_shape=jax.ShapeDtypeStruct((t, d), F32),
        scratch_shapes=[
            pltpu.VMEM((NSLOT, tb), F32),
            pltpu.VMEM((NSLOT, tb), F32),
            pltpu.VMEM((tb, d), F32),
            pltpu.SemaphoreType.DMA((PEER_SETS,)),
        ] + [pltpu.VMEM((PEER_G * NSLOT * ROW_PITCH, HD), jnp.uint32) for _ in range(PEER_SETS)],
        compiler_params=pltpu.CompilerParams(
            dimension_semantics=("arbitrary",),
            vmem_limit_bytes=VMEM_LIMIT),
        name="peer_mix",
    )(idx_ext, gt, hn, h, fg, w_tab)


def _tile(n, pref):
    return pref if n % pref == 0 else n


def kernel(x, attn_norm_g, w_in, hgrn_lb_logits, hgrn_norm_g, gdn_conv_w, gdn_A_log, gdn_dt_bias,
           gdn_norm_g, w_out, ffn_norm_g, peer_w_query, peer_sub_keys, peer_u, peer_v,
           final_norm_g):
    b, s, d = x.shape
    t = b * s
    depth = w_in.shape[0]
    in_width = w_in.shape[2]
    n_small = 2 * B_V_HEADS
    n_main = in_width - n_small
    in_pad = n_main + HD
    lb_all = jnp.cumsum(jax.nn.softmax(hgrn_lb_logits.astype(F32), axis=0), axis=0)
    pos = jnp.array([i * P_TOPK + j for i, j in _PAIRS]
                    + [10 ** 6 + c for c in range(_NCAND - len(_PAIRS))], F32)
    pos = jnp.broadcast_to(pos[:, None], (_NCAND, HD))

    h2 = x.reshape(t, d)
    for l in range(depth):
        w_l = jnp.pad(w_in[l], ((0, 0), (0, in_pad - in_width))).astype(BF16)
        tn = in_pad // 3 if (in_pad // HD) % 3 == 0 else in_pad
        proj = _in_proj(h2, attn_norm_g[l][None, :], w_l, _tile(t, 512), tn)
        proj3 = proj.reshape(b, s, in_pad)

        o_a = _hgrn(proj3, lb_all[l], hgrn_norm_g[l][None, :], _tile(s, 256))

        ts = _tile(s, 256)
        small = proj3[:, :, n_main:n_main + n_small]
        smt = small.reshape(b, s // CHUNK, CHUNK, n_small).transpose(0, 1, 3, 2)
        zeros8 = jnp.zeros((B_V_HEADS,), F32)
        a_neg = jnp.exp(gdn_A_log[l].astype(F32))
        dtb = gdn_dt_bias[l].astype(F32)
        pcol = jnp.zeros((2, HD), F32)
        pcol = pcol.at[0, B_V_HEADS:n_small].set(a_neg).at[1, B_V_HEADS:n_small].set(dtb)
        prow = jnp.zeros((n_small, HD), F32)
        prow = prow.at[:, 0].set(jnp.concatenate([zeros8, a_neg]))
        prow = prow.at[:, 1].set(jnp.concatenate([zeros8, dtb]))
        o_b = _gdn(proj3, smt, gdn_conv_w[l], pcol, prow, gdn_norm_g[l][None, :], ts)

        wq = peer_w_query[l].reshape(d, P_HEADS * 2 * HD).astype(BF16)
        h2, hn, s1, s2 = _out_proj(
            h2, o_a.reshape(t, -1), o_b.reshape(t, -1), w_out[l].astype(BF16),
            ffn_norm_g[l][None, :], wq, peer_sub_keys[l, 0], peer_sub_keys[l, 1], _tile(t, 256))

        e_t, g_t = _peer_topk(s1, s2, pos, _tile(t, 512))
        w_tab = _peer_pack(peer_u[l], peer_v[l], _tile(peer_u.shape[1], 256))
        h2 = _peer_mix(e_t.T, g_t, hn, h2, final_norm_g[None, :], w_tab, l == depth - 1)
    return h2.reshape(b, s, d)
```

```python
import functools

import jax
import jax.numpy as jnp
from jax import lax
from jax.experimental import pallas as pl
from jax.experimental.pallas import tpu as pltpu

F32 = jnp.float32
BF16 = jnp.bfloat16

EPS = 1e-6
CHUNK = 64
SUB = 16
HD = 128
A_HEADS = 8
B_QK_HEADS = 4
B_V_HEADS = 8
CONV = 4
P_HEADS = 8
N_KEYS = 128
P_TOPK = 16
VMEM_LIMIT = 56 * 1024 * 1024


def _nt(a, b):
    return lax.dot_general(a, b, (((1,), (1,)), ((), ())), preferred_element_type=F32)


def _tn(a, b):
    return lax.dot_general(a, b, (((0,), (0,)), ((), ())), preferred_element_type=F32)


def _mm(a, b):
    return jnp.dot(a, b, preferred_element_type=F32)


def _split_bf16(x, n):
    parts = []
    for _ in range(n - 1):
        p = x.astype(BF16)
        parts.append(p)
        x = x - p.astype(F32)
    parts.append(x.astype(BF16))
    return parts


def _mm3(a, b):
    ah, al = _split_bf16(a, 2)
    bh, bl = _split_bf16(b, 2)
    return _mm(ah, bh) + (_mm(ah, bl) + _mm(al, bh))


def _cumsum_mm(tri, g, g_is_lhs):
    t16 = tri.astype(BF16)
    out = None
    for p in reversed(_split_bf16(g, 3)):
        term = _mm(p, t16) if g_is_lhs else _mm(t16, p)
        out = term if out is None else out + term
    return out


def _sigmoid(x):
    return 1.0 / (1.0 + jnp.exp(-x))


def _softplus(x):
    return jnp.maximum(x, 0.0) + jnp.log1p(jnp.exp(-jnp.abs(x)))


def _bcast_rows(x, idx):
    n = x.shape[1]
    return jnp.concatenate(
        [jnp.broadcast_to(x[r:r + 1, :], (SUB, n)) for r in idx], axis=0)


def _inproj_body(x_ref, g_ref, w_ref, o_ref):
    x = x_ref[...]
    ms = jnp.mean(x * x, axis=-1, keepdims=True)
    xn = (x * lax.rsqrt(ms + EPS) * g_ref[...]).astype(BF16)
    o_ref[...] = _mm(xn, w_ref[...])


def _in_proj(x2, g, w, tm, tn):
    t, d = x2.shape
    n = w.shape[1]
    return pl.pallas_call(
        _inproj_body,
        grid=(n // tn, t // tm),
        in_specs=[
            pl.BlockSpec((tm, d), lambda j, i: (i, 0)),
            pl.BlockSpec((1, d), lambda j, i: (0, 0)),
            pl.BlockSpec((d, tn), lambda j, i: (0, j)),
        ],
        out_specs=pl.BlockSpec((tm, tn), lambda j, i: (i, j)),
        out_shape=jax.ShapeDtypeStruct((t, n), F32),
        compiler_params=pltpu.CompilerParams(
            dimension_semantics=("parallel", "parallel"),
            vmem_limit_bytes=VMEM_LIMIT),
        name="in_proj",
    )(x2, g, w)


def _hgrn_body(q_ref, f_ref, i_ref, g_ref, lb_ref, gn_ref, o_ref, st_ref, *, ts):
    @pl.when(pl.program_id(1) == 0)
    def _():
        st_ref[...] = jnp.zeros_like(st_ref)

    row = lax.broadcasted_iota(jnp.int32, (CHUNK, CHUNK), 0)
    col = lax.broadcasted_iota(jnp.int32, (CHUNK, CHUNK), 1)
    tril_f = (col <= row).astype(F32)
    blk_r = row // SUB
    blk_c = col // SUB
    rblk = lax.broadcasted_iota(jnp.int32, (CHUNK, HD), 0) // SUB
    nsub = CHUNK // SUB
    gn = gn_ref[...]

    def chunk_body(c, carry):
        r0 = pl.multiple_of(c * CHUNK, CHUNK)
        rows = pl.ds(r0, CHUNK)
        hs = range(A_HEADS)
        cols = [slice(h * HD, (h + 1) * HD) for h in hs]
        q = [q_ref[0, rows, cols[h]] for h in hs]
        z = [f_ref[0, rows, cols[h]] for h in hs]
        v16 = [i_ref[0, rows, cols[h]].astype(BF16) for h in hs]
        lb = [lb_ref[h:h + 1, :] for h in hs]
        sts = [st_ref[h] for h in hs]
        k = [(1.0 - lb[h]) * _sigmoid(-z[h]) for h in hs]
        b = [_cumsum_mm(tril_f, jnp.log(lb[h] + (1.0 - lb[h]) * _sigmoid(z[h])), False)
             for h in hs]
        o = [_nt((q[h] * jnp.exp(b[h])).astype(BF16), sts[h].astype(BF16)) for h in hs]
        att = []
        for h in hs:
            bnd = [b[h][SUB * j + SUB - 1:SUB * j + SUB, :] for j in range(nsub)]
            k_off = k[h] * jnp.exp(
                _bcast_rows(b[h], [SUB * j + SUB - 1 for j in range(nsub)]) - b[h])
            qcat = jnp.concatenate(
                [q[h] * jnp.exp(jnp.minimum(b[h] - bnd[j], 0.0)) for j in range(nsub - 1)],
                axis=1)
            kcat = jnp.concatenate(
                [jnp.where(rblk == j, k_off, 0.0) for j in range(nsub - 1)], axis=1)
            att_off = _nt(qcat.astype(BF16), kcat.astype(BF16))
            ref_rows = _bcast_rows(b[h], [SUB * i for i in range(nsub)])
            q_d = q[h] * jnp.exp(b[h] - ref_rows)
            k_d = k[h] * jnp.exp(ref_rows - b[h])
            att_d = _nt(q_d.astype(BF16), k_d.astype(BF16))
            att.append(jnp.where(blk_r > blk_c, att_off,
                                 jnp.where((blk_r == blk_c) & (col <= row), att_d, 0.0)))
        o = [o[h] + _mm(att[h].astype(BF16), v16[h]) for h in hs]
        new_sts = []
        for h in hs:
            b_last = b[h][CHUNK - 1:CHUNK, :]
            kdec = k[h] * jnp.exp(b_last - b[h])
            new_sts.append(sts[h] * jnp.exp(b_last) + _tn(v16[h], kdec.astype(BF16)))
        for h in hs:
            st_ref[h] = new_sts[h]
        for h in hs:
            gate = g_ref[0, rows, cols[h]]
            y = o[h] * lax.rsqrt(jnp.mean(o[h] * o[h], axis=-1, keepdims=True) + EPS)
            o_ref[0, rows, cols[h]] = (y * gn * (gate * _sigmoid(gate))).astype(o_ref.dtype)
        return carry

    lax.fori_loop(0, ts // CHUNK, chunk_body, 0)


def _hgrn(proj3, lb, gn, ts):
    b, s, _ = proj3.shape
    w = A_HEADS * HD
    return pl.pallas_call(
        functools.partial(_hgrn_body, ts=ts),
        grid=(b, s // ts),
        in_specs=[
            pl.BlockSpec((1, ts, w), lambda bi, ti: (bi, ti, 0)),
            pl.BlockSpec((1, ts, w), lambda bi, ti: (bi, ti, 1)),
            pl.BlockSpec((1, ts, w), lambda bi, ti: (bi, ti, 2)),
            pl.BlockSpec((1, ts, w), lambda bi, ti: (bi, ti, 3)),
            pl.BlockSpec((A_HEADS, HD), lambda bi, ti: (0, 0)),
            pl.BlockSpec((1, HD), lambda bi, ti: (0, 0)),
        ],
        out_specs=pl.BlockSpec((1, ts, w), lambda bi, ti: (bi, ti, 0)),
        out_shape=jax.ShapeDtypeStruct((b, s, w), BF16),
        scratch_shapes=[pltpu.VMEM((A_HEADS, HD, HD), F32)],
        compiler_params=pltpu.CompilerParams(
            dimension_semantics=("parallel", "arbitrary"),
            vmem_limit_bytes=VMEM_LIMIT),
        name="hgrn",
    )(proj3, proj3, proj3, proj3, lb, gn)


def _gdn_body(bq_ref, bk_ref, bv_ref, bz_ref, sm_ref, smt_ref, cw_ref, pcol_ref, prow_ref,
              gn_ref, o_ref, s_ref, tail_ref, qn_ref, kn_ref, vv_ref,
              u_s, w_s, qg_s, kd_s, aqk_s, dl_s, *, ts):
    nqk = B_QK_HEADS * HD

    @pl.when(pl.program_id(1) == 0)
    def _():
        s_ref[...] = jnp.zeros_like(s_ref)
        tail_ref[...] = jnp.zeros_like(tail_ref)

    def conv_silu(x_ref, h, c0):
        cs = slice(h * HD, (h + 1) * HD)
        cc = slice(c0 + h * HD, c0 + (h + 1) * HD)
        x = x_ref[0, :, cs]
        xe = jnp.concatenate([tail_ref[:, cc], x], axis=0)
        w = cw_ref[:, cc]
        y = x * w[CONV - 1:CONV, :]
        for j in range(1, CONV):
            y = y + pltpu.roll(xe, j, axis=0)[8:, :] * w[CONV - 1 - j:CONV - j, :]
        tail_ref[:, cc] = x[ts - 8:, :]
        return y * _sigmoid(y)

    def l2n(x):
        return x * lax.rsqrt(jnp.sum(x * x, axis=-1, keepdims=True) + EPS)

    for h in range(B_QK_HEADS):
        cs = slice(h * HD, (h + 1) * HD)
        qn_ref[:, cs] = l2n(conv_silu(bq_ref, h, 0)) * (HD ** -0.5)
        kn_ref[:, cs] = l2n(conv_silu(bk_ref, h, nqk))
    for h in range(B_V_HEADS):
        vv_ref[:, h * HD:(h + 1) * HD] = conv_silu(bv_ref, h, 2 * nqk)

    row = lax.broadcasted_iota(jnp.int32, (CHUNK, CHUNK), 0)
    col = lax.broadcasted_iota(jnp.int32, (CHUNK, CHUNK), 1)
    incl = col <= row
    strict = col < row
    tril_f = incl.astype(F32)
    triu_f = (row <= col).astype(F32)
    same_blk = (row // SUB) == (col // SUB)
    eye = (row == col).astype(F32)
    gn = gn_ref[...]
    a_col = pcol_ref[0:1, :]
    dt_col = pcol_ref[1:2, :]
    a_row = prow_ref[:, 0:1]
    dt_row = prow_ref[:, 1:2]
    rep = B_V_HEADS // B_QK_HEADS

    def prep_chunk(c, carry):
        r0 = pl.multiple_of(c * CHUNK, CHUNK)
        rows = pl.ds(r0, CHUNK)
        sm = sm_ref[0, rows, :]
        beta_c = _sigmoid(sm)
        gc_c = _cumsum_mm(tril_f, -a_col * _softplus(sm + dt_col), False)
        smt = smt_ref[0, c]
        gc_r = _cumsum_mm(triu_f, -a_row * _softplus(smt + dt_row), True)
        hs = range(B_V_HEADS)
        qq = [qn_ref[rows, (h // rep) * HD:(h // rep + 1) * HD] for h in hs]
        kk = [kn_ref[rows, (h // rep) * HD:(h // rep + 1) * HD] for h in hs]
        kk16 = [k.astype(BF16) for k in kk]
        bcol = [beta_c[:, h:h + 1] for h in hs]
        gcol = [gc_c[:, B_V_HEADS + h:B_V_HEADS + h + 1] for h in hs]
        decay = [jnp.where(incl, jnp.exp(jnp.minimum(
            gcol[h] - gc_r[B_V_HEADS + h:B_V_HEADS + h + 1, :], 0.0)), 0.0) for h in hs]
        kb = [kk[h] * bcol[h] for h in hs]
        a = [jnp.where(strict, _nt(kb[h].astype(BF16), kk16[h]) * decay[h], 0.0) for h in hs]
        dg = [jnp.where(same_blk, a[h], 0.0) for h in hs]
        off = [a[h] - dg[h] for h in hs]
        tinv = [eye - dg[h] for h in hs]
        p = dg
        for _ in range(3):
            p = [_mm3(p[h], p[h]) for h in hs]
            tinv = [tinv[h] + _mm3(tinv[h], p[h]) for h in hs]
        pm = [_mm3(tinv[h], off[h]) for h in hs]
        pm2 = [_mm3(pm[h], pm[h]) for h in hs]
        egc = [jnp.exp(gcol[h]) for h in hs]
        x = [_mm3(tinv[h], jnp.concatenate(
            [vv_ref[rows, h * HD:(h + 1) * HD] * bcol[h], kb[h] * egc[h]], axis=1)) for h in hs]
        x = [x[h] + _mm3(pm2[h], x[h]) for h in hs]
        x = [x[h] - _mm3(pm[h], x[h]) for h in hs]
        aqk = [jnp.where(incl, _nt(qq[h].astype(BF16), kk16[h]) * decay[h], 0.0) for h in hs]
        for h in hs:
            cs = slice(h * HD, (h + 1) * HD)
            g_last = gcol[h][CHUNK - 1:CHUNK, :]
            u_s[rows, cs] = x[h][:, :HD]
            w_s[rows, cs] = x[h][:, HD:].astype(BF16)
            qg_s[rows, cs] = (qq[h] * egc[h]).astype(BF16)
            kd_s[rows, cs] = (kk[h] * jnp.exp(g_last - gcol[h])).astype(BF16)
            aqk_s[c, h] = aqk[h].astype(BF16)
            dl_s[c, h:h + 1, :] = jnp.broadcast_to(jnp.exp(g_last), (1, HD))
        return carry

    lax.fori_loop(0, ts // CHUNK, prep_chunk, 0)

    def scan_chunk(c, carry):
        r0 = pl.multiple_of(c * CHUNK, CHUNK)
        rows = pl.ds(r0, CHUNK)
        hs = range(B_V_HEADS)
        cols = [slice(h * HD, (h + 1) * HD) for h in hs]
        sts = [s_ref[h] for h in hs]
        st16 = [s.astype(BF16) for s in sts]
        v16 = [(u_s[rows, cols[h]] - _mm(w_s[rows, cols[h]], st16[h])).astype(BF16) for h in hs]
        o = [_mm(qg_s[rows, cols[h]], st16[h]) + _mm(aqk_s[c, h], v16[h]) for h in hs]
        new_sts = [sts[h] * dl_s[c, h:h + 1, :] + _tn(kd_s[rows, cols[h]], v16[h]) for h in hs]
        for h in hs:
            s_ref[h] = new_sts[h]
        for h in hs:
            zz = bz_ref[0, rows, cols[h]]
            y = o[h] * lax.rsqrt(jnp.mean(o[h] * o[h], axis=-1, keepdims=True) + EPS)
            o_ref[0, rows, cols[h]] = (y * gn * (zz * _sigmoid(zz))).astype(o_ref.dtype)
        return carry

    lax.fori_loop(0, ts // CHUNK, scan_chunk, 0)


def _gdn(proj3, smt, conv_w, pcol, prow, gn, ts):
    b, s, _ = proj3.shape
    nqk = B_QK_HEADS * HD
    wv = B_V_HEADS * HD
    off_q = (4 * A_HEADS * HD) // nqk
    off_v = (4 * A_HEADS * HD + 2 * nqk) // wv
    off_s = (4 * A_HEADS * HD + 2 * nqk + 2 * wv) // HD
    nch = ts // CHUNK
    return pl.pallas_call(
        functools.partial(_gdn_body, ts=ts),
        grid=(b, s // ts),
        in_specs=[
            pl.BlockSpec((1, ts, nqk), lambda bi, ti: (bi, ti, off_q)),
            pl.BlockSpec((1, ts, nqk), lambda bi, ti: (bi, ti, off_q + 1)),
            pl.BlockSpec((1, ts, wv), lambda bi, ti: (bi, ti, off_v)),
            pl.BlockSpec((1, ts, wv), lambda bi, ti: (bi, ti, off_v + 1)),
            pl.BlockSpec((1, ts, HD), lambda bi, ti: (bi, ti, off_s)),
            pl.BlockSpec((1, nch, 2 * B_V_HEADS, CHUNK), lambda bi, ti: (bi, ti, 0, 0)),
            pl.BlockSpec((CONV, 2 * nqk + wv), lambda bi, ti: (0, 0)),
            pl.BlockSpec((2, HD), lambda bi, ti: (0, 0)),
            pl.BlockSpec((2 * B_V_HEADS, HD), lambda bi, ti: (0, 0)),
            pl.BlockSpec((1, HD), lambda bi, ti: (0, 0)),
        ],
        out_specs=pl.BlockSpec((1, ts, wv), lambda bi, ti: (bi, ti, 0)),
        out_shape=jax.ShapeDtypeStruct((b, s, wv), BF16),
        scratch_shapes=[
            pltpu.VMEM((B_V_HEADS, HD, HD), F32),
            pltpu.VMEM((8, 2 * nqk + wv), F32),
            pltpu.VMEM((ts, nqk), F32),
            pltpu.VMEM((ts, nqk), F32),
            pltpu.VMEM((ts, wv), F32),
            pltpu.VMEM((ts, wv), F32),
            pltpu.VMEM((ts, wv), BF16),
            pltpu.VMEM((ts, wv), BF16),
            pltpu.VMEM((ts, wv), BF16),
            pltpu.VMEM((nch, B_V_HEADS, CHUNK, CHUNK), BF16),
            pltpu.VMEM((nch, B_V_HEADS, HD), F32),
        ],
        compiler_params=pltpu.CompilerParams(
            dimension_semantics=("parallel", "arbitrary"),
            vmem_limit_bytes=VMEM_LIMIT),
        name="gdn",
    )(proj3, proj3, proj3, proj3, proj3, smt, conv_w, pcol, prow, gn)


def _outproj_body(x_ref, oa_ref, ob_ref, wo_ref, fg_ref, wq_ref, k1_ref, k2_ref,
                  h_ref, hn_ref, s1_ref, s2_ref):
    wa = oa_ref.shape[1]
    h = x_ref[...] + _mm(oa_ref[...], wo_ref[:wa, :]) + _mm(ob_ref[...], wo_ref[wa:, :])
    h_ref[...] = h
    hn = h * lax.rsqrt(jnp.mean(h * h, axis=-1, keepdims=True) + EPS) * fg_ref[...]
    hn_ref[...] = hn
    qry = _mm(hn.astype(BF16), wq_ref[...])
    for hh in range(P_HEADS):
        q1 = qry[:, hh * 2 * HD:hh * 2 * HD + HD].astype(BF16)
        q2 = qry[:, hh * 2 * HD + HD:(hh + 1) * 2 * HD].astype(BF16)
        s1_ref[hh] = _nt(k1_ref[hh].astype(BF16), q1)
        s2_ref[hh] = _nt(k2_ref[hh].astype(BF16), q2)


def _out_proj(x2, oa, ob, wo, fg, wq, k1, k2, tm):
    t, d = x2.shape
    wa = oa.shape[1]
    wb = ob.shape[1]
    const = dict(pipeline_mode=pl.Buffered(1))
    return pl.pallas_call(
        _outproj_body,
        grid=(t // tm,),
        in_specs=[
            pl.BlockSpec((tm, d), lambda i: (i, 0)),
            pl.BlockSpec((tm, wa), lambda i: (i, 0)),
            pl.BlockSpec((tm, wb), lambda i: (i, 0)),
            pl.BlockSpec((wa + wb, d), lambda i: (0, 0), **const),
            pl.BlockSpec((1, d), lambda i: (0, 0)),
            pl.BlockSpec((d, P_HEADS * 2 * HD), lambda i: (0, 0), **const),
            pl.BlockSpec((P_HEADS, N_KEYS, HD), lambda i: (0, 0, 0)),
            pl.BlockSpec((P_HEADS, N_KEYS, HD), lambda i: (0, 0, 0)),
        ],
        out_specs=[
            pl.BlockSpec((tm, d), lambda i: (i, 0)),
            pl.BlockSpec((tm, d), lambda i: (i, 0)),
            pl.BlockSpec((P_HEADS, N_KEYS, tm), lambda i: (0, 0, i)),
            pl.BlockSpec((P_HEADS, N_KEYS, tm), lambda i: (0, 0, i)),
        ],
        out_shape=[
            jax.ShapeDtypeStruct((t, d), F32),
            jax.ShapeDtypeStruct((t, d), F32),
            jax.ShapeDtypeStruct((P_HEADS, N_KEYS, t), F32),
            jax.ShapeDtypeStruct((P_HEADS, N_KEYS, t), F32),
        ],
        compiler_params=pltpu.CompilerParams(
            dimension_semantics=("parallel",),
            vmem_limit_bytes=VMEM_LIMIT),
        name="out_proj",
    )(x2, oa, ob, wo, fg, wq, k1, k2)


_PAIRS = [(i, j) for i in range(P_TOPK) for j in range(P_TOPK) if (i + 1) * (j + 1) <= P_TOPK]
_NCAND = -(-len(_PAIRS) // 8) * 8


def _topk_body(s1_ref, s2_ref, pos_ref, e_ref, g_ref, v_scr, i_scr, c_scr, ce_scr, b_scr, x_scr):
    tt = s1_ref.shape[2]
    kio = lax.broadcasted_iota(jnp.int32, (N_KEYS, tt), 0).astype(F32)
    neg = -jnp.inf
    for half, sref in ((0, s1_ref), (1, s2_ref)):
        s = sref[0]
        for r in range(P_TOPK):
            m = jnp.max(s, axis=0, keepdims=True)
            idx = jnp.min(jnp.where(s == m, kio, float(N_KEYS)), axis=0, keepdims=True)
            v_scr[half, r:r + 1, :] = m
            i_scr[half, r:r + 1, :] = idx
            s = jnp.where(kio == idx, neg, s)
    c_scr[...] = jnp.full(c_scr.shape, neg, F32)
    ce_scr[...] = jnp.zeros(ce_scr.shape, F32)
    for c, (i, j) in enumerate(_PAIRS):
        c_scr[c:c + 1, :] = v_scr[0, i:i + 1, :] + v_scr[1, j:j + 1, :]
        ce_scr[c:c + 1, :] = i_scr[0, i:i + 1, :] * float(N_KEYS) + i_scr[1, j:j + 1, :]
    cand = c_scr[...]
    ce = ce_scr[...]
    pos = jnp.broadcast_to(pos_ref[:, 0:1], cand.shape)
    for r in range(P_TOPK):
        m = jnp.max(cand, axis=0, keepdims=True)
        sel = jnp.min(jnp.where(cand == m, pos, 1e9), axis=0, keepdims=True)
        hit = pos == sel
        b_scr[r:r + 1, :] = m
        x_scr[r:r + 1, :] = jnp.max(jnp.where(hit, ce, -1.0), axis=0, keepdims=True)
        cand = jnp.where(hit, neg, cand)
    best = b_scr[...]
    ex = jnp.exp(best - best[0:1, :])
    g_ref[...] = ex / jnp.sum(ex, axis=0, keepdims=True)
    e_ref[...] = x_scr[...].astype(jnp.int32)


def _peer_topk(s1, s2, pos, tt):
    t = s1.shape[2]
    return pl.pallas_call(
        _topk_body,
        grid=(t // tt, P_HEADS),
        in_specs=[
            pl.BlockSpec((1, N_KEYS, tt), lambda i, h: (h, 0, i)),
            pl.BlockSpec((1, N_KEYS, tt), lambda i, h: (h, 0, i)),
            pl.BlockSpec((_NCAND, HD), lambda i, h: (0, 0)),
        ],
        out_specs=[
            pl.BlockSpec((P_TOPK, tt), lambda i, h: (h, i)),
            pl.BlockSpec((P_TOPK, tt), lambda i, h: (h, i)),
        ],
        out_shape=[
            jax.ShapeDtypeStruct((P_HEADS * P_TOPK, t), jnp.int32),
            jax.ShapeDtypeStruct((P_HEADS * P_TOPK, t), F32),
        ],
        scratch_shapes=[
            pltpu.VMEM((2, P_TOPK, tt), F32),
            pltpu.VMEM((2, P_TOPK, tt), F32),
            pltpu.VMEM((_NCAND, tt), F32),
            pltpu.VMEM((_NCAND, tt), F32),
            pltpu.VMEM((P_TOPK, tt), F32),
            pltpu.VMEM((P_TOPK, tt), F32),
        ],
        compiler_params=pltpu.CompilerParams(
            dimension_semantics=("parallel", "parallel"),
            vmem_limit_bytes=VMEM_LIMIT),
        name="peer_topk",
    )(s1, s2, pos)


PEER_TB = 128
PEER_G = 8
PEER_SETS = 4
DMA_QUEUES = 2
NSLOT = P_HEADS * P_TOPK
SG = 8
CC_AHEAD = 3
ROW_PITCH = 17
HID_UNROLL = 1
MIX_UNROLL = 2


def _gelu(x):
    return 0.5 * x * (1.0 + lax.erf(x * (2.0 ** -0.5)))


def _carry_loop(n, unroll, body, init):
    def trip(q, carry):
        for j in range(unroll):
            carry = body(q * unroll + j, carry)
        return carry
    return lax.fori_loop(0, n // unroll, trip, init)


def _pack_body(u_ref, v_ref, o_ref):
    ub = lax.bitcast_convert_type(u_ref[...].astype(BF16).astype(F32), jnp.uint32) >> 16
    vb = lax.bitcast_convert_type(v_ref[...].astype(BF16).astype(F32), jnp.uint32)
    w = (vb & jnp.uint32(0xFFFF0000)) | ub
    nlc = w.shape[1] // HD
    for c in range(nlc):
        o_ref[pl.ds(c, w.shape[0], stride=nlc), :] = w[:, c * HD:(c + 1) * HD]


def _peer_pack(u_tab, v_tab, te):
    e, d = u_tab.shape
    return pl.pallas_call(
        _pack_body,
        grid=(e // te,),
        in_specs=[pl.BlockSpec((te, d), lambda i: (i, 0)),
                  pl.BlockSpec((te, d), lambda i: (i, 0))],
        out_specs=pl.BlockSpec((te * (d // HD), HD), lambda i: (i, 0)),
        out_shape=jax.ShapeDtypeStruct((e * (d // HD), HD), jnp.uint32),
        compiler_params=pltpu.CompilerParams(
            dimension_semantics=("parallel",),
            vmem_limit_bytes=VMEM_LIMIT),
        name="peer_pack",
    )(u_tab, v_tab)


def _peer_body(idx_ref, gt_ref, hn_ref, h_ref, fg_ref, w_hbm, o_ref,
               hid_scr, c_scr, y_scr, sem, *bufs, final, nsteps):
    tb, d = hn_ref.shape
    ngroups = tb // PEER_G
    nsg = NSLOT // SG
    ahead = PEER_SETS - 1
    nlc = d // HD
    step = pl.program_id(0)

    def issue_tok(tok, set_, r, s0, s1):
        ids = idx_ref.at[tok]
        dst = bufs[set_].at[pl.ds(r * NSLOT * ROW_PITCH, NSLOT * ROW_PITCH)]
        for s in range(s0, s1):
            e0 = pl.multiple_of(ids[s] * nlc, nlc)
            pltpu.make_async_copy(w_hbm.at[pl.ds(e0, nlc)], dst.at[pl.ds(s * ROW_PITCH, nlc)],
                                  sem.at[set_]).start(priority=s % DMA_QUEUES)

    def wait_set(set_):
        full = bufs[set_].at[pl.ds(0, PEER_G * NSLOT * nlc)]
        pltpu.make_async_copy(full, full, sem.at[set_]).wait()

    @pl.when(step == 0)
    def _():
        def first(r, carry):
            for a in range(ahead):
                issue_tok(a * PEER_G + r, a, r, 0, NSLOT)
            return carry
        lax.fori_loop(0, PEER_G, first, 0)

    per_tile = NSLOT // 2 // nsg
    lane = lax.broadcasted_iota(jnp.int32, (SG, tb), 1)
    hid_scr[...] = jnp.zeros_like(hid_scr)

    def group_body(g, set_):
        nxt = (set_ + ahead) % PEER_SETS
        wait_set(set_)


        def store_hidden(accs, lane_id):
            for sg in range(nsg):
                pltpu.store(hid_scr.at[pl.ds(sg * SG, SG), :],
                            jnp.broadcast_to(jnp.sum(accs[sg], axis=1, keepdims=True), (SG, tb)),
                            mask=lane == lane_id)

        def hid_tok(r, prev):
            t = g * PEER_G + r
            xrow = hn_ref[pl.ds(t, 1), :]
            xb = [jnp.broadcast_to(xrow[:, lc * HD:(lc + 1) * HD], (SG, HD)) for lc in range(nlc)]
            tiles = bufs[set_].at[pl.ds(r * NSLOT * ROW_PITCH, NSLOT * ROW_PITCH)]
            accs = []
            for sg in range(nsg):
                issue_tok((g + ahead) * PEER_G + r, nxt, r, sg * per_tile, (sg + 1) * per_tile)
                acc = None
                for lc in range(nlc):
                    w = tiles[pl.ds(sg * SG * ROW_PITCH + lc, SG, stride=ROW_PITCH), :]
                    u = lax.bitcast_convert_type(w << 16, F32)
                    term = u * xb[lc]
                    acc = term if acc is None else acc + term
                accs.append(acc)
            store_hidden(prev, jnp.where(r > 0, t - 1, -1))
            return tuple(accs)
        last = _carry_loop(PEER_G, HID_UNROLL, hid_tok,
                           tuple(jnp.zeros((SG, HD), F32) for _ in range(nsg)))
        store_hidden(last, g * PEER_G + PEER_G - 1)

        c_scr[...] = gt_ref[...] * _gelu(hid_scr[...])

        def store_row(accs, row):
            y_scr[pl.ds(row, 1), :] = jnp.concatenate(
                [jnp.sum(a, axis=0, keepdims=True) for a in accs], axis=1)

        def mix_tok(r, prev):
            t = g * PEER_G + r
            tiles = bufs[set_].at[pl.ds(r * NSLOT * ROW_PITCH, NSLOT * ROW_PITCH)]
            def coef_col(sg):
                return jnp.sum(jnp.where(lane == t, c_scr[pl.ds(sg * SG, SG), :], 0.0),
                               axis=1, keepdims=True)
            cc = {sg: coef_col(sg) for sg in range(CC_AHEAD)}
            accs = [None] * nlc
            for sg in range(nsg):
                issue_tok((g + ahead) * PEER_G + r, nxt, r,
                          NSLOT // 2 + sg * per_tile, NSLOT // 2 + (sg + 1) * per_tile)
                if sg + CC_AHEAD < nsg:
                    cc[sg + CC_AHEAD] = coef_col(sg + CC_AHEAD)
                for lc in range(nlc):
                    w = tiles[pl.ds(sg * SG * ROW_PITCH + lc, SG, stride=ROW_PITCH), :]
                    v = lax.bitcast_convert_type(w & jnp.uint32(0xFFFF0000), F32)
                    term = v * cc[sg]
                    accs[lc] = term if accs[lc] is None else accs[lc] + term
            store_row(prev, jnp.where(r > 0, t - 1, tb))
            return tuple(accs)

        last = _carry_loop(PEER_G, MIX_UNROLL, mix_tok,
                           tuple(jnp.zeros((SG, HD), F32) for _ in range(nlc)))
        store_row(last, g * PEER_G + PEER_G - 1)

    def group_round(q, carry):
        for j in range(PEER_SETS):
            group_body(q * PEER_SETS + j, j)
        return carry

    lax.fori_loop(0, ngroups // PEER_SETS, group_round, 0)

    @pl.when(step == nsteps - 1)
    def _():
        for a in range(ahead):
            wait_set((nsteps * ngroups + a) % PEER_SETS)

    hh = h_ref[...] + y_scr[pl.ds(0, tb), :]
    if final:
        hh = hh * lax.rsqrt(jnp.mean(hh * hh, axis=-1, keepdims=True) + EPS) * fg_ref[...]
    o_ref[...] = hh


def _peer_mix(idx, gt, hn, h, fg, w_tab, final):
    t, d = hn.shape
    tb = PEER_TB
    nsteps = t // tb
    extra = (PEER_SETS - 1) * PEER_G
    assert (tb // PEER_G) % PEER_SETS == 0
    idx_pad = jnp.concatenate([idx, jnp.zeros((tb, NSLOT), idx.dtype)], axis=0)
    idx_ext = jnp.concatenate(
        [idx.reshape(nsteps, tb, NSLOT),
         idx_pad[tb:].reshape(nsteps, tb, NSLOT)[:, :extra]], axis=1)
    return pl.pallas_call(
        functools.partial(_peer_body, final=final, nsteps=nsteps),
        grid=(nsteps,),
        in_specs=[
            pl.BlockSpec((None, tb + extra, NSLOT), lambda i: (i, 0, 0),
                         memory_space=pltpu.SMEM),
            pl.BlockSpec((NSLOT, tb), lambda i: (0, i)),
            pl.BlockSpec((tb, d), lambda i: (i, 0)),
            pl.BlockSpec((tb, d), lambda i: (i, 0)),
            pl.BlockSpec((1, d), lambda i: (0, 0)),
            pl.BlockSpec(memory_space=pl.ANY),
        ],
        out_specs=pl.BlockSpec((tb, d), lambda i: (i, 0)),
        out_shape=jax.ShapeDtypeStruct((t, d), F32),
        scratch_shapes=[
            pltpu.VMEM((NSLOT, tb), F32),
            pltpu.VMEM((NSLOT, tb), F32),
            pltpu.VMEM((tb + SG, d), F32),
            pltpu.SemaphoreType.DMA((PEER_SETS,)),
        ] + [pltpu.VMEM((PEER_G * NSLOT * ROW_PITCH, HD), jnp.uint32) for _ in range(PEER_SETS)],
        compiler_params=pltpu.CompilerParams(
            dimension_semantics=("arbitrary",),
            vmem_limit_bytes=VMEM_LIMIT),
        name="peer_mix",
    )(idx_ext, gt, hn, h, fg, w_tab)


def _tile(n, pref):
    return pref if n % pref == 0 else n


def kernel(x, attn_norm_g, w_in, hgrn_lb_logits, hgrn_norm_g, gdn_conv_w, gdn_A_log, gdn_dt_bias,
           gdn_norm_g, w_out, ffn_norm_g, peer_w_query, peer_sub_keys, peer_u, peer_v,
           final_norm_g):
    b, s, d = x.shape
    t = b * s
    depth = w_in.shape[0]
    in_width = w_in.shape[2]
    n_small = 2 * B_V_HEADS
    n_main = in_width - n_small
    in_pad = n_main + HD
    lb_all = jnp.cumsum(jax.nn.softmax(hgrn_lb_logits.astype(F32), axis=0), axis=0)
    pos = jnp.array([i * P_TOPK + j for i, j in _PAIRS]
                    + [10 ** 6 + c for c in range(_NCAND - len(_PAIRS))], F32)
    pos = jnp.broadcast_to(pos[:, None], (_NCAND, HD))

    h2 = x.reshape(t, d)
    for l in range(depth):
        w_l = jnp.pad(w_in[l], ((0, 0), (0, in_pad - in_width))).astype(BF16)
        tn = in_pad // 3 if (in_pad // HD) % 3 == 0 else in_pad
        proj = _in_proj(h2, attn_norm_g[l][None, :], w_l, _tile(t, 512), tn)
        proj3 = proj.reshape(b, s, in_pad)

        o_a = _hgrn(proj3, lb_all[l], hgrn_norm_g[l][None, :], _tile(s, 256))

        ts = _tile(s, 256)
        small = proj3[:, :, n_main:n_main + n_small]
        smt = small.reshape(b, s // CHUNK, CHUNK, n_small).transpose(0, 1, 3, 2)
        zeros8 = jnp.zeros((B_V_HEADS,), F32)
        a_neg = jnp.exp(gdn_A_log[l].astype(F32))
        dtb = gdn_dt_bias[l].astype(F32)
        pcol = jnp.zeros((2, HD), F32)
        pcol = pcol.at[0, B_V_HEADS:n_small].set(a_neg).at[1, B_V_HEADS:n_small].set(dtb)
        prow = jnp.zeros((n_small, HD), F32)
        prow = prow.at[:, 0].set(jnp.concatenate([zeros8, a_neg]))
        prow = prow.at[:, 1].set(jnp.concatenate([zeros8, dtb]))
        o_b = _gdn(proj3, smt, gdn_conv_w[l], pcol, prow, gdn_norm_g[l][None, :], ts)

        wq = peer_w_query[l].reshape(d, P_HEADS * 2 * HD).astype(BF16)
        h2, hn, s1, s2 = _out_proj(
            h2, o_a.reshape(t, -1), o_b.reshape(t, -1), w_out[l].astype(BF16),
            ffn_norm_g[l][None, :], wq, peer_sub_keys[l, 0], peer_sub_keys[l, 1], _tile(t, 256))

        e_t, g_t = _peer_topk(s1, s2, pos, _tile(t, 512))
        w_tab = _peer_pack(peer_u[l], peer_v[l], _tile(peer_u.shape[1], 256))
        h2 = _peer_mix(e_t.T, g_t, hn, h2, final_norm_g[None, :], w_tab, l == depth - 1)
    return h2.reshape(b, s, d)
```

```python
import functools
from typing import NamedTuple

import jax
import jax.numpy as jnp
from jax import lax
from jax.experimental import pallas as pl
from jax.experimental.pallas import tpu as pltpu

F32 = jnp.float32
BF16 = jnp.bfloat16

EPS = 1e-6
CHUNK = 64
SUB = 16
HD = 128
A_HEADS = 8
B_QK_HEADS = 4
B_V_HEADS = 8
CONV = 4
P_HEADS = 8
N_KEYS = 128
P_TOPK = 16
SUBLANES = 8
VMEM_LIMIT = 56 * 1024 * 1024


def _nt(a, b):
    return lax.dot_general(a, b, (((1,), (1,)), ((), ())), preferred_element_type=F32)


def _tn(a, b):
    return lax.dot_general(a, b, (((0,), (0,)), ((), ())), preferred_element_type=F32)


def _mm(a, b):
    return jnp.dot(a, b, preferred_element_type=F32)


def _split_bf16(x, n):
    parts = []
    for _ in range(n - 1):
        p = x.astype(BF16)
        parts.append(p)
        x = x - p.astype(F32)
    parts.append(x.astype(BF16))
    return parts


def _mm3(a, b):
    ah, al = _split_bf16(a, 2)
    bh, bl = _split_bf16(b, 2)
    return _mm(ah, bh) + (_mm(ah, bl) + _mm(al, bh))


def _cumsum_mm(tri, g, g_is_lhs):
    t16 = tri.astype(BF16)
    out = None
    for p in reversed(_split_bf16(g, 3)):
        term = _mm(p, t16) if g_is_lhs else _mm(t16, p)
        out = term if out is None else out + term
    return out


def _sigmoid(x):
    return 1.0 / (1.0 + jnp.exp(-x))


def _softplus(x):
    return jnp.maximum(x, 0.0) + jnp.log1p(jnp.exp(-jnp.abs(x)))


def _bcast_rows(x, idx):
    n = x.shape[1]
    return jnp.concatenate(
        [jnp.broadcast_to(x[r:r + 1, :], (SUB, n)) for r in idx], axis=0)


def _inproj_body(x_ref, g_ref, w_ref, o_ref):
    x = x_ref[...]
    ms = jnp.mean(x * x, axis=-1, keepdims=True)
    xn = (x * lax.rsqrt(ms + EPS) * g_ref[...]).astype(BF16)
    o_ref[...] = _mm(xn, w_ref[...])


def _in_proj(x2, g, w, tm, tn):
    t, d = x2.shape
    n = w.shape[1]
    return pl.pallas_call(
        _inproj_body,
        grid=(n // tn, t // tm),
        in_specs=[
            pl.BlockSpec((tm, d), lambda j, i: (i, 0)),
            pl.BlockSpec((1, d), lambda j, i: (0, 0)),
            pl.BlockSpec((d, tn), lambda j, i: (0, j)),
        ],
        out_specs=pl.BlockSpec((tm, tn), lambda j, i: (i, j)),
        out_shape=jax.ShapeDtypeStruct((t, n), F32),
        compiler_params=pltpu.CompilerParams(
            dimension_semantics=("parallel", "parallel"),
            vmem_limit_bytes=VMEM_LIMIT),
        name="in_proj",
    )(x2, g, w)


def _hgrn_body(q_ref, f_ref, i_ref, g_ref, lb_ref, gn_ref, o_ref, st_ref, *, ts):
    @pl.when(pl.program_id(1) == 0)
    def _():
        st_ref[...] = jnp.zeros_like(st_ref)

    row = lax.broadcasted_iota(jnp.int32, (CHUNK, CHUNK), 0)
    col = lax.broadcasted_iota(jnp.int32, (CHUNK, CHUNK), 1)
    tril_f = (col <= row).astype(F32)
    blk_r = row // SUB
    blk_c = col // SUB
    rblk = lax.broadcasted_iota(jnp.int32, (CHUNK, HD), 0) // SUB
    nsub = CHUNK // SUB
    gn = gn_ref[...]

    def chunk_body(c, carry):
        r0 = pl.multiple_of(c * CHUNK, CHUNK)
        rows = pl.ds(r0, CHUNK)
        hs = range(A_HEADS)
        cols = [slice(h * HD, (h + 1) * HD) for h in hs]
        q = [q_ref[0, rows, cols[h]] for h in hs]
        z = [f_ref[0, rows, cols[h]] for h in hs]
        v16 = [i_ref[0, rows, cols[h]].astype(BF16) for h in hs]
        lb = [lb_ref[h:h + 1, :] for h in hs]
        sts = [st_ref[h] for h in hs]
        k = [(1.0 - lb[h]) * _sigmoid(-z[h]) for h in hs]
        b = [_cumsum_mm(tril_f, jnp.log(lb[h] + (1.0 - lb[h]) * _sigmoid(z[h])), False)
             for h in hs]
        o = [_nt((q[h] * jnp.exp(b[h])).astype(BF16), sts[h].astype(BF16)) for h in hs]
        att = []
        for h in hs:
            bnd = [b[h][SUB * j + SUB - 1:SUB * j + SUB, :] for j in range(nsub)]
            k_off = k[h] * jnp.exp(
                _bcast_rows(b[h], [SUB * j + SUB - 1 for j in range(nsub)]) - b[h])
            qcat = jnp.concatenate(
                [q[h] * jnp.exp(jnp.minimum(b[h] - bnd[j], 0.0)) for j in range(nsub - 1)],
                axis=1)
            kcat = jnp.concatenate(
                [jnp.where(rblk == j, k_off, 0.0) for j in range(nsub - 1)], axis=1)
            att_off = _nt(qcat.astype(BF16), kcat.astype(BF16))
            ref_rows = _bcast_rows(b[h], [SUB * i for i in range(nsub)])
            q_d = q[h] * jnp.exp(b[h] - ref_rows)
            k_d = k[h] * jnp.exp(ref_rows - b[h])
            att_d = _nt(q_d.astype(BF16), k_d.astype(BF16))
            att.append(jnp.where(blk_r > blk_c, att_off,
                                 jnp.where((blk_r == blk_c) & (col <= row), att_d, 0.0)))
        o = [o[h] + _mm(att[h].astype(BF16), v16[h]) for h in hs]
        new_sts = []
        for h in hs:
            b_last = b[h][CHUNK - 1:CHUNK, :]
            kdec = k[h] * jnp.exp(b_last - b[h])
            new_sts.append(sts[h] * jnp.exp(b_last) + _tn(v16[h], kdec.astype(BF16)))
        for h in hs:
            st_ref[h] = new_sts[h]
        for h in hs:
            gate = g_ref[0, rows, cols[h]]
            y = o[h] * lax.rsqrt(jnp.mean(o[h] * o[h], axis=-1, keepdims=True) + EPS)
            o_ref[0, rows, cols[h]] = (y * gn * (gate * _sigmoid(gate))).astype(o_ref.dtype)
        return carry

    lax.fori_loop(0, ts // CHUNK, chunk_body, 0)


def _hgrn(proj3, lb, gn, ts):
    b, s, _ = proj3.shape
    w = A_HEADS * HD
    return pl.pallas_call(
        functools.partial(_hgrn_body, ts=ts),
        grid=(b, s // ts),
        in_specs=[
            pl.BlockSpec((1, ts, w), lambda bi, ti: (bi, ti, 0)),
            pl.BlockSpec((1, ts, w), lambda bi, ti: (bi, ti, 1)),
            pl.BlockSpec((1, ts, w), lambda bi, ti: (bi, ti, 2)),
            pl.BlockSpec((1, ts, w), lambda bi, ti: (bi, ti, 3)),
            pl.BlockSpec((A_HEADS, HD), lambda bi, ti: (0, 0)),
            pl.BlockSpec((1, HD), lambda bi, ti: (0, 0)),
        ],
        out_specs=pl.BlockSpec((1, ts, w), lambda bi, ti: (bi, ti, 0)),
        out_shape=jax.ShapeDtypeStruct((b, s, w), BF16),
        scratch_shapes=[pltpu.VMEM((A_HEADS, HD, HD), F32)],
        compiler_params=pltpu.CompilerParams(
            dimension_semantics=("parallel", "arbitrary"),
            vmem_limit_bytes=VMEM_LIMIT),
        name="hgrn",
    )(proj3, proj3, proj3, proj3, lb, gn)


def _gdn_body(bq_ref, bk_ref, bv_ref, bz_ref, sm_ref, smt_ref, cw_ref, pcol_ref, prow_ref,
              gn_ref, o_ref, s_ref, tail_ref, qn_ref, kn_ref, vv_ref,
              u_s, w_s, qg_s, kd_s, aqk_s, dl_s, *, ts):
    nqk = B_QK_HEADS * HD

    @pl.when(pl.program_id(1) == 0)
    def _():
        s_ref[...] = jnp.zeros_like(s_ref)
        tail_ref[...] = jnp.zeros_like(tail_ref)

    def conv_silu(x_ref, h, c0):
        cs = slice(h * HD, (h + 1) * HD)
        cc = slice(c0 + h * HD, c0 + (h + 1) * HD)
        x = x_ref[0, :, cs]
        xe = jnp.concatenate([tail_ref[:, cc], x], axis=0)
        w = cw_ref[:, cc]
        y = x * w[CONV - 1:CONV, :]
        for j in range(1, CONV):
            y = y + pltpu.roll(xe, j, axis=0)[SUBLANES:, :] * w[CONV - 1 - j:CONV - j, :]
        tail_ref[:, cc] = x[ts - SUBLANES:, :]
        return y * _sigmoid(y)

    def l2n(x):
        return x * lax.rsqrt(jnp.sum(x * x, axis=-1, keepdims=True) + EPS)

    for h in range(B_QK_HEADS):
        cs = slice(h * HD, (h + 1) * HD)
        qn_ref[:, cs] = l2n(conv_silu(bq_ref, h, 0)) * (HD ** -0.5)
        kn_ref[:, cs] = l2n(conv_silu(bk_ref, h, nqk))
    for h in range(B_V_HEADS):
        vv_ref[:, h * HD:(h + 1) * HD] = conv_silu(bv_ref, h, 2 * nqk)

    row = lax.broadcasted_iota(jnp.int32, (CHUNK, CHUNK), 0)
    col = lax.broadcasted_iota(jnp.int32, (CHUNK, CHUNK), 1)
    incl = col <= row
    strict = col < row
    tril_f = incl.astype(F32)
    triu_f = (row <= col).astype(F32)
    same_blk = (row // SUB) == (col // SUB)
    eye = (row == col).astype(F32)
    gn = gn_ref[...]
    a_col = pcol_ref[0:1, :]
    dt_col = pcol_ref[1:2, :]
    a_row = prow_ref[:, 0:1]
    dt_row = prow_ref[:, 1:2]
    rep = B_V_HEADS // B_QK_HEADS

    def prep_chunk(c, carry):
        r0 = pl.multiple_of(c * CHUNK, CHUNK)
        rows = pl.ds(r0, CHUNK)
        sm = sm_ref[0, rows, :]
        beta_c = _sigmoid(sm)
        gc_c = _cumsum_mm(tril_f, -a_col * _softplus(sm + dt_col), False)
        smt = smt_ref[0, c]
        gc_r = _cumsum_mm(triu_f, -a_row * _softplus(smt + dt_row), True)
        hs = range(B_V_HEADS)
        qq = [qn_ref[rows, (h // rep) * HD:(h // rep + 1) * HD] for h in hs]
        kk = [kn_ref[rows, (h // rep) * HD:(h // rep + 1) * HD] for h in hs]
        kk16 = [k.astype(BF16) for k in kk]
        bcol = [beta_c[:, h:h + 1] for h in hs]
        gcol = [gc_c[:, B_V_HEADS + h:B_V_HEADS + h + 1] for h in hs]
        decay = [jnp.where(incl, jnp.exp(jnp.minimum(
            gcol[h] - gc_r[B_V_HEADS + h:B_V_HEADS + h + 1, :], 0.0)), 0.0) for h in hs]
        kb = [kk[h] * bcol[h] for h in hs]
        a = [jnp.where(strict, _nt(kb[h].astype(BF16), kk16[h]) * decay[h], 0.0) for h in hs]
        dg = [jnp.where(same_blk, a[h], 0.0) for h in hs]
        off = [a[h] - dg[h] for h in hs]
        tinv = [eye - dg[h] for h in hs]
        p = dg
        for _ in range(3):
            p = [_mm3(p[h], p[h]) for h in hs]
            tinv = [tinv[h] + _mm3(tinv[h], p[h]) for h in hs]
        pm = [_mm3(tinv[h], off[h]) for h in hs]
        pm2 = [_mm3(pm[h], pm[h]) for h in hs]
        egc = [jnp.exp(gcol[h]) for h in hs]
        x = [_mm3(tinv[h], jnp.concatenate(
            [vv_ref[rows, h * HD:(h + 1) * HD] * bcol[h], kb[h] * egc[h]], axis=1)) for h in hs]
        x = [x[h] + _mm3(pm2[h], x[h]) for h in hs]
        x = [x[h] - _mm3(pm[h], x[h]) for h in hs]
        aqk = [jnp.where(incl, _nt(qq[h].astype(BF16), kk16[h]) * decay[h], 0.0) for h in hs]
        for h in hs:
            cs = slice(h * HD, (h + 1) * HD)
            g_last = gcol[h][CHUNK - 1:CHUNK, :]
            u_s[rows, cs] = x[h][:, :HD]
            w_s[rows, cs] = x[h][:, HD:].astype(BF16)
            qg_s[rows, cs] = (qq[h] * egc[h]).astype(BF16)
            kd_s[rows, cs] = (kk[h] * jnp.exp(g_last - gcol[h])).astype(BF16)
            aqk_s[c, h] = aqk[h].astype(BF16)
            dl_s[c, h:h + 1, :] = jnp.broadcast_to(jnp.exp(g_last), (1, HD))
        return carry

    lax.fori_loop(0, ts // CHUNK, prep_chunk, 0)

    def scan_chunk(c, carry):
        r0 = pl.multiple_of(c * CHUNK, CHUNK)
        rows = pl.ds(r0, CHUNK)
        hs = range(B_V_HEADS)
        cols = [slice(h * HD, (h + 1) * HD) for h in hs]
        sts = [s_ref[h] for h in hs]
        st16 = [s.astype(BF16) for s in sts]
        v16 = [(u_s[rows, cols[h]] - _mm(w_s[rows, cols[h]], st16[h])).astype(BF16) for h in hs]
        o = [_mm(qg_s[rows, cols[h]], st16[h]) + _mm(aqk_s[c, h], v16[h]) for h in hs]
        new_sts = [sts[h] * dl_s[c, h:h + 1, :] + _tn(kd_s[rows, cols[h]], v16[h]) for h in hs]
        for h in hs:
            s_ref[h] = new_sts[h]
        for h in hs:
            zz = bz_ref[0, rows, cols[h]]
            y = o[h] * lax.rsqrt(jnp.mean(o[h] * o[h], axis=-1, keepdims=True) + EPS)
            o_ref[0, rows, cols[h]] = (y * gn * (zz * _sigmoid(zz))).astype(o_ref.dtype)
        return carry

    lax.fori_loop(0, ts // CHUNK, scan_chunk, 0)


def _gdn(proj3, smt, conv_w, pcol, prow, gn, ts):
    b, s, _ = proj3.shape
    nqk = B_QK_HEADS * HD
    wv = B_V_HEADS * HD
    off_q = (4 * A_HEADS * HD) // nqk
    off_v = (4 * A_HEADS * HD + 2 * nqk) // wv
    off_s = (4 * A_HEADS * HD + 2 * nqk + 2 * wv) // HD
    nch = ts // CHUNK
    return pl.pallas_call(
        functools.partial(_gdn_body, ts=ts),
        grid=(b, s // ts),
        in_specs=[
            pl.BlockSpec((1, ts, nqk), lambda bi, ti: (bi, ti, off_q)),
            pl.BlockSpec((1, ts, nqk), lambda bi, ti: (bi, ti, off_q + 1)),
            pl.BlockSpec((1, ts, wv), lambda bi, ti: (bi, ti, off_v)),
            pl.BlockSpec((1, ts, wv), lambda bi, ti: (bi, ti, off_v + 1)),
            pl.BlockSpec((1, ts, HD), lambda bi, ti: (bi, ti, off_s)),
            pl.BlockSpec((1, nch, 2 * B_V_HEADS, CHUNK), lambda bi, ti: (bi, ti, 0, 0)),
            pl.BlockSpec((CONV, 2 * nqk + wv), lambda bi, ti: (0, 0)),
            pl.BlockSpec((2, HD), lambda bi, ti: (0, 0)),
            pl.BlockSpec((2 * B_V_HEADS, HD), lambda bi, ti: (0, 0)),
            pl.BlockSpec((1, HD), lambda bi, ti: (0, 0)),
        ],
        out_specs=pl.BlockSpec((1, ts, wv), lambda bi, ti: (bi, ti, 0)),
        out_shape=jax.ShapeDtypeStruct((b, s, wv), BF16),
        scratch_shapes=[
            pltpu.VMEM((B_V_HEADS, HD, HD), F32),
            pltpu.VMEM((SUBLANES, 2 * nqk + wv), F32),
            pltpu.VMEM((ts, nqk), F32),
            pltpu.VMEM((ts, nqk), F32),
            pltpu.VMEM((ts, wv), F32),
            pltpu.VMEM((ts, wv), F32),
            pltpu.VMEM((ts, wv), BF16),
            pltpu.VMEM((ts, wv), BF16),
            pltpu.VMEM((ts, wv), BF16),
            pltpu.VMEM((nch, B_V_HEADS, CHUNK, CHUNK), BF16),
            pltpu.VMEM((nch, B_V_HEADS, HD), F32),
        ],
        compiler_params=pltpu.CompilerParams(
            dimension_semantics=("parallel", "arbitrary"),
            vmem_limit_bytes=VMEM_LIMIT),
        name="gdn",
    )(proj3, proj3, proj3, proj3, proj3, smt, conv_w, pcol, prow, gn)


def _outproj_body(x_ref, oa_ref, ob_ref, wo_ref, fg_ref, wq_ref, k1_ref, k2_ref,
                  h_ref, hn_ref, s1_ref, s2_ref):
    wa = oa_ref.shape[1]
    h = x_ref[...] + _mm(oa_ref[...], wo_ref[:wa, :]) + _mm(ob_ref[...], wo_ref[wa:, :])
    h_ref[...] = h
    hn = h * lax.rsqrt(jnp.mean(h * h, axis=-1, keepdims=True) + EPS) * fg_ref[...]
    hn_ref[...] = hn
    qry = _mm(hn.astype(BF16), wq_ref[...])
    for hh in range(P_HEADS):
        q1 = qry[:, hh * 2 * HD:hh * 2 * HD + HD].astype(BF16)
        q2 = qry[:, hh * 2 * HD + HD:(hh + 1) * 2 * HD].astype(BF16)
        s1_ref[hh] = _nt(k1_ref[hh].astype(BF16), q1)
        s2_ref[hh] = _nt(k2_ref[hh].astype(BF16), q2)


def _out_proj(x2, oa, ob, wo, fg, wq, k1, k2, tm):
    t, d = x2.shape
    wa = oa.shape[1]
    wb = ob.shape[1]
    const = dict(pipeline_mode=pl.Buffered(1))
    return pl.pallas_call(
        _outproj_body,
        grid=(t // tm,),
        in_specs=[
            pl.BlockSpec((tm, d), lambda i: (i, 0)),
            pl.BlockSpec((tm, wa), lambda i: (i, 0)),
            pl.BlockSpec((tm, wb), lambda i: (i, 0)),
            pl.BlockSpec((wa + wb, d), lambda i: (0, 0), **const),
            pl.BlockSpec((1, d), lambda i: (0, 0)),
            pl.BlockSpec((d, P_HEADS * 2 * HD), lambda i: (0, 0), **const),
            pl.BlockSpec((P_HEADS, N_KEYS, HD), lambda i: (0, 0, 0)),
            pl.BlockSpec((P_HEADS, N_KEYS, HD), lambda i: (0, 0, 0)),
        ],
        out_specs=[
            pl.BlockSpec((tm, d), lambda i: (i, 0)),
            pl.BlockSpec((tm, d), lambda i: (i, 0)),
            pl.BlockSpec((P_HEADS, N_KEYS, tm), lambda i: (0, 0, i)),
            pl.BlockSpec((P_HEADS, N_KEYS, tm), lambda i: (0, 0, i)),
        ],
        out_shape=[
            jax.ShapeDtypeStruct((t, d), F32),
            jax.ShapeDtypeStruct((t, d), F32),
            jax.ShapeDtypeStruct((P_HEADS, N_KEYS, t), F32),
            jax.ShapeDtypeStruct((P_HEADS, N_KEYS, t), F32),
        ],
        compiler_params=pltpu.CompilerParams(
            dimension_semantics=("parallel",),
            vmem_limit_bytes=VMEM_LIMIT),
        name="out_proj",
    )(x2, oa, ob, wo, fg, wq, k1, k2)


_PAIRS = [(i, j) for i in range(P_TOPK) for j in range(P_TOPK) if (i + 1) * (j + 1) <= P_TOPK]
_NCAND = -(-len(_PAIRS) // 8) * 8


def _topk_body(s1_ref, s2_ref, pos_ref, e_ref, g_ref, v_scr, i_scr, c_scr, ce_scr, b_scr, x_scr):
    tt = s1_ref.shape[2]
    kio = lax.broadcasted_iota(jnp.int32, (N_KEYS, tt), 0).astype(F32)
    neg = -jnp.inf
    for half, sref in ((0, s1_ref), (1, s2_ref)):
        s = sref[0]
        for r in range(P_TOPK):
            m = jnp.max(s, axis=0, keepdims=True)
            idx = jnp.min(jnp.where(s == m, kio, float(N_KEYS)), axis=0, keepdims=True)
            v_scr[half, r:r + 1, :] = m
            i_scr[half, r:r + 1, :] = idx
            s = jnp.where(kio == idx, neg, s)
    c_scr[...] = jnp.full(c_scr.shape, neg, F32)
    ce_scr[...] = jnp.zeros(ce_scr.shape, F32)
    for c, (i, j) in enumerate(_PAIRS):
        c_scr[c:c + 1, :] = v_scr[0, i:i + 1, :] + v_scr[1, j:j + 1, :]
        ce_scr[c:c + 1, :] = i_scr[0, i:i + 1, :] * float(N_KEYS) + i_scr[1, j:j + 1, :]
    cand = c_scr[...]
    ce = ce_scr[...]
    pos = jnp.broadcast_to(pos_ref[:, 0:1], cand.shape)
    for r in range(P_TOPK):
        m = jnp.max(cand, axis=0, keepdims=True)
        sel = jnp.min(jnp.where(cand == m, pos, 1e9), axis=0, keepdims=True)
        hit = pos == sel
        b_scr[r:r + 1, :] = m
        x_scr[r:r + 1, :] = jnp.max(jnp.where(hit, ce, -1.0), axis=0, keepdims=True)
        cand = jnp.where(hit, neg, cand)
    best = b_scr[...]
    ex = jnp.exp(best - best[0:1, :])
    g_ref[...] = ex / jnp.sum(ex, axis=0, keepdims=True)
    e_ref[...] = x_scr[...].astype(jnp.int32)


def _peer_topk(s1, s2, pos, tt):
    t = s1.shape[2]
    return pl.pallas_call(
        _topk_body,
        grid=(t // tt, P_HEADS),
        in_specs=[
            pl.BlockSpec((1, N_KEYS, tt), lambda i, h: (h, 0, i)),
            pl.BlockSpec((1, N_KEYS, tt), lambda i, h: (h, 0, i)),
            pl.BlockSpec((_NCAND, HD), lambda i, h: (0, 0)),
        ],
        out_specs=[
            pl.BlockSpec((P_TOPK, tt), lambda i, h: (h, i)),
            pl.BlockSpec((P_TOPK, tt), lambda i, h: (h, i)),
        ],
        out_shape=[
            jax.ShapeDtypeStruct((P_HEADS * P_TOPK, t), jnp.int32),
            jax.ShapeDtypeStruct((P_HEADS * P_TOPK, t), F32),
        ],
        scratch_shapes=[
            pltpu.VMEM((2, P_TOPK, tt), F32),
            pltpu.VMEM((2, P_TOPK, tt), F32),
            pltpu.VMEM((_NCAND, tt), F32),
            pltpu.VMEM((_NCAND, tt), F32),
            pltpu.VMEM((P_TOPK, tt), F32),
            pltpu.VMEM((P_TOPK, tt), F32),
        ],
        compiler_params=pltpu.CompilerParams(
            dimension_semantics=("parallel", "parallel"),
            vmem_limit_bytes=VMEM_LIMIT),
        name="peer_topk",
    )(s1, s2, pos)


PEER_TB = 128
PEER_G = 8
PEER_SETS = 4
DMA_QUEUES = 2
NSLOT = P_HEADS * P_TOPK
SG = SUBLANES
CC_AHEAD = 3
ROW_PITCH = 17
HID_UNROLL = 1
MIX_UNROLL = 2


def _gelu(x):
    return 0.5 * x * (1.0 + lax.erf(x * (2.0 ** -0.5)))


def _unrolled_loop(n, unroll, body):
    def trip(q, carry):
        for j in range(unroll):
            body(q * unroll + j, 0)
        return carry
    lax.fori_loop(0, n // unroll, trip, 0)


def _pack_body(u_ref, v_ref, o_ref):
    ub = lax.bitcast_convert_type(u_ref[...].astype(BF16).astype(F32), jnp.uint32) >> 16
    vb = lax.bitcast_convert_type(v_ref[...].astype(BF16).astype(F32), jnp.uint32)
    w = (vb & jnp.uint32(0xFFFF0000)) | ub
    nlc = w.shape[1] // HD
    for c in range(nlc):
        o_ref[pl.ds(c, w.shape[0], stride=nlc), :] = w[:, c * HD:(c + 1) * HD]


def _peer_pack(u_tab, v_tab, te):
    e, d = u_tab.shape
    return pl.pallas_call(
        _pack_body,
        grid=(e // te,),
        in_specs=[pl.BlockSpec((te, d), lambda i: (i, 0)),
                  pl.BlockSpec((te, d), lambda i: (i, 0))],
        out_specs=pl.BlockSpec((te * (d // HD), HD), lambda i: (i, 0)),
        out_shape=jax.ShapeDtypeStruct((e * (d // HD), HD), jnp.uint32),
        compiler_params=pltpu.CompilerParams(
            dimension_semantics=("parallel",),
            vmem_limit_bytes=VMEM_LIMIT),
        name="peer_pack",
    )(u_tab, v_tab)


def _peer_body(idx_ref, gt_ref, hn_ref, h_ref, fg_ref, w_hbm, o_ref,
               hid_scr, c_scr, y_scr, sem, *bufs, final, nsteps):
    tb, d = hn_ref.shape
    ngroups = tb // PEER_G
    nsg = NSLOT // SG
    ahead = PEER_SETS - 1
    nlc = d // HD
    step = pl.program_id(0)

    def issue_tok(tok, set_, r, s0, s1):
        ids = idx_ref.at[tok]
        dst = bufs[set_].at[pl.ds(r * NSLOT * ROW_PITCH, NSLOT * ROW_PITCH)]
        for s in range(s0, s1):
            e0 = pl.multiple_of(ids[s] * nlc, nlc)
            pltpu.make_async_copy(w_hbm.at[pl.ds(e0, nlc)], dst.at[pl.ds(s * ROW_PITCH, nlc)],
                                  sem.at[set_]).start(priority=s % DMA_QUEUES)

    def wait_set(set_):
        full = bufs[set_].at[pl.ds(0, PEER_G * NSLOT * nlc)]
        pltpu.make_async_copy(full, full, sem.at[set_]).wait()

    @pl.when(step == 0)
    def _():
        def first(r, carry):
            for a in range(ahead):
                issue_tok(a * PEER_G + r, a, r, 0, NSLOT)
            return carry
        lax.fori_loop(0, PEER_G, first, 0)

    per_tile = NSLOT // 2 // nsg
    lane = lax.broadcasted_iota(jnp.int32, (SG, tb), 1)
    hid_scr[...] = jnp.zeros_like(hid_scr)

    def group_body(g, set_):
        nxt = (set_ + ahead) % PEER_SETS
        wait_set(set_)


        def hid_tok(r, c2):
            t = g * PEER_G + r
            xrow = hn_ref[pl.ds(t, 1), :]
            xb = [jnp.broadcast_to(xrow[:, lc * HD:(lc + 1) * HD], (SG, HD)) for lc in range(nlc)]
            tiles = bufs[set_].at[pl.ds(r * NSLOT * ROW_PITCH, NSLOT * ROW_PITCH)]
            hs = []
            for sg in range(nsg):
                issue_tok((g + ahead) * PEER_G + r, nxt, r, sg * per_tile, (sg + 1) * per_tile)
                acc = None
                for lc in range(nlc):
                    w = tiles[pl.ds(sg * SG * ROW_PITCH + lc, SG, stride=ROW_PITCH), :]
                    u = lax.bitcast_convert_type(w << 16, F32)
                    term = u * xb[lc]
                    acc = term if acc is None else acc + term
                hs.append(jnp.sum(acc, axis=1, keepdims=True))
            for sg in range(nsg):
                pltpu.store(hid_scr.at[pl.ds(sg * SG, SG), :],
                            jnp.broadcast_to(hs[sg], (SG, tb)), mask=lane == t)
            return c2
        _unrolled_loop(PEER_G, HID_UNROLL, hid_tok)

        c_scr[...] = gt_ref[...] * _gelu(hid_scr[...])

        def mix_tok(r, c2):
            t = g * PEER_G + r
            tiles = bufs[set_].at[pl.ds(r * NSLOT * ROW_PITCH, NSLOT * ROW_PITCH)]
            def coef_col(sg):
                return jnp.sum(jnp.where(lane == t, c_scr[pl.ds(sg * SG, SG), :], 0.0),
                               axis=1, keepdims=True)
            cc = {sg: coef_col(sg) for sg in range(CC_AHEAD)}
            accs = [None] * nlc
            for sg in range(nsg):
                issue_tok((g + ahead) * PEER_G + r, nxt, r,
                          NSLOT // 2 + sg * per_tile, NSLOT // 2 + (sg + 1) * per_tile)
                if sg + CC_AHEAD < nsg:
                    cc[sg + CC_AHEAD] = coef_col(sg + CC_AHEAD)
                for lc in range(nlc):
                    w = tiles[pl.ds(sg * SG * ROW_PITCH + lc, SG, stride=ROW_PITCH), :]
                    v = lax.bitcast_convert_type(w & jnp.uint32(0xFFFF0000), F32)
                    term = v * cc[sg]
                    accs[lc] = term if accs[lc] is None else accs[lc] + term
            yrow = jnp.concatenate(
                [jnp.sum(a, axis=0, keepdims=True) for a in accs], axis=1)
            y_scr[pl.ds(t, 1), :] = yrow
            return c2
        _unrolled_loop(PEER_G, MIX_UNROLL, mix_tok)

    def group_round(q, carry):
        for j in range(PEER_SETS):
            group_body(q * PEER_SETS + j, j)
        return carry

    lax.fori_loop(0, ngroups // PEER_SETS, group_round, 0)

    @pl.when(step == nsteps - 1)
    def _():
        for a in range(ahead):
            wait_set((nsteps * ngroups + a) % PEER_SETS)

    hh = h_ref[...] + y_scr[...]
    if final:
        hh = hh * lax.rsqrt(jnp.mean(hh * hh, axis=-1, keepdims=True) + EPS) * fg_ref[...]
    o_ref[...] = hh


def _peer_mix(idx, gt, hn, h, fg, w_tab, final):
    t, d = hn.shape
    tb = PEER_TB
    nsteps = t // tb
    extra = (PEER_SETS - 1) * PEER_G
    assert (tb // PEER_G) % PEER_SETS == 0
    idx_pad = jnp.concatenate([idx, jnp.zeros((tb, NSLOT), idx.dtype)], axis=0)
    idx_ext = jnp.concatenate(
        [idx.reshape(nsteps, tb, NSLOT),
         idx_pad[tb:].reshape(nsteps, tb, NSLOT)[:, :extra]], axis=1)
    return pl.pallas_call(
        functools.partial(_peer_body, final=final, nsteps=nsteps),
        grid=(nsteps,),
        in_specs=[
            pl.BlockSpec((None, tb + extra, NSLOT), lambda i: (i, 0, 0),
                         memory_space=pltpu.SMEM),
            pl.BlockSpec((NSLOT, tb), lambda i: (0, i)),
            pl.BlockSpec((tb, d), lambda i: (i, 0)),
            pl.BlockSpec((tb, d), lambda i: (i, 0)),
            pl.BlockSpec((1, d), lambda i: (0, 0)),
            pl.BlockSpec(memory_space=pl.ANY),
        ],
        out_specs=pl.BlockSpec((tb, d), lambda i: (i, 0)),
        out_shape=jax.ShapeDtypeStruct((t, d), F32),
        scratch_shapes=[
            pltpu.VMEM((NSLOT, tb), F32),
            pltpu.VMEM((NSLOT, tb), F32),
            pltpu.VMEM((tb, d), F32),
            pltpu.SemaphoreType.DMA((PEER_SETS,)),
        ] + [pltpu.VMEM((PEER_G * NSLOT * ROW_PITCH, HD), jnp.uint32) for _ in range(PEER_SETS)],
        compiler_params=pltpu.CompilerParams(
            dimension_semantics=("arbitrary",),
            vmem_limit_bytes=VMEM_LIMIT),
        name="peer_mix",
    )(idx_ext, gt, hn, h, fg, w_tab)


def _tile(n, pref):
    return pref if n % pref == 0 else n


class _Tiles(NamedTuple):
    in_proj_rows: int
    in_proj_cols: int
    seq: int
    out_proj_rows: int
    topk_tokens: int
    pack_experts: int


def _plan_tiles(t, s, in_pad, n_experts):
    third = in_pad // 3 if (in_pad // HD) % 3 == 0 else in_pad
    return _Tiles(in_proj_rows=_tile(t, 512), in_proj_cols=third, seq=_tile(s, 256),
                  out_proj_rows=_tile(t, 256), topk_tokens=_tile(t, 512),
                  pack_experts=_tile(n_experts, 256))


def kernel(x, attn_norm_g, w_in, hgrn_lb_logits, hgrn_norm_g, gdn_conv_w, gdn_A_log, gdn_dt_bias,
           gdn_norm_g, w_out, ffn_norm_g, peer_w_query, peer_sub_keys, peer_u, peer_v,
           final_norm_g):
    b, s, d = x.shape
    t = b * s
    depth = w_in.shape[0]
    in_width = w_in.shape[2]
    n_small = 2 * B_V_HEADS
    n_main = in_width - n_small
    in_pad = n_main + HD
    lb_all = jnp.cumsum(jax.nn.softmax(hgrn_lb_logits.astype(F32), axis=0), axis=0)
    pos = jnp.array([i * P_TOPK + j for i, j in _PAIRS]
                    + [10 ** 6 + c for c in range(_NCAND - len(_PAIRS))], F32)
    pos = jnp.broadcast_to(pos[:, None], (_NCAND, HD))

    tiles = _plan_tiles(t, s, in_pad, peer_u.shape[1])

    h2 = x.reshape(t, d)
    for l in range(depth):
        w_l = jnp.pad(w_in[l], ((0, 0), (0, in_pad - in_width))).astype(BF16)
        proj = _in_proj(h2, attn_norm_g[l][None, :], w_l, tiles.in_proj_rows, tiles.in_proj_cols)
        proj3 = proj.reshape(b, s, in_pad)

        o_a = _hgrn(proj3, lb_all[l], hgrn_norm_g[l][None, :], tiles.seq)

        small = proj3[:, :, n_main:n_main + n_small]
        smt = small.reshape(b, s // CHUNK, CHUNK, n_small).transpose(0, 1, 3, 2)
        zeros8 = jnp.zeros((B_V_HEADS,), F32)
        a_neg = jnp.exp(gdn_A_log[l].astype(F32))
        dtb = gdn_dt_bias[l].astype(F32)
        pcol = jnp.zeros((2, HD), F32)
        pcol = pcol.at[0, B_V_HEADS:n_small].set(a_neg).at[1, B_V_HEADS:n_small].set(dtb)
        prow = jnp.zeros((n_small, HD), F32)
        prow = prow.at[:, 0].set(jnp.concatenate([zeros8, a_neg]))
        prow = prow.at[:, 1].set(jnp.concatenate([zeros8, dtb]))
        o_b = _gdn(proj3, smt, gdn_conv_w[l], pcol, prow, gdn_norm_g[l][None, :], tiles.seq)

        wq = peer_w_query[l].reshape(d, P_HEADS * 2 * HD).astype(BF16)
        h2, hn, s1, s2 = _out_proj(
            h2, o_a.reshape(t, -1), o_b.reshape(t, -1), w_out[l].astype(BF16),
            ffn_norm_g[l][None, :], wq, peer_sub_keys[l, 0], peer_sub_keys[l, 1],
            tiles.out_proj_rows)

        e_t, g_t = _peer_topk(s1, s2, pos, tiles.topk_tokens)
        w_tab = _peer_pack(peer_u[l], peer_v[l], tiles.pack_experts)
        h2 = _peer_mix(e_t.T, g_t, hn, h2, final_norm_g[None, :], w_tab, l == depth - 1)
    return h2.reshape(b, s, d)
```

```python
import functools
from typing import NamedTuple

import jax
import jax.numpy as jnp
from jax import lax
from jax.experimental import pallas as pl
from jax.experimental.pallas import tpu as pltpu

F32 = jnp.float32
BF16 = jnp.bfloat16

EPS = 1e-6
CHUNK = 64
SUB = 16
PREP_CHUNKS = 4
HD = 128
A_HEADS = 8
B_QK_HEADS = 4
B_V_HEADS = 8
CONV = 4
P_HEADS = 8
N_KEYS = 128
P_TOPK = 16
SUBLANES = 8
VMEM_LIMIT = 56 * 1024 * 1024


def _nt(a, b):
    return lax.dot_general(a, b, (((1,), (1,)), ((), ())), preferred_element_type=F32)


def _tn(a, b):
    return lax.dot_general(a, b, (((0,), (0,)), ((), ())), preferred_element_type=F32)


def _mm(a, b):
    return jnp.dot(a, b, preferred_element_type=F32)


def _split_bf16(x, n):
    parts = []
    for _ in range(n - 1):
        p = x.astype(BF16)
        parts.append(p)
        x = x - p.astype(F32)
    parts.append(x.astype(BF16))
    return parts


def _mm3(a, b):
    ah, al = _split_bf16(a, 2)
    bh, bl = _split_bf16(b, 2)
    return _mm(ah, bh) + (_mm(ah, bl) + _mm(al, bh))


def _cumsum_mm(tri, g, g_is_lhs):
    t16 = tri.astype(BF16)
    out = None
    for p in reversed(_split_bf16(g, 3)):
        term = _mm(p, t16) if g_is_lhs else _mm(t16, p)
        out = term if out is None else out + term
    return out


def _sigmoid(x):
    return 1.0 / (1.0 + jnp.exp(-x))


def _softplus(x):
    return jnp.maximum(x, 0.0) + jnp.log1p(jnp.exp(-jnp.abs(x)))


def _bcast_rows(x, idx):
    n = x.shape[1]
    return jnp.concatenate(
        [jnp.broadcast_to(x[r:r + 1, :], (SUB, n)) for r in idx], axis=0)


def _inproj_body(x_ref, g_ref, w_ref, o_ref):
    x = x_ref[...]
    ms = jnp.mean(x * x, axis=-1, keepdims=True)
    xn = (x * lax.rsqrt(ms + EPS) * g_ref[...]).astype(BF16)
    o_ref[...] = _mm(xn, w_ref[...])


def _in_proj(x2, g, w, tm, tn):
    t, d = x2.shape
    n = w.shape[1]
    return pl.pallas_call(
        _inproj_body,
        grid=(n // tn, t // tm),
        in_specs=[
            pl.BlockSpec((tm, d), lambda j, i: (i, 0)),
            pl.BlockSpec((1, d), lambda j, i: (0, 0)),
            pl.BlockSpec((d, tn), lambda j, i: (0, j)),
        ],
        out_specs=pl.BlockSpec((tm, tn), lambda j, i: (i, j)),
        out_shape=jax.ShapeDtypeStruct((t, n), F32),
        compiler_params=pltpu.CompilerParams(
            dimension_semantics=("parallel", "parallel"),
            vmem_limit_bytes=VMEM_LIMIT),
        name="in_proj",
    )(x2, g, w)


def _hgrn_body(q_ref, f_ref, i_ref, g_ref, lb_ref, gn_ref, o_ref, st_ref, *, ts):
    @pl.when(pl.program_id(1) == 0)
    def _():
        st_ref[...] = jnp.zeros_like(st_ref)

    row = lax.broadcasted_iota(jnp.int32, (CHUNK, CHUNK), 0)
    col = lax.broadcasted_iota(jnp.int32, (CHUNK, CHUNK), 1)
    tril_f = (col <= row).astype(F32)
    blk_r = row // SUB
    blk_c = col // SUB
    rblk = lax.broadcasted_iota(jnp.int32, (CHUNK, HD), 0) // SUB
    nsub = CHUNK // SUB
    gn = gn_ref[...]

    def chunk_body(c, carry):
        r0 = pl.multiple_of(c * CHUNK, CHUNK)
        rows = pl.ds(r0, CHUNK)
        hs = range(A_HEADS)
        cols = [slice(h * HD, (h + 1) * HD) for h in hs]
        q = [q_ref[0, rows, cols[h]] for h in hs]
        z = [f_ref[0, rows, cols[h]] for h in hs]
        v16 = [i_ref[0, rows, cols[h]].astype(BF16) for h in hs]
        lb = [lb_ref[h:h + 1, :] for h in hs]
        sts = [st_ref[h] for h in hs]
        k = [(1.0 - lb[h]) * _sigmoid(-z[h]) for h in hs]
        b = [_cumsum_mm(tril_f, jnp.log(lb[h] + (1.0 - lb[h]) * _sigmoid(z[h])), False)
             for h in hs]
        o = [_nt((q[h] * jnp.exp(b[h])).astype(BF16), sts[h].astype(BF16)) for h in hs]
        att = []
        for h in hs:
            bnd = [b[h][SUB * j + SUB - 1:SUB * j + SUB, :] for j in range(nsub)]
            k_off = k[h] * jnp.exp(
                _bcast_rows(b[h], [SUB * j + SUB - 1 for j in range(nsub)]) - b[h])
            qcat = jnp.concatenate(
                [q[h] * jnp.exp(jnp.minimum(b[h] - bnd[j], 0.0)) for j in range(nsub - 1)],
                axis=1)
            kcat = jnp.concatenate(
                [jnp.where(rblk == j, k_off, 0.0) for j in range(nsub - 1)], axis=1)
            att_off = _nt(qcat.astype(BF16), kcat.astype(BF16))
            ref_rows = _bcast_rows(b[h], [SUB * i for i in range(nsub)])
            q_d = q[h] * jnp.exp(b[h] - ref_rows)
            k_d = k[h] * jnp.exp(ref_rows - b[h])
            att_d = _nt(q_d.astype(BF16), k_d.astype(BF16))
            att.append(jnp.where(blk_r > blk_c, att_off,
                                 jnp.where((blk_r == blk_c) & (col <= row), att_d, 0.0)))
        o = [o[h] + _mm(att[h].astype(BF16), v16[h]) for h in hs]
        new_sts = []
        for h in hs:
            b_last = b[h][CHUNK - 1:CHUNK, :]
            kdec = k[h] * jnp.exp(b_last - b[h])
            new_sts.append(sts[h] * jnp.exp(b_last) + _tn(v16[h], kdec.astype(BF16)))
        for h in hs:
            st_ref[h] = new_sts[h]
        for h in hs:
            gate = g_ref[0, rows, cols[h]]
            y = o[h] * lax.rsqrt(jnp.mean(o[h] * o[h], axis=-1, keepdims=True) + EPS)
            o_ref[0, rows, cols[h]] = (y * gn * (gate * _sigmoid(gate))).astype(o_ref.dtype)
        return carry

    lax.fori_loop(0, ts // CHUNK, chunk_body, 0)


def _hgrn(proj3, lb, gn, ts):
    b, s, _ = proj3.shape
    w = A_HEADS * HD
    return pl.pallas_call(
        functools.partial(_hgrn_body, ts=ts),
        grid=(b, s // ts),
        in_specs=[
            pl.BlockSpec((1, ts, w), lambda bi, ti: (bi, ti, 0)),
            pl.BlockSpec((1, ts, w), lambda bi, ti: (bi, ti, 1)),
            pl.BlockSpec((1, ts, w), lambda bi, ti: (bi, ti, 2)),
            pl.BlockSpec((1, ts, w), lambda bi, ti: (bi, ti, 3)),
            pl.BlockSpec((A_HEADS, HD), lambda bi, ti: (0, 0)),
            pl.BlockSpec((1, HD), lambda bi, ti: (0, 0)),
        ],
        out_specs=pl.BlockSpec((1, ts, w), lambda bi, ti: (bi, ti, 0)),
        out_shape=jax.ShapeDtypeStruct((b, s, w), BF16),
        scratch_shapes=[pltpu.VMEM((A_HEADS, HD, HD), F32)],
        compiler_params=pltpu.CompilerParams(
            dimension_semantics=("parallel", "arbitrary"),
            vmem_limit_bytes=VMEM_LIMIT),
        name="hgrn",
    )(proj3, proj3, proj3, proj3, lb, gn)


def _gdn_body(bq_ref, bk_ref, bv_ref, bz_ref, sm_ref, smt_ref, cw_ref, pcol_ref, prow_ref,
              gn_ref, o_ref, s_ref, tail_ref, qn_ref, kn_ref, vv_ref,
              u_s, w_s, qg_s, kd_s, aqk_s, dl_s, *, ts):
    nqk = B_QK_HEADS * HD

    @pl.when(pl.program_id(1) == 0)
    def _():
        s_ref[...] = jnp.zeros_like(s_ref)
        tail_ref[...] = jnp.zeros_like(tail_ref)

    def conv_silu(x_ref, h, c0):
        cs = slice(h * HD, (h + 1) * HD)
        cc = slice(c0 + h * HD, c0 + (h + 1) * HD)
        x = x_ref[0, :, cs]
        xe = jnp.concatenate([tail_ref[:, cc], x], axis=0)
        w = cw_ref[:, cc]
        y = x * w[CONV - 1:CONV, :]
        for j in range(1, CONV):
            y = y + pltpu.roll(xe, j, axis=0)[SUBLANES:, :] * w[CONV - 1 - j:CONV - j, :]
        tail_ref[:, cc] = x[ts - SUBLANES:, :]
        return y * _sigmoid(y)

    def l2n(x):
        return x * lax.rsqrt(jnp.sum(x * x, axis=-1, keepdims=True) + EPS)

    for h in range(B_QK_HEADS):
        cs = slice(h * HD, (h + 1) * HD)
        qn_ref[:, cs] = l2n(conv_silu(bq_ref, h, 0)) * (HD ** -0.5)
        kn_ref[:, cs] = l2n(conv_silu(bk_ref, h, nqk))
    for h in range(B_V_HEADS):
        vv_ref[:, h * HD:(h + 1) * HD] = conv_silu(bv_ref, h, 2 * nqk)

    row = lax.broadcasted_iota(jnp.int32, (CHUNK, CHUNK), 0)
    col = lax.broadcasted_iota(jnp.int32, (CHUNK, CHUNK), 1)
    incl = col <= row
    strict = col < row
    tril_f = incl.astype(F32)
    triu_f = (row <= col).astype(F32)
    same_blk = (row // SUB) == (col // SUB)
    eye = (row == col).astype(F32)
    gn = gn_ref[...]
    a_col = pcol_ref[0:1, :]
    dt_col = pcol_ref[1:2, :]
    a_row = prow_ref[:, 0:1]
    dt_row = prow_ref[:, 1:2]
    rep = B_V_HEADS // B_QK_HEADS

    def prep_chunks(trip, carry):
        cs_ = [trip * PREP_CHUNKS + i for i in range(PREP_CHUNKS)]
        rows_ = [pl.ds(pl.multiple_of(c * CHUNK, CHUNK), CHUNK) for c in cs_]
        beta_c, gc_c, gc_r = [], [], []
        for c, rows in zip(cs_, rows_):
            sm = sm_ref[0, rows, :]
            beta_c.append(_sigmoid(sm))
            gc_c.append(_cumsum_mm(tril_f, -a_col * _softplus(sm + dt_col), False))
            smt = smt_ref[0, c]
            gc_r.append(_cumsum_mm(triu_f, -a_row * _softplus(smt + dt_row), True))
        items = [(i, h) for i in range(PREP_CHUNKS) for h in range(B_V_HEADS)]
        ns = range(len(items))
        qq = [qn_ref[rows_[i], (h // rep) * HD:(h // rep + 1) * HD] for i, h in items]
        kk = [kn_ref[rows_[i], (h // rep) * HD:(h // rep + 1) * HD] for i, h in items]
        kk16 = [k.astype(BF16) for k in kk]
        bcol = [beta_c[i][:, h:h + 1] for i, h in items]
        gcol = [gc_c[i][:, B_V_HEADS + h:B_V_HEADS + h + 1] for i, h in items]
        decay = [jnp.where(incl, jnp.exp(jnp.minimum(
            gcol[n] - gc_r[i][B_V_HEADS + h:B_V_HEADS + h + 1, :], 0.0)), 0.0)
            for n, (i, h) in enumerate(items)]
        kb = [kk[n] * bcol[n] for n in ns]
        a = [jnp.where(strict, _nt(kb[n].astype(BF16), kk16[n]) * decay[n], 0.0) for n in ns]
        dg = [jnp.where(same_blk, a[n], 0.0) for n in ns]
        off = [a[n] - dg[n] for n in ns]
        tinv = [eye - dg[n] for n in ns]
        p = dg
        for _ in range(3):
            p = [_mm3(p[n], p[n]) for n in ns]
            tinv = [tinv[n] + _mm3(tinv[n], p[n]) for n in ns]
        pm = [_mm3(tinv[n], off[n]) for n in ns]
        pm2 = [_mm3(pm[n], pm[n]) for n in ns]
        egc = [jnp.exp(gcol[n]) for n in ns]
        x = [_mm(tinv[n].astype(BF16), jnp.concatenate(
            [vv_ref[rows_[i], h * HD:(h + 1) * HD] * bcol[n], kb[n] * egc[n]],
            axis=1).astype(BF16)) for n, (i, h) in enumerate(items)]
        x = [x[n] + _mm(pm2[n].astype(BF16), x[n].astype(BF16)) for n in ns]
        x = [x[n] - _mm(pm[n].astype(BF16), x[n].astype(BF16)) for n in ns]
        aqk = [jnp.where(incl, _nt(qq[n].astype(BF16), kk16[n]) * decay[n], 0.0) for n in ns]
        for n, (i, h) in enumerate(items):
            cs = slice(h * HD, (h + 1) * HD)
            g_last = gcol[n][CHUNK - 1:CHUNK, :]
            u_s[rows_[i], cs] = x[n][:, :HD]
            w_s[rows_[i], cs] = x[n][:, HD:].astype(BF16)
            qg_s[rows_[i], cs] = (qq[n] * egc[n]).astype(BF16)
            kd_s[rows_[i], cs] = (kk[n] * jnp.exp(g_last - gcol[n])).astype(BF16)
            aqk_s[cs_[i], h] = aqk[n].astype(BF16)
            dl_s[cs_[i], h:h + 1, :] = jnp.broadcast_to(jnp.exp(g_last), (1, HD))
        return carry

    lax.fori_loop(0, ts // CHUNK // PREP_CHUNKS, prep_chunks, 0)

    def scan_chunk(c, carry):
        r0 = pl.multiple_of(c * CHUNK, CHUNK)
        rows = pl.ds(r0, CHUNK)
        hs = range(B_V_HEADS)
        cols = [slice(h * HD, (h + 1) * HD) for h in hs]
        sts = [s_ref[h] for h in hs]
        st16 = [s.astype(BF16) for s in sts]
        v16 = [(u_s[rows, cols[h]] - _mm(w_s[rows, cols[h]], st16[h])).astype(BF16) for h in hs]
        o = [_mm(qg_s[rows, cols[h]], st16[h]) + _mm(aqk_s[c, h], v16[h]) for h in hs]
        new_sts = [sts[h] * dl_s[c, h:h + 1, :] + _tn(kd_s[rows, cols[h]], v16[h]) for h in hs]
        for h in hs:
            s_ref[h] = new_sts[h]
        for h in hs:
            zz = bz_ref[0, rows, cols[h]]
            y = o[h] * lax.rsqrt(jnp.mean(o[h] * o[h], axis=-1, keepdims=True) + EPS)
            o_ref[0, rows, cols[h]] = (y * gn * (zz * _sigmoid(zz))).astype(o_ref.dtype)
        return carry

    lax.fori_loop(0, ts // CHUNK, scan_chunk, 0)


def _gdn(proj3, smt, conv_w, pcol, prow, gn, ts):
    b, s, _ = proj3.shape
    nqk = B_QK_HEADS * HD
    wv = B_V_HEADS * HD
    off_q = (4 * A_HEADS * HD) // nqk
    off_v = (4 * A_HEADS * HD + 2 * nqk) // wv
    off_s = (4 * A_HEADS * HD + 2 * nqk + 2 * wv) // HD
    nch = ts // CHUNK
    return pl.pallas_call(
        functools.partial(_gdn_body, ts=ts),
        grid=(b, s // ts),
        in_specs=[
            pl.BlockSpec((1, ts, nqk), lambda bi, ti: (bi, ti, off_q)),
            pl.BlockSpec((1, ts, nqk), lambda bi, ti: (bi, ti, off_q + 1)),
            pl.BlockSpec((1, ts, wv), lambda bi, ti: (bi, ti, off_v)),
            pl.BlockSpec((1, ts, wv), lambda bi, ti: (bi, ti, off_v + 1)),
            pl.BlockSpec((1, ts, HD), lambda bi, ti: (bi, ti, off_s)),
            pl.BlockSpec((1, nch, 2 * B_V_HEADS, CHUNK), lambda bi, ti: (bi, ti, 0, 0)),
            pl.BlockSpec((CONV, 2 * nqk + wv), lambda bi, ti: (0, 0)),
            pl.BlockSpec((2, HD), lambda bi, ti: (0, 0)),
            pl.BlockSpec((2 * B_V_HEADS, HD), lambda bi, ti: (0, 0)),
            pl.BlockSpec((1, HD), lambda bi, ti: (0, 0)),
        ],
        out_specs=pl.BlockSpec((1, ts, wv), lambda bi, ti: (bi, ti, 0)),
        out_shape=jax.ShapeDtypeStruct((b, s, wv), BF16),
        scratch_shapes=[
            pltpu.VMEM((B_V_HEADS, HD, HD), F32),
            pltpu.VMEM((SUBLANES, 2 * nqk + wv), F32),
            pltpu.VMEM((ts, nqk), F32),
            pltpu.VMEM((ts, nqk), F32),
            pltpu.VMEM((ts, wv), F32),
            pltpu.VMEM((ts, wv), F32),
            pltpu.VMEM((ts, wv), BF16),
            pltpu.VMEM((ts, wv), BF16),
            pltpu.VMEM((ts, wv), BF16),
            pltpu.VMEM((nch, B_V_HEADS, CHUNK, CHUNK), BF16),
            pltpu.VMEM((nch, B_V_HEADS, HD), F32),
        ],
        compiler_params=pltpu.CompilerParams(
            dimension_semantics=("parallel", "arbitrary"),
            vmem_limit_bytes=VMEM_LIMIT),
        name="gdn",
    )(proj3, proj3, proj3, proj3, proj3, smt, conv_w, pcol, prow, gn)


def _outproj_body(x_ref, oa_ref, ob_ref, wo_ref, fg_ref, wq_ref, k1_ref, k2_ref,
                  h_ref, hn_ref, s1_ref, s2_ref):
    wa = oa_ref.shape[1]
    h = x_ref[...] + _mm(oa_ref[...], wo_ref[:wa, :]) + _mm(ob_ref[...], wo_ref[wa:, :])
    h_ref[...] = h
    hn = h * lax.rsqrt(jnp.mean(h * h, axis=-1, keepdims=True) + EPS) * fg_ref[...]
    hn_ref[...] = hn
    qry = _mm(hn.astype(BF16), wq_ref[...])
    for hh in range(P_HEADS):
        q1 = qry[:, hh * 2 * HD:hh * 2 * HD + HD].astype(BF16)
        q2 = qry[:, hh * 2 * HD + HD:(hh + 1) * 2 * HD].astype(BF16)
        s1_ref[hh] = _nt(k1_ref[hh].astype(BF16), q1)
        s2_ref[hh] = _nt(k2_ref[hh].astype(BF16), q2)


def _out_proj(x2, oa, ob, wo, fg, wq, k1, k2, tm):
    t, d = x2.shape
    wa = oa.shape[1]
    wb = ob.shape[1]
    const = dict(pipeline_mode=pl.Buffered(1))
    return pl.pallas_call(
        _outproj_body,
        grid=(t // tm,),
        in_specs=[
            pl.BlockSpec((tm, d), lambda i: (i, 0)),
            pl.BlockSpec((tm, wa), lambda i: (i, 0)),
            pl.BlockSpec((tm, wb), lambda i: (i, 0)),
            pl.BlockSpec((wa + wb, d), lambda i: (0, 0), **const),
            pl.BlockSpec((1, d), lambda i: (0, 0)),
            pl.BlockSpec((d, P_HEADS * 2 * HD), lambda i: (0, 0), **const),
            pl.BlockSpec((P_HEADS, N_KEYS, HD), lambda i: (0, 0, 0)),
            pl.BlockSpec((P_HEADS, N_KEYS, HD), lambda i: (0, 0, 0)),
        ],
        out_specs=[
            pl.BlockSpec((tm, d), lambda i: (i, 0)),
            pl.BlockSpec((tm, d), lambda i: (i, 0)),
            pl.BlockSpec((P_HEADS, N_KEYS, tm), lambda i: (0, 0, i)),
            pl.BlockSpec((P_HEADS, N_KEYS, tm), lambda i: (0, 0, i)),
        ],
        out_shape=[
            jax.ShapeDtypeStruct((t, d), F32),
            jax.ShapeDtypeStruct((t, d), F32),
            jax.ShapeDtypeStruct((P_HEADS, N_KEYS, t), F32),
            jax.ShapeDtypeStruct((P_HEADS, N_KEYS, t), F32),
        ],
        compiler_params=pltpu.CompilerParams(
            dimension_semantics=("parallel",),
            vmem_limit_bytes=VMEM_LIMIT),
        name="out_proj",
    )(x2, oa, ob, wo, fg, wq, k1, k2)


_PAIRS = [(i, j) for i in range(P_TOPK) for j in range(P_TOPK) if (i + 1) * (j + 1) <= P_TOPK]
_NCAND = -(-len(_PAIRS) // 8) * 8


def _topk_body(s1_ref, s2_ref, pos_ref, e_ref, g_ref, v_scr, i_scr, c_scr, ce_scr, b_scr, x_scr):
    tt = s1_ref.shape[2]
    kio = lax.broadcasted_iota(jnp.int32, (N_KEYS, tt), 0).astype(F32)
    neg = -jnp.inf
    for half, sref in ((0, s1_ref), (1, s2_ref)):
        s = sref[0]
        for r in range(P_TOPK):
            m = jnp.max(s, axis=0, keepdims=True)
            idx = jnp.min(jnp.where(s == m, kio, float(N_KEYS)), axis=0, keepdims=True)
            v_scr[half, r:r + 1, :] = m
            i_scr[half, r:r + 1, :] = idx
            s = jnp.where(kio == idx, neg, s)
    c_scr[...] = jnp.full(c_scr.shape, neg, F32)
    ce_scr[...] = jnp.zeros(ce_scr.shape, F32)
    for c, (i, j) in enumerate(_PAIRS):
        c_scr[c:c + 1, :] = v_scr[0, i:i + 1, :] + v_scr[1, j:j + 1, :]
        ce_scr[c:c + 1, :] = i_scr[0, i:i + 1, :] * float(N_KEYS) + i_scr[1, j:j + 1, :]
    cand = c_scr[...]
    ce = ce_scr[...]
    pos = jnp.broadcast_to(pos_ref[:, 0:1], cand.shape)
    for r in range(P_TOPK):
        m = jnp.max(cand, axis=0, keepdims=True)
        sel = jnp.min(jnp.where(cand == m, pos, 1e9), axis=0, keepdims=True)
        hit = pos == sel
        b_scr[r:r + 1, :] = m
        x_scr[r:r + 1, :] = jnp.max(jnp.where(hit, ce, -1.0), axis=0, keepdims=True)
        cand = jnp.where(hit, neg, cand)
    best = b_scr[...]
    ex = jnp.exp(best - best[0:1, :])
    g_ref[...] = ex / jnp.sum(ex, axis=0, keepdims=True)
    e_ref[...] = x_scr[...].astype(jnp.int32)


def _peer_topk(s1, s2, pos, tt):
    t = s1.shape[2]
    return pl.pallas_call(
        _topk_body,
        grid=(t // tt, P_HEADS),
        in_specs=[
            pl.BlockSpec((1, N_KEYS, tt), lambda i, h: (h, 0, i)),
            pl.BlockSpec((1, N_KEYS, tt), lambda i, h: (h, 0, i)),
            pl.BlockSpec((_NCAND, HD), lambda i, h: (0, 0)),
        ],
        out_specs=[
            pl.BlockSpec((P_TOPK, tt), lambda i, h: (h, i)),
            pl.BlockSpec((P_TOPK, tt), lambda i, h: (h, i)),
        ],
        out_shape=[
            jax.ShapeDtypeStruct((P_HEADS * P_TOPK, t), jnp.int32),
            jax.ShapeDtypeStruct((P_HEADS * P_TOPK, t), F32),
        ],
        scratch_shapes=[
            pltpu.VMEM((2, P_TOPK, tt), F32),
            pltpu.VMEM((2, P_TOPK, tt), F32),
            pltpu.VMEM((_NCAND, tt), F32),
            pltpu.VMEM((_NCAND, tt), F32),
            pltpu.VMEM((P_TOPK, tt), F32),
            pltpu.VMEM((P_TOPK, tt), F32),
        ],
        compiler_params=pltpu.CompilerParams(
            dimension_semantics=("parallel", "parallel"),
            vmem_limit_bytes=VMEM_LIMIT),
        name="peer_topk",
    )(s1, s2, pos)


PEER_TB = 128
PEER_G = 8
PEER_SETS = 4
DMA_QUEUES = 2
NSLOT = P_HEADS * P_TOPK
SG = SUBLANES
CC_AHEAD = 3
ROW_PITCH = 17
HID_UNROLL = 1
MIX_UNROLL = 2


def _gelu(x):
    return 0.5 * x * (1.0 + lax.erf(x * (2.0 ** -0.5)))


def _unrolled_loop(n, unroll, body):
    def trip(q, carry):
        for j in range(unroll):
            body(q * unroll + j, 0)
        return carry
    lax.fori_loop(0, n // unroll, trip, 0)


def _pack_body(u_ref, v_ref, o_ref):
    ub = lax.bitcast_convert_type(u_ref[...].astype(BF16).astype(F32), jnp.uint32) >> 16
    vb = lax.bitcast_convert_type(v_ref[...].astype(BF16).astype(F32), jnp.uint32)
    w = (vb & jnp.uint32(0xFFFF0000)) | ub
    nlc = w.shape[1] // HD
    for c in range(nlc):
        o_ref[pl.ds(c, w.shape[0], stride=nlc), :] = w[:, c * HD:(c + 1) * HD]


def _peer_pack(u_tab, v_tab, te):
    e, d = u_tab.shape
    return pl.pallas_call(
        _pack_body,
        grid=(e // te,),
        in_specs=[pl.BlockSpec((te, d), lambda i: (i, 0)),
                  pl.BlockSpec((te, d), lambda i: (i, 0))],
        out_specs=pl.BlockSpec((te * (d // HD), HD), lambda i: (i, 0)),
        out_shape=jax.ShapeDtypeStruct((e * (d // HD), HD), jnp.uint32),
        compiler_params=pltpu.CompilerParams(
            dimension_semantics=("parallel",),
            vmem_limit_bytes=VMEM_LIMIT),
        name="peer_pack",
    )(u_tab, v_tab)


def _peer_body(idx_ref, gt_ref, hn_ref, h_ref, fg_ref, w_hbm, o_ref,
               hid_scr, c_scr, y_scr, sem, *bufs, final, nsteps):
    tb, d = hn_ref.shape
    ngroups = tb // PEER_G
    nsg = NSLOT // SG
    ahead = PEER_SETS - 1
    nlc = d // HD
    step = pl.program_id(0)

    def issue_tok(tok, set_, r, s0, s1):
        ids = idx_ref.at[tok]
        dst = bufs[set_].at[pl.ds(r * NSLOT * ROW_PITCH, NSLOT * ROW_PITCH)]
        for s in range(s0, s1):
            e0 = pl.multiple_of(ids[s] * nlc, nlc)
            pltpu.make_async_copy(w_hbm.at[pl.ds(e0, nlc)], dst.at[pl.ds(s * ROW_PITCH, nlc)],
                                  sem.at[set_]).start(priority=s % DMA_QUEUES)

    def wait_set(set_):
        full = bufs[set_].at[pl.ds(0, PEER_G * NSLOT * nlc)]
        pltpu.make_async_copy(full, full, sem.at[set_]).wait()

    @pl.when(step == 0)
    def _():
        def first(r, carry):
            for a in range(ahead):
                issue_tok(a * PEER_G + r, a, r, 0, NSLOT)
            return carry
        lax.fori_loop(0, PEER_G, first, 0)

    per_tile = NSLOT // 2 // nsg
    lane = lax.broadcasted_iota(jnp.int32, (SG, tb), 1)
    hid_scr[...] = jnp.zeros_like(hid_scr)

    def group_body(g, set_):
        nxt = (set_ + ahead) % PEER_SETS
        wait_set(set_)


        def hid_tok(r, c2):
            t = g * PEER_G + r
            xrow = hn_ref[pl.ds(t, 1), :]
            xb = [jnp.broadcast_to(xrow[:, lc * HD:(lc + 1) * HD], (SG, HD)) for lc in range(nlc)]
            tiles = bufs[set_].at[pl.ds(r * NSLOT * ROW_PITCH, NSLOT * ROW_PITCH)]
            hs = []
            for sg in range(nsg):
                issue_tok((g + ahead) * PEER_G + r, nxt, r, sg * per_tile, (sg + 1) * per_tile)
                acc = None
                for lc in range(nlc):
                    w = tiles[pl.ds(sg * SG * ROW_PITCH + lc, SG, stride=ROW_PITCH), :]
                    u = lax.bitcast_convert_type(w << 16, F32)
                    term = u * xb[lc]
                    acc = term if acc is None else acc + term
                hs.append(jnp.sum(acc, axis=1, keepdims=True))
            for sg in range(nsg):
                pltpu.store(hid_scr.at[pl.ds(sg * SG, SG), :],
                            jnp.broadcast_to(hs[sg], (SG, tb)), mask=lane == t)
            return c2
        _unrolled_loop(PEER_G, HID_UNROLL, hid_tok)

        c_scr[...] = gt_ref[...] * _gelu(hid_scr[...])

        def mix_tok(r, c2):
            t = g * PEER_G + r
            tiles = bufs[set_].at[pl.ds(r * NSLOT * ROW_PITCH, NSLOT * ROW_PITCH)]
            def coef_col(sg):
                return jnp.sum(jnp.where(lane == t, c_scr[pl.ds(sg * SG, SG), :], 0.0),
                               axis=1, keepdims=True)
            cc = {sg: coef_col(sg) for sg in range(CC_AHEAD)}
            accs = [None] * nlc
            for sg in range(nsg):
                issue_tok((g + ahead) * PEER_G + r, nxt, r,
                          NSLOT // 2 + sg * per_tile, NSLOT // 2 + (sg + 1) * per_tile)
                if sg + CC_AHEAD < nsg:
                    cc[sg + CC_AHEAD] = coef_col(sg + CC_AHEAD)
                for lc in range(nlc):
                    w = tiles[pl.ds(sg * SG * ROW_PITCH + lc, SG, stride=ROW_PITCH), :]
                    v = lax.bitcast_convert_type(w & jnp.uint32(0xFFFF0000), F32)
                    term = v * cc[sg]
                    accs[lc] = term if accs[lc] is None else accs[lc] + term
            yrow = jnp.concatenate(
                [jnp.sum(a, axis=0, keepdims=True) for a in accs], axis=1)
            y_scr[pl.ds(t, 1), :] = yrow
            return c2
        _unrolled_loop(PEER_G, MIX_UNROLL, mix_tok)

    def group_round(q, carry):
        for j in range(PEER_SETS):
            group_body(q * PEER_SETS + j, j)
        return carry

    lax.fori_loop(0, ngroups // PEER_SETS, group_round, 0)

    @pl.when(step == nsteps - 1)
    def _():
        for a in range(ahead):
            wait_set((nsteps * ngroups + a) % PEER_SETS)

    hh = h_ref[...] + y_scr[...]
    if final:
        hh = hh * lax.rsqrt(jnp.mean(hh * hh, axis=-1, keepdims=True) + EPS) * fg_ref[...]
    o_ref[...] = hh


def _peer_mix(idx, gt, hn, h, fg, w_tab, final):
    t, d = hn.shape
    tb = PEER_TB
    nsteps = t // tb
    extra = (PEER_SETS - 1) * PEER_G
    assert (tb // PEER_G) % PEER_SETS == 0
    idx_pad = jnp.concatenate([idx, jnp.zeros((tb, NSLOT), idx.dtype)], axis=0)
    idx_ext = jnp.concatenate(
        [idx.reshape(nsteps, tb, NSLOT),
         idx_pad[tb:].reshape(nsteps, tb, NSLOT)[:, :extra]], axis=1)
    return pl.pallas_call(
        functools.partial(_peer_body, final=final, nsteps=nsteps),
        grid=(nsteps,),
        in_specs=[
            pl.BlockSpec((None, tb + extra, NSLOT), lambda i: (i, 0, 0),
                         memory_space=pltpu.SMEM),
            pl.BlockSpec((NSLOT, tb), lambda i: (0, i)),
            pl.BlockSpec((tb, d), lambda i: (i, 0)),
            pl.BlockSpec((tb, d), lambda i: (i, 0)),
            pl.BlockSpec((1, d), lambda i: (0, 0)),
            pl.BlockSpec(memory_space=pl.ANY),
        ],
        out_specs=pl.BlockSpec((tb, d), lambda i: (i, 0)),
        out_shape=jax.ShapeDtypeStruct((t, d), F32),
        scratch_shapes=[
            pltpu.VMEM((NSLOT, tb), F32),
            pltpu.VMEM((NSLOT, tb), F32),
            pltpu.VMEM((tb, d), F32),
            pltpu.SemaphoreType.DMA((PEER_SETS,)),
        ] + [pltpu.VMEM((PEER_G * NSLOT * ROW_PITCH, HD), jnp.uint32) for _ in range(PEER_SETS)],
        compiler_params=pltpu.CompilerParams(
            dimension_semantics=("arbitrary",),
            vmem_limit_bytes=VMEM_LIMIT),
        name="peer_mix",
    )(idx_ext, gt, hn, h, fg, w_tab)


def _tile(n, pref):
    return pref if n % pref == 0 else n


class _Tiles(NamedTuple):
    in_proj_rows: int
    in_proj_cols: int
    seq: int
    out_proj_rows: int
    topk_tokens: int
    pack_experts: int


def _plan_tiles(t, s, in_pad, n_experts):
    third = in_pad // 3 if (in_pad // HD) % 3 == 0 else in_pad
    return _Tiles(in_proj_rows=_tile(t, 512), in_proj_cols=third, seq=_tile(s, 256),
                  out_proj_rows=_tile(t, 256), topk_tokens=_tile(t, 512),
                  pack_experts=_tile(n_experts, 256))


def kernel(x, attn_norm_g, w_in, hgrn_lb_logits, hgrn_norm_g, gdn_conv_w, gdn_A_log, gdn_dt_bias,
           gdn_norm_g, w_out, ffn_norm_g, peer_w_query, peer_sub_keys, peer_u, peer_v,
           final_norm_g):
    b, s, d = x.shape
    t = b * s
    depth = w_in.shape[0]
    in_width = w_in.shape[2]
    n_small = 2 * B_V_HEADS
    n_main = in_width - n_small
    in_pad = n_main + HD
    lb_all = jnp.cumsum(jax.nn.softmax(hgrn_lb_logits.astype(F32), axis=0), axis=0)
    pos = jnp.array([i * P_TOPK + j for i, j in _PAIRS]
                    + [10 ** 6 + c for c in range(_NCAND - len(_PAIRS))], F32)
    pos = jnp.broadcast_to(pos[:, None], (_NCAND, HD))

    tiles = _plan_tiles(t, s, in_pad, peer_u.shape[1])

    h2 = x.reshape(t, d)
    for l in range(depth):
        w_l = jnp.pad(w_in[l], ((0, 0), (0, in_pad - in_width))).astype(BF16)
        proj = _in_proj(h2, attn_norm_g[l][None, :], w_l, tiles.in_proj_rows, tiles.in_proj_cols)
        proj3 = proj.reshape(b, s, in_pad)

        o_a = _hgrn(proj3, lb_all[l], hgrn_norm_g[l][None, :], tiles.seq)

        small = proj3[:, :, n_main:n_main + n_small]
        smt = small.reshape(b, s // CHUNK, CHUNK, n_small).transpose(0, 1, 3, 2)
        zeros8 = jnp.zeros((B_V_HEADS,), F32)
        a_neg = jnp.exp(gdn_A_log[l].astype(F32))
        dtb = gdn_dt_bias[l].astype(F32)
        pcol = jnp.zeros((2, HD), F32)
        pcol = pcol.at[0, B_V_HEADS:n_small].set(a_neg).at[1, B_V_HEADS:n_small].set(dtb)
        prow = jnp.zeros((n_small, HD), F32)
        prow = prow.at[:, 0].set(jnp.concatenate([zeros8, a_neg]))
        prow = prow.at[:, 1].set(jnp.concatenate([zeros8, dtb]))
        o_b = _gdn(proj3, smt, gdn_conv_w[l], pcol, prow, gdn_norm_g[l][None, :], tiles.seq)

        wq = peer_w_query[l].reshape(d, P_HEADS * 2 * HD).astype(BF16)
        h2, hn, s1, s2 = _out_proj(
            h2, o_a.reshape(t, -1), o_b.reshape(t, -1), w_out[l].astype(BF16),
            ffn_norm_g[l][None, :], wq, peer_sub_keys[l, 0], peer_sub_keys[l, 1],
            tiles.out_proj_rows)

        e_t, g_t = _peer_topk(s1, s2, pos, tiles.topk_tokens)
        w_tab = _peer_pack(peer_u[l], peer_v[l], tiles.pack_experts)
        h2 = _peer_mix(e_t.T, g_t, hn, h2, final_norm_g[None, :], w_tab, l == depth - 1)
    return h2.reshape(b, s, d)
```

```python
import functools
from typing import NamedTuple

import jax
import jax.numpy as jnp
from jax import lax
from jax.experimental import pallas as pl
from jax.experimental.pallas import tpu as pltpu

F32 = jnp.float32
BF16 = jnp.bfloat16

EPS = 1e-6
CHUNK = 64
SUB = 16
PREP_CHUNKS = 4
HD = 128
A_HEADS = 8
B_QK_HEADS = 4
B_V_HEADS = 8
CONV = 4
P_HEADS = 8
N_KEYS = 128
P_TOPK = 16
SUBLANES = 8
VMEM_LIMIT = 56 * 1024 * 1024


def _nt(a, b):
    return lax.dot_general(a, b, (((1,), (1,)), ((), ())), preferred_element_type=F32)


def _tn(a, b):
    return lax.dot_general(a, b, (((0,), (0,)), ((), ())), preferred_element_type=F32)


def _mm(a, b):
    return jnp.dot(a, b, preferred_element_type=F32)


def _split_bf16(x, n):
    parts = []
    for _ in range(n - 1):
        p = x.astype(BF16)
        parts.append(p)
        x = x - p.astype(F32)
    parts.append(x.astype(BF16))
    return parts


def _mm16(a, b):
    return _mm(a.astype(BF16), b.astype(BF16))


def _cumsum_mm(tri, g, g_is_lhs):
    t16 = tri.astype(BF16)
    out = None
    for p in reversed(_split_bf16(g, 3)):
        term = _mm(p, t16) if g_is_lhs else _mm(t16, p)
        out = term if out is None else out + term
    return out


def _sigmoid(x):
    return 1.0 / (1.0 + jnp.exp(-x))


def _softplus(x):
    return jnp.maximum(x, 0.0) + jnp.log1p(jnp.exp(-jnp.abs(x)))


def _bcast_rows(x, idx):
    n = x.shape[1]
    return jnp.concatenate(
        [jnp.broadcast_to(x[r:r + 1, :], (SUB, n)) for r in idx], axis=0)


def _inproj_body(x_ref, g_ref, w_ref, o_ref):
    x = x_ref[...]
    ms = jnp.mean(x * x, axis=-1, keepdims=True)
    xn = (x * lax.rsqrt(ms + EPS) * g_ref[...]).astype(BF16)
    o_ref[...] = _mm(xn, w_ref[...])


def _in_proj(x2, g, w, tm, tn):
    t, d = x2.shape
    n = w.shape[1]
    return pl.pallas_call(
        _inproj_body,
        grid=(n // tn, t // tm),
        in_specs=[
            pl.BlockSpec((tm, d), lambda j, i: (i, 0)),
            pl.BlockSpec((1, d), lambda j, i: (0, 0)),
            pl.BlockSpec((d, tn), lambda j, i: (0, j)),
        ],
        out_specs=pl.BlockSpec((tm, tn), lambda j, i: (i, j)),
        out_shape=jax.ShapeDtypeStruct((t, n), F32),
        compiler_params=pltpu.CompilerParams(
            dimension_semantics=("parallel", "parallel"),
            vmem_limit_bytes=VMEM_LIMIT),
        name="in_proj",
    )(x2, g, w)


def _hgrn_body(q_ref, f_ref, i_ref, g_ref, lb_ref, gn_ref, o_ref, st_ref, *, ts):
    @pl.when(pl.program_id(1) == 0)
    def _():
        st_ref[...] = jnp.zeros_like(st_ref)

    row = lax.broadcasted_iota(jnp.int32, (CHUNK, CHUNK), 0)
    col = lax.broadcasted_iota(jnp.int32, (CHUNK, CHUNK), 1)
    tril_f = (col <= row).astype(F32)
    blk_r = row // SUB
    blk_c = col // SUB
    rblk = lax.broadcasted_iota(jnp.int32, (CHUNK, HD), 0) // SUB
    nsub = CHUNK // SUB
    gn = gn_ref[...]

    def chunk_body(c, carry):
        r0 = pl.multiple_of(c * CHUNK, CHUNK)
        rows = pl.ds(r0, CHUNK)
        hs = range(A_HEADS)
        cols = [slice(h * HD, (h + 1) * HD) for h in hs]
        q = [q_ref[0, rows, cols[h]] for h in hs]
        z = [f_ref[0, rows, cols[h]] for h in hs]
        v16 = [i_ref[0, rows, cols[h]].astype(BF16) for h in hs]
        lb = [lb_ref[h:h + 1, :] for h in hs]
        sts = [st_ref[h] for h in hs]
        k = [(1.0 - lb[h]) * _sigmoid(-z[h]) for h in hs]
        b = [_cumsum_mm(tril_f, jnp.log(lb[h] + (1.0 - lb[h]) * _sigmoid(z[h])), False)
             for h in hs]
        o = [_nt((q[h] * jnp.exp(b[h])).astype(BF16), sts[h].astype(BF16)) for h in hs]
        att = []
        for h in hs:
            bnd = [b[h][SUB * j + SUB - 1:SUB * j + SUB, :] for j in range(nsub)]
            k_off = k[h] * jnp.exp(
                _bcast_rows(b[h], [SUB * j + SUB - 1 for j in range(nsub)]) - b[h])
            qcat = jnp.concatenate(
                [q[h] * jnp.exp(jnp.minimum(b[h] - bnd[j], 0.0)) for j in range(nsub - 1)],
                axis=1)
            kcat = jnp.concatenate(
                [jnp.where(rblk == j, k_off, 0.0) for j in range(nsub - 1)], axis=1)
            att_off = _nt(qcat.astype(BF16), kcat.astype(BF16))
            ref_rows = _bcast_rows(b[h], [SUB * i for i in range(nsub)])
            q_d = q[h] * jnp.exp(b[h] - ref_rows)
            k_d = k[h] * jnp.exp(ref_rows - b[h])
            att_d = _nt(q_d.astype(BF16), k_d.astype(BF16))
            att.append(jnp.where(blk_r > blk_c, att_off,
                                 jnp.where((blk_r == blk_c) & (col <= row), att_d, 0.0)))
        o = [o[h] + _mm(att[h].astype(BF16), v16[h]) for h in hs]
        new_sts = []
        for h in hs:
            b_last = b[h][CHUNK - 1:CHUNK, :]
            kdec = k[h] * jnp.exp(b_last - b[h])
            new_sts.append(sts[h] * jnp.exp(b_last) + _tn(v16[h], kdec.astype(BF16)))
        for h in hs:
            st_ref[h] = new_sts[h]
        for h in hs:
            gate = g_ref[0, rows, cols[h]]
            y = o[h] * lax.rsqrt(jnp.mean(o[h] * o[h], axis=-1, keepdims=True) + EPS)
            o_ref[0, rows, cols[h]] = (y * gn * (gate * _sigmoid(gate))).astype(o_ref.dtype)
        return carry

    lax.fori_loop(0, ts // CHUNK, chunk_body, 0)


def _hgrn(proj3, lb, gn, ts):
    b, s, _ = proj3.shape
    w = A_HEADS * HD
    return pl.pallas_call(
        functools.partial(_hgrn_body, ts=ts),
        grid=(b, s // ts),
        in_specs=[
            pl.BlockSpec((1, ts, w), lambda bi, ti: (bi, ti, 0)),
            pl.BlockSpec((1, ts, w), lambda bi, ti: (bi, ti, 1)),
            pl.BlockSpec((1, ts, w), lambda bi, ti: (bi, ti, 2)),
            pl.BlockSpec((1, ts, w), lambda bi, ti: (bi, ti, 3)),
            pl.BlockSpec((A_HEADS, HD), lambda bi, ti: (0, 0)),
            pl.BlockSpec((1, HD), lambda bi, ti: (0, 0)),
        ],
        out_specs=pl.BlockSpec((1, ts, w), lambda bi, ti: (bi, ti, 0)),
        out_shape=jax.ShapeDtypeStruct((b, s, w), BF16),
        scratch_shapes=[pltpu.VMEM((A_HEADS, HD, HD), F32)],
        compiler_params=pltpu.CompilerParams(
            dimension_semantics=("parallel", "arbitrary"),
            vmem_limit_bytes=VMEM_LIMIT),
        name="hgrn",
    )(proj3, proj3, proj3, proj3, lb, gn)


def _gdn_body(bq_ref, bk_ref, bv_ref, bz_ref, sm_ref, smt_ref, cw_ref, pcol_ref, prow_ref,
              gn_ref, o_ref, s_ref, tail_ref, qn_ref, kn_ref, vv_ref,
              u_s, w_s, qg_s, kd_s, aqk_s, dl_s, *, ts):
    nqk = B_QK_HEADS * HD

    @pl.when(pl.program_id(1) == 0)
    def _():
        s_ref[...] = jnp.zeros_like(s_ref)
        tail_ref[...] = jnp.zeros_like(tail_ref)

    def conv_silu(x_ref, h, c0):
        cs = slice(h * HD, (h + 1) * HD)
        cc = slice(c0 + h * HD, c0 + (h + 1) * HD)
        x = x_ref[0, :, cs]
        xe = jnp.concatenate([tail_ref[:, cc], x], axis=0)
        w = cw_ref[:, cc]
        y = x * w[CONV - 1:CONV, :]
        for j in range(1, CONV):
            y = y + pltpu.roll(xe, j, axis=0)[SUBLANES:, :] * w[CONV - 1 - j:CONV - j, :]
        tail_ref[:, cc] = x[ts - SUBLANES:, :]
        return y * _sigmoid(y)

    def l2n(x):
        return x * lax.rsqrt(jnp.sum(x * x, axis=-1, keepdims=True) + EPS)

    for h in range(B_QK_HEADS):
        cs = slice(h * HD, (h + 1) * HD)
        qn_ref[:, cs] = l2n(conv_silu(bq_ref, h, 0)) * (HD ** -0.5)
        kn_ref[:, cs] = l2n(conv_silu(bk_ref, h, nqk))
    for h in range(B_V_HEADS):
        vv_ref[:, h * HD:(h + 1) * HD] = conv_silu(bv_ref, h, 2 * nqk)

    row = lax.broadcasted_iota(jnp.int32, (CHUNK, CHUNK), 0)
    col = lax.broadcasted_iota(jnp.int32, (CHUNK, CHUNK), 1)
    incl = col <= row
    strict = col < row
    tril_f = incl.astype(F32)
    triu_f = (row <= col).astype(F32)
    same_blk = (row // SUB) == (col // SUB)
    eye = (row == col).astype(F32)
    gn = gn_ref[...]
    a_col = pcol_ref[0:1, :]
    dt_col = pcol_ref[1:2, :]
    a_row = prow_ref[:, 0:1]
    dt_row = prow_ref[:, 1:2]
    rep = B_V_HEADS // B_QK_HEADS

    def prep_chunks(trip, carry):
        cs_ = [trip * PREP_CHUNKS + i for i in range(PREP_CHUNKS)]
        rows_ = [pl.ds(pl.multiple_of(c * CHUNK, CHUNK), CHUNK) for c in cs_]
        beta_c, gc_c, gc_r = [], [], []
        for c, rows in zip(cs_, rows_):
            sm = sm_ref[0, rows, :]
            beta_c.append(_sigmoid(sm))
            gc_c.append(_cumsum_mm(tril_f, -a_col * _softplus(sm + dt_col), False))
            smt = smt_ref[0, c]
            gc_r.append(_cumsum_mm(triu_f, -a_row * _softplus(smt + dt_row), True))
        items = [(i, h) for i in range(PREP_CHUNKS) for h in range(B_V_HEADS)]
        ns = range(len(items))
        qq = [qn_ref[rows_[i], (h // rep) * HD:(h // rep + 1) * HD] for i, h in items]
        kk = [kn_ref[rows_[i], (h // rep) * HD:(h // rep + 1) * HD] for i, h in items]
        kk16 = [k.astype(BF16) for k in kk]
        bcol = [beta_c[i][:, h:h + 1] for i, h in items]
        gcol = [gc_c[i][:, B_V_HEADS + h:B_V_HEADS + h + 1] for i, h in items]
        decay = [jnp.where(incl, jnp.exp(jnp.minimum(
            gcol[n] - gc_r[i][B_V_HEADS + h:B_V_HEADS + h + 1, :], 0.0)), 0.0)
            for n, (i, h) in enumerate(items)]
        kb = [kk[n] * bcol[n] for n in ns]
        a = [jnp.where(strict, _nt(kb[n].astype(BF16), kk16[n]) * decay[n], 0.0) for n in ns]
        dg = [jnp.where(same_blk, a[n], 0.0) for n in ns]
        off = [a[n] - dg[n] for n in ns]
        tinv = [eye - dg[n] for n in ns]
        p = dg
        for _ in range(3):
            p = [_mm16(p[n], p[n]) for n in ns]
            tinv = [tinv[n] + _mm16(tinv[n], p[n]) for n in ns]
        pm = [_mm16(tinv[n], off[n]) for n in ns]
        pm2 = [_mm16(pm[n], pm[n]) for n in ns]
        egc = [jnp.exp(gcol[n]) for n in ns]
        x = [_mm16(tinv[n], jnp.concatenate(
            [vv_ref[rows_[i], h * HD:(h + 1) * HD] * bcol[n], kb[n] * egc[n]], axis=1))
             for n, (i, h) in enumerate(items)]
        x = [x[n] + _mm16(pm2[n], x[n]) for n in ns]
        x = [x[n] - _mm16(pm[n], x[n]) for n in ns]
        aqk = [jnp.where(incl, _nt(qq[n].astype(BF16), kk16[n]) * decay[n], 0.0) for n in ns]
        for n, (i, h) in enumerate(items):
            cs = slice(h * HD, (h + 1) * HD)
            g_last = gcol[n][CHUNK - 1:CHUNK, :]
            u_s[rows_[i], cs] = x[n][:, :HD]
            w_s[rows_[i], cs] = x[n][:, HD:].astype(BF16)
            qg_s[rows_[i], cs] = (qq[n] * egc[n]).astype(BF16)
            kd_s[rows_[i], cs] = (kk[n] * jnp.exp(g_last - gcol[n])).astype(BF16)
            aqk_s[cs_[i], h] = aqk[n].astype(BF16)
            dl_s[cs_[i], h:h + 1, :] = jnp.broadcast_to(jnp.exp(g_last), (1, HD))
        return carry

    lax.fori_loop(0, ts // CHUNK // PREP_CHUNKS, prep_chunks, 0)

    def scan_chunk(c, carry):
        r0 = pl.multiple_of(c * CHUNK, CHUNK)
        rows = pl.ds(r0, CHUNK)
        hs = range(B_V_HEADS)
        cols = [slice(h * HD, (h + 1) * HD) for h in hs]
        sts = [s_ref[h] for h in hs]
        st16 = [s.astype(BF16) for s in sts]
        v16 = [(u_s[rows, cols[h]] - _mm(w_s[rows, cols[h]], st16[h])).astype(BF16) for h in hs]
        o = [_mm(qg_s[rows, cols[h]], st16[h]) + _mm(aqk_s[c, h], v16[h]) for h in hs]
        new_sts = [sts[h] * dl_s[c, h:h + 1, :] + _tn(kd_s[rows, cols[h]], v16[h]) for h in hs]
        for h in hs:
            s_ref[h] = new_sts[h]
        for h in hs:
            zz = bz_ref[0, rows, cols[h]]
            y = o[h] * lax.rsqrt(jnp.mean(o[h] * o[h], axis=-1, keepdims=True) + EPS)
            o_ref[0, rows, cols[h]] = (y * gn * (zz * _sigmoid(zz))).astype(o_ref.dtype)
        return carry

    lax.fori_loop(0, ts // CHUNK, scan_chunk, 0)


def _gdn(proj3, smt, conv_w, pcol, prow, gn, ts):
    b, s, _ = proj3.shape
    nqk = B_QK_HEADS * HD
    wv = B_V_HEADS * HD
    off_q = (4 * A_HEADS * HD) // nqk
    off_v = (4 * A_HEADS * HD + 2 * nqk) // wv
    off_s = (4 * A_HEADS * HD + 2 * nqk + 2 * wv) // HD
    nch = ts // CHUNK
    return pl.pallas_call(
        functools.partial(_gdn_body, ts=ts),
        grid=(b, s // ts),
        in_specs=[
            pl.BlockSpec((1, ts, nqk), lambda bi, ti: (bi, ti, off_q)),
            pl.BlockSpec((1, ts, nqk), lambda bi, ti: (bi, ti, off_q + 1)),
            pl.BlockSpec((1, ts, wv), lambda bi, ti: (bi, ti, off_v)),
            pl.BlockSpec((1, ts, wv), lambda bi, ti: (bi, ti, off_v + 1)),
            pl.BlockSpec((1, ts, HD), lambda bi, ti: (bi, ti, off_s)),
            pl.BlockSpec((1, nch, 2 * B_V_HEADS, CHUNK), lambda bi, ti: (bi, ti, 0, 0)),
            pl.BlockSpec((CONV, 2 * nqk + wv), lambda bi, ti: (0, 0)),
            pl.BlockSpec((2, HD), lambda bi, ti: (0, 0)),
            pl.BlockSpec((2 * B_V_HEADS, HD), lambda bi, ti: (0, 0)),
            pl.BlockSpec((1, HD), lambda bi, ti: (0, 0)),
        ],
        out_specs=pl.BlockSpec((1, ts, wv), lambda bi, ti: (bi, ti, 0)),
        out_shape=jax.ShapeDtypeStruct((b, s, wv), BF16),
        scratch_shapes=[
            pltpu.VMEM((B_V_HEADS, HD, HD), F32),
            pltpu.VMEM((SUBLANES, 2 * nqk + wv), F32),
            pltpu.VMEM((ts, nqk), F32),
            pltpu.VMEM((ts, nqk), F32),
            pltpu.VMEM((ts, wv), F32),
            pltpu.VMEM((ts, wv), F32),
            pltpu.VMEM((ts, wv), BF16),
            pltpu.VMEM((ts, wv), BF16),
            pltpu.VMEM((ts, wv), BF16),
            pltpu.VMEM((nch, B_V_HEADS, CHUNK, CHUNK), BF16),
            pltpu.VMEM((nch, B_V_HEADS, HD), F32),
        ],
        compiler_params=pltpu.CompilerParams(
            dimension_semantics=("parallel", "arbitrary"),
            vmem_limit_bytes=VMEM_LIMIT),
        name="gdn",
    )(proj3, proj3, proj3, proj3, proj3, smt, conv_w, pcol, prow, gn)


def _outproj_body(x_ref, oa_ref, ob_ref, wo_ref, fg_ref, wq_ref, k1_ref, k2_ref,
                  h_ref, hn_ref, s1_ref, s2_ref):
    wa = oa_ref.shape[1]
    h = x_ref[...] + _mm(oa_ref[...], wo_ref[:wa, :]) + _mm(ob_ref[...], wo_ref[wa:, :])
    h_ref[...] = h
    hn = h * lax.rsqrt(jnp.mean(h * h, axis=-1, keepdims=True) + EPS) * fg_ref[...]
    hn_ref[...] = hn
    qry = _mm(hn.astype(BF16), wq_ref[...])
    for hh in range(P_HEADS):
        q1 = qry[:, hh * 2 * HD:hh * 2 * HD + HD].astype(BF16)
        q2 = qry[:, hh * 2 * HD + HD:(hh + 1) * 2 * HD].astype(BF16)
        s1_ref[hh] = _nt(k1_ref[hh].astype(BF16), q1)
        s2_ref[hh] = _nt(k2_ref[hh].astype(BF16), q2)


def _out_proj(x2, oa, ob, wo, fg, wq, k1, k2, tm):
    t, d = x2.shape
    wa = oa.shape[1]
    wb = ob.shape[1]
    const = dict(pipeline_mode=pl.Buffered(1))
    return pl.pallas_call(
        _outproj_body,
        grid=(t // tm,),
        in_specs=[
            pl.BlockSpec((tm, d), lambda i: (i, 0)),
            pl.BlockSpec((tm, wa), lambda i: (i, 0)),
            pl.BlockSpec((tm, wb), lambda i: (i, 0)),
            pl.BlockSpec((wa + wb, d), lambda i: (0, 0), **const),
            pl.BlockSpec((1, d), lambda i: (0, 0)),
            pl.BlockSpec((d, P_HEADS * 2 * HD), lambda i: (0, 0), **const),
            pl.BlockSpec((P_HEADS, N_KEYS, HD), lambda i: (0, 0, 0)),
            pl.BlockSpec((P_HEADS, N_KEYS, HD), lambda i: (0, 0, 0)),
        ],
        out_specs=[
            pl.BlockSpec((tm, d), lambda i: (i, 0)),
            pl.BlockSpec((tm, d), lambda i: (i, 0)),
            pl.BlockSpec((P_HEADS, N_KEYS, tm), lambda i: (0, 0, i)),
            pl.BlockSpec((P_HEADS, N_KEYS, tm), lambda i: (0, 0, i)),
        ],
        out_shape=[
            jax.ShapeDtypeStruct((t, d), F32),
            jax.ShapeDtypeStruct((t, d), F32),
            jax.ShapeDtypeStruct((P_HEADS, N_KEYS, t), F32),
            jax.ShapeDtypeStruct((P_HEADS, N_KEYS, t), F32),
        ],
        compiler_params=pltpu.CompilerParams(
            dimension_semantics=("parallel",),
            vmem_limit_bytes=VMEM_LIMIT),
        name="out_proj",
    )(x2, oa, ob, wo, fg, wq, k1, k2)


_PAIRS = [(i, j) for i in range(P_TOPK) for j in range(P_TOPK) if (i + 1) * (j + 1) <= P_TOPK]
_NCAND = -(-len(_PAIRS) // 8) * 8


def _topk_body(s1_ref, s2_ref, pos_ref, e_ref, g_ref, v_scr, i_scr, c_scr, ce_scr, b_scr, x_scr):
    tt = s1_ref.shape[2]
    kio = lax.broadcasted_iota(jnp.int32, (N_KEYS, tt), 0).astype(F32)
    neg = -jnp.inf
    for half, sref in ((0, s1_ref), (1, s2_ref)):
        s = sref[0]
        for r in range(P_TOPK):
            m = jnp.max(s, axis=0, keepdims=True)
            idx = jnp.min(jnp.where(s == m, kio, float(N_KEYS)), axis=0, keepdims=True)
            v_scr[half, r:r + 1, :] = m
            i_scr[half, r:r + 1, :] = idx
            s = jnp.where(kio == idx, neg, s)
    c_scr[...] = jnp.full(c_scr.shape, neg, F32)
    ce_scr[...] = jnp.zeros(ce_scr.shape, F32)
    for c, (i, j) in enumerate(_PAIRS):
        c_scr[c:c + 1, :] = v_scr[0, i:i + 1, :] + v_scr[1, j:j + 1, :]
        ce_scr[c:c + 1, :] = i_scr[0, i:i + 1, :] * float(N_KEYS) + i_scr[1, j:j + 1, :]
    cand = c_scr[...]
    ce = ce_scr[...]
    pos = jnp.broadcast_to(pos_ref[:, 0:1], cand.shape)
    for r in range(P_TOPK):
        m = jnp.max(cand, axis=0, keepdims=True)
        sel = jnp.min(jnp.where(cand == m, pos, 1e9), axis=0, keepdims=True)
        hit = pos == sel
        b_scr[r:r + 1, :] = m
        x_scr[r:r + 1, :] = jnp.max(jnp.where(hit, ce, -1.0), axis=0, keepdims=True)
        cand = jnp.where(hit, neg, cand)
    best = b_scr[...]
    ex = jnp.exp(best - best[0:1, :])
    g_ref[...] = ex / jnp.sum(ex, axis=0, keepdims=True)
    e_ref[...] = x_scr[...].astype(jnp.int32)


def _peer_topk(s1, s2, pos, tt):
    t = s1.shape[2]
    return pl.pallas_call(
        _topk_body,
        grid=(t // tt, P_HEADS),
        in_specs=[
            pl.BlockSpec((1, N_KEYS, tt), lambda i, h: (h, 0, i)),
            pl.BlockSpec((1, N_KEYS, tt), lambda i, h: (h, 0, i)),
            pl.BlockSpec((_NCAND, HD), lambda i, h: (0, 0)),
        ],
        out_specs=[
            pl.BlockSpec((P_TOPK, tt), lambda i, h: (h, i)),
            pl.BlockSpec((P_TOPK, tt), lambda i, h: (h, i)),
        ],
        out_shape=[
            jax.ShapeDtypeStruct((P_HEADS * P_TOPK, t), jnp.int32),
            jax.ShapeDtypeStruct((P_HEADS * P_TOPK, t), F32),
        ],
        scratch_shapes=[
            pltpu.VMEM((2, P_TOPK, tt), F32),
            pltpu.VMEM((2, P_TOPK, tt), F32),
            pltpu.VMEM((_NCAND, tt), F32),
            pltpu.VMEM((_NCAND, tt), F32),
            pltpu.VMEM((P_TOPK, tt), F32),
            pltpu.VMEM((P_TOPK, tt), F32),
        ],
        compiler_params=pltpu.CompilerParams(
            dimension_semantics=("parallel", "parallel"),
            vmem_limit_bytes=VMEM_LIMIT),
        name="peer_topk",
    )(s1, s2, pos)


PEER_TB = 128
PEER_G = 8
PEER_SETS = 4
DMA_QUEUES = 2
NSLOT = P_HEADS * P_TOPK
SG = SUBLANES
CC_AHEAD = 3
ROW_PITCH = 17
HID_UNROLL = 1
MIX_UNROLL = 2


def _gelu(x):
    return 0.5 * x * (1.0 + lax.erf(x * (2.0 ** -0.5)))


def _unrolled_loop(n, unroll, body):
    def trip(q, carry):
        for j in range(unroll):
            body(q * unroll + j, 0)
        return carry
    lax.fori_loop(0, n // unroll, trip, 0)


def _pack_body(u_ref, v_ref, o_ref):
    ub = lax.bitcast_convert_type(u_ref[...].astype(BF16).astype(F32), jnp.uint32) >> 16
    vb = lax.bitcast_convert_type(v_ref[...].astype(BF16).astype(F32), jnp.uint32)
    w = (vb & jnp.uint32(0xFFFF0000)) | ub
    nlc = w.shape[1] // HD
    for c in range(nlc):
        o_ref[pl.ds(c, w.shape[0], stride=nlc), :] = w[:, c * HD:(c + 1) * HD]


def _peer_pack(u_tab, v_tab, te):
    e, d = u_tab.shape
    return pl.pallas_call(
        _pack_body,
        grid=(e // te,),
        in_specs=[pl.BlockSpec((te, d), lambda i: (i, 0)),
                  pl.BlockSpec((te, d), lambda i: (i, 0))],
        out_specs=pl.BlockSpec((te * (d // HD), HD), lambda i: (i, 0)),
        out_shape=jax.ShapeDtypeStruct((e * (d // HD), HD), jnp.uint32),
        compiler_params=pltpu.CompilerParams(
            dimension_semantics=("parallel",),
            vmem_limit_bytes=VMEM_LIMIT),
        name="peer_pack",
    )(u_tab, v_tab)


def _peer_body(idx_ref, gt_ref, hn_ref, h_ref, fg_ref, w_hbm, o_ref,
               hid_scr, c_scr, y_scr, sem, *bufs, final, nsteps):
    tb, d = hn_ref.shape
    ngroups = tb // PEER_G
    nsg = NSLOT // SG
    ahead = PEER_SETS - 1
    nlc = d // HD
    step = pl.program_id(0)

    def issue_tok(tok, set_, r, s0, s1):
        ids = idx_ref.at[tok]
        dst = bufs[set_].at[pl.ds(r * NSLOT * ROW_PITCH, NSLOT * ROW_PITCH)]
        for s in range(s0, s1):
            e0 = pl.multiple_of(ids[s] * nlc, nlc)
            pltpu.make_async_copy(w_hbm.at[pl.ds(e0, nlc)], dst.at[pl.ds(s * ROW_PITCH, nlc)],
                                  sem.at[set_]).start(priority=s % DMA_QUEUES)

    def wait_set(set_):
        full = bufs[set_].at[pl.ds(0, PEER_G * NSLOT * nlc)]
        pltpu.make_async_copy(full, full, sem.at[set_]).wait()

    @pl.when(step == 0)
    def _():
        def first(r, carry):
            for a in range(ahead):
                issue_tok(a * PEER_G + r, a, r, 0, NSLOT)
            return carry
        lax.fori_loop(0, PEER_G, first, 0)

    per_tile = NSLOT // 2 // nsg
    lane = lax.broadcasted_iota(jnp.int32, (SG, tb), 1)
    hid_scr[...] = jnp.zeros_like(hid_scr)

    def group_body(g, set_):
        nxt = (set_ + ahead) % PEER_SETS
        wait_set(set_)


        def hid_tok(r, c2):
            t = g * PEER_G + r
            xrow = hn_ref[pl.ds(t, 1), :]
            xb = [jnp.broadcast_to(xrow[:, lc * HD:(lc + 1) * HD], (SG, HD)) for lc in range(nlc)]
            tiles = bufs[set_].at[pl.ds(r * NSLOT * ROW_PITCH, NSLOT * ROW_PITCH)]
            hs = []
            for sg in range(nsg):
                issue_tok((g + ahead) * PEER_G + r, nxt, r, sg * per_tile, (sg + 1) * per_tile)
                acc = None
                for lc in range(nlc):
                    w = tiles[pl.ds(sg * SG * ROW_PITCH + lc, SG, stride=ROW_PITCH), :]
                    u = lax.bitcast_convert_type(w << 16, F32)
                    term = u * xb[lc]
                    acc = term if acc is None else acc + term
                hs.append(jnp.sum(acc, axis=1, keepdims=True))
            for sg in range(nsg):
                pltpu.store(hid_scr.at[pl.ds(sg * SG, SG), :],
                            jnp.broadcast_to(hs[sg], (SG, tb)), mask=lane == t)
            return c2
        _unrolled_loop(PEER_G, HID_UNROLL, hid_tok)

        c_scr[...] = gt_ref[...] * _gelu(hid_scr[...])

        def mix_tok(r, c2):
            t = g * PEER_G + r
            tiles = bufs[set_].at[pl.ds(r * NSLOT * ROW_PITCH, NSLOT * ROW_PITCH)]
            def coef_col(sg):
                return jnp.sum(jnp.where(lane == t, c_scr[pl.ds(sg * SG, SG), :], 0.0),
                               axis=1, keepdims=True)
            cc = {sg: coef_col(sg) for sg in range(CC_AHEAD)}
            accs = [None] * nlc
            for sg in range(nsg):
                issue_tok((g + ahead) * PEER_G + r, nxt, r,
                          NSLOT // 2 + sg * per_tile, NSLOT // 2 + (sg + 1) * per_tile)
                if sg + CC_AHEAD < nsg:
                    cc[sg + CC_AHEAD] = coef_col(sg + CC_AHEAD)
                for lc in range(nlc):
                    w = tiles[pl.ds(sg * SG * ROW_PITCH + lc, SG, stride=ROW_PITCH), :]
                    v = lax.bitcast_convert_type(w & jnp.uint32(0xFFFF0000), F32)
                    term = v * cc[sg]
                    accs[lc] = term if accs[lc] is None else accs[lc] + term
            yrow = jnp.concatenate(
                [jnp.sum(a, axis=0, keepdims=True) for a in accs], axis=1)
            y_scr[pl.ds(t, 1), :] = yrow
            return c2
        _unrolled_loop(PEER_G, MIX_UNROLL, mix_tok)

    def group_round(q, carry):
        for j in range(PEER_SETS):
            group_body(q * PEER_SETS + j, j)
        return carry

    lax.fori_loop(0, ngroups // PEER_SETS, group_round, 0)

    @pl.when(step == nsteps - 1)
    def _():
        for a in range(ahead):
            wait_set((nsteps * ngroups + a) % PEER_SETS)

    hh = h_ref[...] + y_scr[...]
    if final:
        hh = hh * lax.rsqrt(jnp.mean(hh * hh, axis=-1, keepdims=True) + EPS) * fg_ref[...]
    o_ref[...] = hh


def _peer_mix(idx, gt, hn, h, fg, w_tab, final):
    t, d = hn.shape
    tb = PEER_TB
    nsteps = t // tb
    extra = (PEER_SETS - 1) * PEER_G
    assert (tb // PEER_G) % PEER_SETS == 0
    idx_pad = jnp.concatenate([idx, jnp.zeros((tb, NSLOT), idx.dtype)], axis=0)
    idx_ext = jnp.concatenate(
        [idx.reshape(nsteps, tb, NSLOT),
         idx_pad[tb:].reshape(nsteps, tb, NSLOT)[:, :extra]], axis=1)
    return pl.pallas_call(
        functools.partial(_peer_body, final=final, nsteps=nsteps),
        grid=(nsteps,),
        in_specs=[
            pl.BlockSpec((None, tb + extra, NSLOT), lambda i: (i, 0, 0),
                         memory_space=pltpu.SMEM),
            pl.BlockSpec((NSLOT, tb), lambda i: (0, i)),
            pl.BlockSpec((tb, d), lambda i: (i, 0)),
            pl.BlockSpec((tb, d), lambda i: (i, 0)),
            pl.BlockSpec((1, d), lambda i: (0, 0)),
            pl.BlockSpec(memory_space=pl.ANY),
        ],
        out_specs=pl.BlockSpec((tb, d), lambda i: (i, 0)),
        out_shape=jax.ShapeDtypeStruct((t, d), F32),
        scratch_shapes=[
            pltpu.VMEM((NSLOT, tb), F32),
            pltpu.VMEM((NSLOT, tb), F32),
            pltpu.VMEM((tb, d), F32),
            pltpu.SemaphoreType.DMA((PEER_SETS,)),
        ] + [pltpu.VMEM((PEER_G * NSLOT * ROW_PITCH, HD), jnp.uint32) for _ in range(PEER_SETS)],
        compiler_params=pltpu.CompilerParams(
            dimension_semantics=("arbitrary",),
            vmem_limit_bytes=VMEM_LIMIT),
        name="peer_mix",
    )(idx_ext, gt, hn, h, fg, w_tab)


def _tile(n, pref):
    return pref if n % pref == 0 else n


class _Tiles(NamedTuple):
    in_proj_rows: int
    in_proj_cols: int
    seq: int
    out_proj_rows: int
    topk_tokens: int
    pack_experts: int


def _plan_tiles(t, s, in_pad, n_experts):
    third = in_pad // 3 if (in_pad // HD) % 3 == 0 else in_pad
    return _Tiles(in_proj_rows=_tile(t, 512), in_proj_cols=third, seq=_tile(s, 256),
                  out_proj_rows=_tile(t, 256), topk_tokens=_tile(t, 512),
                  pack_experts=_tile(n_experts, 256))


def kernel(x, attn_norm_g, w_in, hgrn_lb_logits, hgrn_norm_g, gdn_conv_w, gdn_A_log, gdn_dt_bias,
           gdn_norm_g, w_out, ffn_norm_g, peer_w_query, peer_sub_keys, peer_u, peer_v,
           final_norm_g):
    b, s, d = x.shape
    t = b * s
    depth = w_in.shape[0]
    in_width = w_in.shape[2]
    n_small = 2 * B_V_HEADS
    n_main = in_width - n_small
    in_pad = n_main + HD
    lb_all = jnp.cumsum(jax.nn.softmax(hgrn_lb_logits.astype(F32), axis=0), axis=0)
    pos = jnp.array([i * P_TOPK + j for i, j in _PAIRS]
                    + [10 ** 6 + c for c in range(_NCAND - len(_PAIRS))], F32)
    pos = jnp.broadcast_to(pos[:, None], (_NCAND, HD))

    tiles = _plan_tiles(t, s, in_pad, peer_u.shape[1])

    h2 = x.reshape(t, d)
    for l in range(depth):
        w_l = jnp.pad(w_in[l], ((0, 0), (0, in_pad - in_width))).astype(BF16)
        proj = _in_proj(h2, attn_norm_g[l][None, :], w_l, tiles.in_proj_rows, tiles.in_proj_cols)
        proj3 = proj.reshape(b, s, in_pad)

        o_a = _hgrn(proj3, lb_all[l], hgrn_norm_g[l][None, :], tiles.seq)

        small = proj3[:, :, n_main:n_main + n_small]
        smt = small.reshape(b, s // CHUNK, CHUNK, n_small).transpose(0, 1, 3, 2)
        zeros8 = jnp.zeros((B_V_HEADS,), F32)
        a_neg = jnp.exp(gdn_A_log[l].astype(F32))
        dtb = gdn_dt_bias[l].astype(F32)
        pcol = jnp.zeros((2, HD), F32)
        pcol = pcol.at[0, B_V_HEADS:n_small].set(a_neg).at[1, B_V_HEADS:n_small].set(dtb)
        prow = jnp.zeros((n_small, HD), F32)
        prow = prow.at[:, 0].set(jnp.concatenate([zeros8, a_neg]))
        prow = prow.at[:, 1].set(jnp.concatenate([zeros8, dtb]))
        o_b = _gdn(proj3, smt, gdn_conv_w[l], pcol, prow, gdn_norm_g[l][None, :], tiles.seq)

        wq = peer_w_query[l].reshape(d, P_HEADS * 2 * HD).astype(BF16)
        h2, hn, s1, s2 = _out_proj(
            h2, o_a.reshape(t, -1), o_b.reshape(t, -1), w_out[l].astype(BF16),
            ffn_norm_g[l][None, :], wq, peer_sub_keys[l, 0], peer_sub_keys[l, 1],
            tiles.out_proj_rows)

        e_t, g_t = _peer_topk(s1, s2, pos, tiles.topk_tokens)
        w_tab = _peer_pack(peer_u[l], peer_v[l], tiles.pack_experts)
        h2 = _peer_mix(e_t.T, g_t, hn, h2, final_norm_g[None, :], w_tab, l == depth - 1)
    return h2.reshape(b, s, d)
```

```python
import functools
from typing import NamedTuple

import jax
import jax.numpy as jnp
from jax import lax
from jax.experimental import pallas as pl
from jax.experimental.pallas import tpu as pltpu

F32 = jnp.float32
BF16 = jnp.bfloat16

EPS = 1e-6
CHUNK = 64
SUB = 16
PREP_CHUNKS = 4
HD = 128
A_HEADS = 8
B_QK_HEADS = 4
B_V_HEADS = 8
CONV = 4
P_HEADS = 8
N_KEYS = 128
P_TOPK = 16
SUBLANES = 8
VMEM_LIMIT = 56 * 1024 * 1024


def _nt(a, b):
    return lax.dot_general(a, b, (((1,), (1,)), ((), ())), preferred_element_type=F32)


def _tn(a, b):
    return lax.dot_general(a, b, (((0,), (0,)), ((), ())), preferred_element_type=F32)


def _mm(a, b):
    return jnp.dot(a, b, preferred_element_type=F32)


def _split_bf16(x, n):
    parts = []
    for _ in range(n - 1):
        p = x.astype(BF16)
        parts.append(p)
        x = x - p.astype(F32)
    parts.append(x.astype(BF16))
    return parts


def _mm16(a, b):
    return _mm(a.astype(BF16), b.astype(BF16))


def _cumsum_mm(tri, g, g_is_lhs):
    t16 = tri.astype(BF16)
    out = None
    for p in reversed(_split_bf16(g, 3)):
        term = _mm(p, t16) if g_is_lhs else _mm(t16, p)
        out = term if out is None else out + term
    return out


def _sigmoid(x):
    return 1.0 / (1.0 + jnp.exp(-x))


def _softplus(x):
    return jnp.maximum(x, 0.0) + jnp.log1p(jnp.exp(-jnp.abs(x)))


def _bcast_rows(x, idx):
    n = x.shape[1]
    return jnp.concatenate(
        [jnp.broadcast_to(x[r:r + 1, :], (SUB, n)) for r in idx], axis=0)


def _inproj_body(x_ref, g_ref, w_ref, o_ref):
    x = x_ref[...]
    ms = jnp.mean(x * x, axis=-1, keepdims=True)
    xn = (x * lax.rsqrt(ms + EPS) * g_ref[...]).astype(BF16)
    o_ref[...] = _mm(xn, w_ref[...])


def _in_proj(x2, g, w, tm, tn):
    t, d = x2.shape
    n = w.shape[1]
    return pl.pallas_call(
        _inproj_body,
        grid=(n // tn, t // tm),
        in_specs=[
            pl.BlockSpec((tm, d), lambda j, i: (i, 0)),
            pl.BlockSpec((1, d), lambda j, i: (0, 0)),
            pl.BlockSpec((d, tn), lambda j, i: (0, j)),
        ],
        out_specs=pl.BlockSpec((tm, tn), lambda j, i: (i, j)),
        out_shape=jax.ShapeDtypeStruct((t, n), F32),
        compiler_params=pltpu.CompilerParams(
            dimension_semantics=("parallel", "parallel"),
            vmem_limit_bytes=VMEM_LIMIT),
        name="in_proj",
    )(x2, g, w)


def _hgrn_body(q_ref, f_ref, i_ref, g_ref, lb_ref, gn_ref, o_ref, st_ref, *, ts):
    @pl.when(pl.program_id(1) == 0)
    def _():
        st_ref[...] = jnp.zeros_like(st_ref)

    row = lax.broadcasted_iota(jnp.int32, (CHUNK, CHUNK), 0)
    col = lax.broadcasted_iota(jnp.int32, (CHUNK, CHUNK), 1)
    tril_f = (col <= row).astype(F32)
    blk_r = row // SUB
    blk_c = col // SUB
    rblk = lax.broadcasted_iota(jnp.int32, (CHUNK, HD), 0) // SUB
    nsub = CHUNK // SUB
    gn = gn_ref[...]

    def chunk_body(c, carry):
        r0 = pl.multiple_of(c * CHUNK, CHUNK)
        rows = pl.ds(r0, CHUNK)
        hs = range(A_HEADS)
        cols = [slice(h * HD, (h + 1) * HD) for h in hs]
        q = [q_ref[0, rows, cols[h]] for h in hs]
        z = [f_ref[0, rows, cols[h]] for h in hs]
        v16 = [i_ref[0, rows, cols[h]].astype(BF16) for h in hs]
        lb = [lb_ref[h:h + 1, :] for h in hs]
        sts = [st_ref[h] for h in hs]
        k = [(1.0 - lb[h]) * _sigmoid(-z[h]) for h in hs]
        b = [_cumsum_mm(tril_f, jnp.log(lb[h] + (1.0 - lb[h]) * _sigmoid(z[h])), False)
             for h in hs]
        o = [_nt((q[h] * jnp.exp(b[h])).astype(BF16), sts[h].astype(BF16)) for h in hs]
        att = []
        for h in hs:
            bnd = [b[h][SUB * j + SUB - 1:SUB * j + SUB, :] for j in range(nsub)]
            k_off = k[h] * jnp.exp(
                _bcast_rows(b[h], [SUB * j + SUB - 1 for j in range(nsub)]) - b[h])
            qcat = jnp.concatenate(
                [q[h] * jnp.exp(jnp.minimum(b[h] - bnd[j], 0.0)) for j in range(nsub - 1)],
                axis=1)
            kcat = jnp.concatenate(
                [jnp.where(rblk == j, k_off, 0.0) for j in range(nsub - 1)], axis=1)
            att_off = _nt(qcat.astype(BF16), kcat.astype(BF16))
            ref_rows = _bcast_rows(b[h], [SUB * i for i in range(nsub)])
            q_d = q[h] * jnp.exp(b[h] - ref_rows)
            k_d = k[h] * jnp.exp(ref_rows - b[h])
            att_d = _nt(q_d.astype(BF16), k_d.astype(BF16))
            att.append(jnp.where(blk_r > blk_c, att_off,
                                 jnp.where((blk_r == blk_c) & (col <= row), att_d, 0.0)))
        o = [o[h] + _mm(att[h].astype(BF16), v16[h]) for h in hs]
        new_sts = []
        for h in hs:
            b_last = b[h][CHUNK - 1:CHUNK, :]
            kdec = k[h] * jnp.exp(b_last - b[h])
            new_sts.append(sts[h] * jnp.exp(b_last) + _tn(v16[h], kdec.astype(BF16)))
        for h in hs:
            st_ref[h] = new_sts[h]
        for h in hs:
            gate = g_ref[0, rows, cols[h]]
            y = o[h] * lax.rsqrt(jnp.mean(o[h] * o[h], axis=-1, keepdims=True) + EPS)
            o_ref[0, rows, cols[h]] = (y * gn * (gate * _sigmoid(gate))).astype(o_ref.dtype)
        return carry

    lax.fori_loop(0, ts // CHUNK, chunk_body, 0)


def _hgrn(proj3, lb, gn, ts):
    b, s, _ = proj3.shape
    w = A_HEADS * HD
    return pl.pallas_call(
        functools.partial(_hgrn_body, ts=ts),
        grid=(b, s // ts),
        in_specs=[
            pl.BlockSpec((1, ts, w), lambda bi, ti: (bi, ti, 0)),
            pl.BlockSpec((1, ts, w), lambda bi, ti: (bi, ti, 1)),
            pl.BlockSpec((1, ts, w), lambda bi, ti: (bi, ti, 2)),
            pl.BlockSpec((1, ts, w), lambda bi, ti: (bi, ti, 3)),
            pl.BlockSpec((A_HEADS, HD), lambda bi, ti: (0, 0)),
            pl.BlockSpec((1, HD), lambda bi, ti: (0, 0)),
        ],
        out_specs=pl.BlockSpec((1, ts, w), lambda bi, ti: (bi, ti, 0)),
        out_shape=jax.ShapeDtypeStruct((b, s, w), BF16),
        scratch_shapes=[pltpu.VMEM((A_HEADS, HD, HD), F32)],
        compiler_params=pltpu.CompilerParams(
            dimension_semantics=("parallel", "arbitrary"),
            vmem_limit_bytes=VMEM_LIMIT),
        name="hgrn",
    )(proj3, proj3, proj3, proj3, lb, gn)


def _gdn_body(bq_ref, bk_ref, bv_ref, bz_ref, sm_ref, smt_ref, cw_ref, pcol_ref, prow_ref,
              gn_ref, o_ref, s_ref, tail_ref, qn_ref, kn_ref, vv_ref,
              u_s, w_s, qg_s, kd_s, aqk_s, dl_s, *, ts):
    nqk = B_QK_HEADS * HD

    @pl.when(pl.program_id(1) == 0)
    def _():
        s_ref[...] = jnp.zeros_like(s_ref)
        tail_ref[...] = jnp.zeros_like(tail_ref)

    def conv_silu(x_ref, h, c0):
        cs = slice(h * HD, (h + 1) * HD)
        cc = slice(c0 + h * HD, c0 + (h + 1) * HD)
        x = x_ref[0, :, cs]
        xe = jnp.concatenate([tail_ref[:, cc], x], axis=0)
        w = cw_ref[:, cc]
        y = x * w[CONV - 1:CONV, :]
        for j in range(1, CONV):
            y = y + pltpu.roll(xe, j, axis=0)[SUBLANES:, :] * w[CONV - 1 - j:CONV - j, :]
        tail_ref[:, cc] = x[ts - SUBLANES:, :]
        return y * _sigmoid(y)

    def l2n(x):
        return x * lax.rsqrt(jnp.sum(x * x, axis=-1, keepdims=True) + EPS)

    for h in range(B_QK_HEADS):
        cs = slice(h * HD, (h + 1) * HD)
        qn_ref[:, cs] = l2n(conv_silu(bq_ref, h, 0)) * (HD ** -0.5)
        kn_ref[:, cs] = l2n(conv_silu(bk_ref, h, nqk))
    for h in range(B_V_HEADS):
        vv_ref[:, h * HD:(h + 1) * HD] = conv_silu(bv_ref, h, 2 * nqk)

    row = lax.broadcasted_iota(jnp.int32, (CHUNK, CHUNK), 0)
    col = lax.broadcasted_iota(jnp.int32, (CHUNK, CHUNK), 1)
    incl = col <= row
    strict = col < row
    tril_f = incl.astype(F32)
    triu_f = (row <= col).astype(F32)
    same_blk = (row // SUB) == (col // SUB)
    eye = (row == col).astype(F32)
    gn = gn_ref[...]
    a_col = pcol_ref[0:1, :]
    dt_col = pcol_ref[1:2, :]
    a_row = prow_ref[:, 0:1]
    dt_row = prow_ref[:, 1:2]
    rep = B_V_HEADS // B_QK_HEADS

    def prep_chunks(trip, carry):
        cs_ = [trip * PREP_CHUNKS + i for i in range(PREP_CHUNKS)]
        rows_ = [pl.ds(pl.multiple_of(c * CHUNK, CHUNK), CHUNK) for c in cs_]
        beta_c, gc_c, gc_r = [], [], []
        for c, rows in zip(cs_, rows_):
            sm = sm_ref[0, rows, :]
            beta_c.append(_sigmoid(sm))
            gc_c.append(_cumsum_mm(tril_f, -a_col * _softplus(sm + dt_col), False))
            smt = smt_ref[0, c]
            gc_r.append(_cumsum_mm(triu_f, -a_row * _softplus(smt + dt_row), True))
        items = [(i, h) for i in range(PREP_CHUNKS) for h in range(B_V_HEADS)]
        ns = range(len(items))
        qq = [qn_ref[rows_[i], (h // rep) * HD:(h // rep + 1) * HD] for i, h in items]
        kk = [kn_ref[rows_[i], (h // rep) * HD:(h // rep + 1) * HD] for i, h in items]
        kk16 = [k.astype(BF16) for k in kk]
        bcol = [beta_c[i][:, h:h + 1] for i, h in items]
        gcol = [gc_c[i][:, B_V_HEADS + h:B_V_HEADS + h + 1] for i, h in items]
        decay = [jnp.where(incl, jnp.exp(jnp.minimum(
            gcol[n] - gc_r[i][B_V_HEADS + h:B_V_HEADS + h + 1, :], 0.0)), 0.0)
            for n, (i, h) in enumerate(items)]
        kb = [kk[n] * bcol[n] for n in ns]
        a = [jnp.where(strict, _nt(kb[n].astype(BF16), kk16[n]) * decay[n], 0.0) for n in ns]
        dg = [jnp.where(same_blk, a[n], 0.0) for n in ns]
        off = [a[n] - dg[n] for n in ns]
        tinv = [eye - dg[n] for n in ns]
        p = dg
        for _ in range(3):
            p = [_mm16(p[n], p[n]) for n in ns]
            tinv = [tinv[n] + _mm16(tinv[n], p[n]) for n in ns]
        pm = [_mm16(tinv[n], off[n]) for n in ns]
        pm2 = [_mm16(pm[n], pm[n]) for n in ns]
        egc = [jnp.exp(gcol[n]) for n in ns]
        x = [_mm16(tinv[n], jnp.concatenate(
            [vv_ref[rows_[i], h * HD:(h + 1) * HD] * bcol[n], kb[n] * egc[n]], axis=1))
             for n, (i, h) in enumerate(items)]
        x = [x[n] + _mm16(pm2[n], x[n]) for n in ns]
        x = [x[n] - _mm16(pm[n], x[n]) for n in ns]
        aqk = [jnp.where(incl, _nt(qq[n].astype(BF16), kk16[n]) * decay[n], 0.0) for n in ns]
        for n, (i, h) in enumerate(items):
            cs = slice(h * HD, (h + 1) * HD)
            g_last = gcol[n][CHUNK - 1:CHUNK, :]
            u_s[rows_[i], cs] = x[n][:, :HD]
            w_s[rows_[i], cs] = x[n][:, HD:].astype(BF16)
            qg_s[rows_[i], cs] = (qq[n] * egc[n]).astype(BF16)
            kd_s[rows_[i], cs] = (kk[n] * jnp.exp(g_last - gcol[n])).astype(BF16)
            aqk_s[cs_[i], h] = aqk[n].astype(BF16)
            dl_s[cs_[i], h:h + 1, :] = jnp.broadcast_to(jnp.exp(g_last), (1, HD))
        return carry

    lax.fori_loop(0, ts // CHUNK // PREP_CHUNKS, prep_chunks, 0)

    def scan_chunk(c, carry):
        r0 = pl.multiple_of(c * CHUNK, CHUNK)
        rows = pl.ds(r0, CHUNK)
        hs = range(B_V_HEADS)
        cols = [slice(h * HD, (h + 1) * HD) for h in hs]
        sts = [s_ref[h] for h in hs]
        st16 = [s.astype(BF16) for s in sts]
        v16 = [(u_s[rows, cols[h]] - _mm(w_s[rows, cols[h]], st16[h])).astype(BF16) for h in hs]
        o = [_mm(qg_s[rows, cols[h]], st16[h]) + _mm(aqk_s[c, h], v16[h]) for h in hs]
        new_sts = [sts[h] * dl_s[c, h:h + 1, :] + _tn(kd_s[rows, cols[h]], v16[h]) for h in hs]
        for h in hs:
            s_ref[h] = new_sts[h]
        for h in hs:
            zz = bz_ref[0, rows, cols[h]]
            y = o[h] * lax.rsqrt(jnp.mean(o[h] * o[h], axis=-1, keepdims=True) + EPS)
            o_ref[0, rows, cols[h]] = (y * gn * (zz * _sigmoid(zz))).astype(o_ref.dtype)
        return carry

    lax.fori_loop(0, ts // CHUNK, scan_chunk, 0)


def _gdn(proj3, smt, conv_w, pcol, prow, gn, ts):
    b, s, _ = proj3.shape
    nqk = B_QK_HEADS * HD
    wv = B_V_HEADS * HD
    off_q = (4 * A_HEADS * HD) // nqk
    off_v = (4 * A_HEADS * HD + 2 * nqk) // wv
    off_s = (4 * A_HEADS * HD + 2 * nqk + 2 * wv) // HD
    nch = ts // CHUNK
    return pl.pallas_call(
        functools.partial(_gdn_body, ts=ts),
        grid=(b, s // ts),
        in_specs=[
            pl.BlockSpec((1, ts, nqk), lambda bi, ti: (bi, ti, off_q)),
            pl.BlockSpec((1, ts, nqk), lambda bi, ti: (bi, ti, off_q + 1)),
            pl.BlockSpec((1, ts, wv), lambda bi, ti: (bi, ti, off_v)),
            pl.BlockSpec((1, ts, wv), lambda bi, ti: (bi, ti, off_v + 1)),
            pl.BlockSpec((1, ts, HD), lambda bi, ti: (bi, ti, off_s)),
            pl.BlockSpec((1, nch, 2 * B_V_HEADS, CHUNK), lambda bi, ti: (bi, ti, 0, 0)),
            pl.BlockSpec((CONV, 2 * nqk + wv), lambda bi, ti: (0, 0)),
            pl.BlockSpec((2, HD), lambda bi, ti: (0, 0)),
            pl.BlockSpec((2 * B_V_HEADS, HD), lambda bi, ti: (0, 0)),
            pl.BlockSpec((1, HD), lambda bi, ti: (0, 0)),
        ],
        out_specs=pl.BlockSpec((1, ts, wv), lambda bi, ti: (bi, ti, 0)),
        out_shape=jax.ShapeDtypeStruct((b, s, wv), BF16),
        scratch_shapes=[
            pltpu.VMEM((B_V_HEADS, HD, HD), F32),
            pltpu.VMEM((SUBLANES, 2 * nqk + wv), F32),
            pltpu.VMEM((ts, nqk), F32),
            pltpu.VMEM((ts, nqk), F32),
            pltpu.VMEM((ts, wv), F32),
            pltpu.VMEM((ts, wv), F32),
            pltpu.VMEM((ts, wv), BF16),
            pltpu.VMEM((ts, wv), BF16),
            pltpu.VMEM((ts, wv), BF16),
            pltpu.VMEM((nch, B_V_HEADS, CHUNK, CHUNK), BF16),
            pltpu.VMEM((nch, B_V_HEADS, HD), F32),
        ],
        compiler_params=pltpu.CompilerParams(
            dimension_semantics=("parallel", "arbitrary"),
            vmem_limit_bytes=VMEM_LIMIT),
        name="gdn",
    )(proj3, proj3, proj3, proj3, proj3, smt, conv_w, pcol, prow, gn)


def _outproj_body(x_ref, oa_ref, ob_ref, wo_ref, fg_ref, wq_ref, k1_ref, k2_ref,
                  h_ref, hn_ref, s1_ref, s2_ref):
    wa = oa_ref.shape[1]
    h = x_ref[...] + _mm(oa_ref[...], wo_ref[:wa, :]) + _mm(ob_ref[...], wo_ref[wa:, :])
    h_ref[...] = h
    hn = h * lax.rsqrt(jnp.mean(h * h, axis=-1, keepdims=True) + EPS) * fg_ref[...]
    hn_ref[...] = hn
    qry = _mm(hn.astype(BF16), wq_ref[...])
    for hh in range(P_HEADS):
        q1 = qry[:, hh * 2 * HD:hh * 2 * HD + HD].astype(BF16)
        q2 = qry[:, hh * 2 * HD + HD:(hh + 1) * 2 * HD].astype(BF16)
        s1_ref[hh] = _nt(k1_ref[hh].astype(BF16), q1)
        s2_ref[hh] = _nt(k2_ref[hh].astype(BF16), q2)


def _out_proj(x2, oa, ob, wo, fg, wq, k1, k2, tm):
    t, d = x2.shape
    wa = oa.shape[1]
    wb = ob.shape[1]
    const = dict(pipeline_mode=pl.Buffered(1))
    return pl.pallas_call(
        _outproj_body,
        grid=(t // tm,),
        in_specs=[
            pl.BlockSpec((tm, d), lambda i: (i, 0)),
            pl.BlockSpec((tm, wa), lambda i: (i, 0)),
            pl.BlockSpec((tm, wb), lambda i: (i, 0)),
            pl.BlockSpec((wa + wb, d), lambda i: (0, 0), **const),
            pl.BlockSpec((1, d), lambda i: (0, 0)),
            pl.BlockSpec((d, P_HEADS * 2 * HD), lambda i: (0, 0), **const),
            pl.BlockSpec((P_HEADS, N_KEYS, HD), lambda i: (0, 0, 0)),
            pl.BlockSpec((P_HEADS, N_KEYS, HD), lambda i: (0, 0, 0)),
        ],
        out_specs=[
            pl.BlockSpec((tm, d), lambda i: (i, 0)),
            pl.BlockSpec((tm, d), lambda i: (i, 0)),
            pl.BlockSpec((P_HEADS, N_KEYS, tm), lambda i: (0, 0, i)),
            pl.BlockSpec((P_HEADS, N_KEYS, tm), lambda i: (0, 0, i)),
        ],
        out_shape=[
            jax.ShapeDtypeStruct((t, d), F32),
            jax.ShapeDtypeStruct((t, d), F32),
            jax.ShapeDtypeStruct((P_HEADS, N_KEYS, t), F32),
            jax.ShapeDtypeStruct((P_HEADS, N_KEYS, t), F32),
        ],
        compiler_params=pltpu.CompilerParams(
            dimension_semantics=("parallel",),
            vmem_limit_bytes=VMEM_LIMIT),
        name="out_proj",
    )(x2, oa, ob, wo, fg, wq, k1, k2)


_PAIRS = [(i, j) for i in range(P_TOPK) for j in range(P_TOPK) if (i + 1) * (j + 1) <= P_TOPK]
_NCAND = -(-len(_PAIRS) // 8) * 8


def _topk_body(s1_ref, s2_ref, pos_ref, e_ref, g_ref, v_scr, i_scr, c_scr, ce_scr, b_scr, x_scr):
    tt = s1_ref.shape[2]
    kio = lax.broadcasted_iota(jnp.int32, (N_KEYS, tt), 0).astype(F32)
    neg = -jnp.inf
    for half, sref in ((0, s1_ref), (1, s2_ref)):
        s = sref[0]
        for r in range(P_TOPK):
            m = jnp.max(s, axis=0, keepdims=True)
            idx = jnp.min(jnp.where(s == m, kio, float(N_KEYS)), axis=0, keepdims=True)
            v_scr[half, r:r + 1, :] = m
            i_scr[half, r:r + 1, :] = idx
            s = jnp.where(kio == idx, neg, s)
    c_scr[...] = jnp.full(c_scr.shape, neg, F32)
    ce_scr[...] = jnp.zeros(ce_scr.shape, F32)
    for c, (i, j) in enumerate(_PAIRS):
        c_scr[c:c + 1, :] = v_scr[0, i:i + 1, :] + v_scr[1, j:j + 1, :]
        ce_scr[c:c + 1, :] = i_scr[0, i:i + 1, :] * float(N_KEYS) + i_scr[1, j:j + 1, :]
    cand = c_scr[...]
    ce = ce_scr[...]
    pos = jnp.broadcast_to(pos_ref[:, 0:1], cand.shape)
    for r in range(P_TOPK):
        m = jnp.max(cand, axis=0, keepdims=True)
        sel = jnp.min(jnp.where(cand == m, pos, 1e9), axis=0, keepdims=True)
        hit = pos == sel
        b_scr[r:r + 1, :] = m
        x_scr[r:r + 1, :] = jnp.max(jnp.where(hit, ce, -1.0), axis=0, keepdims=True)
        cand = jnp.where(hit, neg, cand)
    best = b_scr[...]
    ex = jnp.exp(best - best[0:1, :])
    g_ref[...] = ex / jnp.sum(ex, axis=0, keepdims=True)
    e_ref[...] = x_scr[...].astype(jnp.int32)


def _peer_topk(s1, s2, pos, tt):
    t = s1.shape[2]
    return pl.pallas_call(
        _topk_body,
        grid=(t // tt, P_HEADS),
        in_specs=[
            pl.BlockSpec((1, N_KEYS, tt), lambda i, h: (h, 0, i)),
            pl.BlockSpec((1, N_KEYS, tt), lambda i, h: (h, 0, i)),
            pl.BlockSpec((_NCAND, HD), lambda i, h: (0, 0)),
        ],
        out_specs=[
            pl.BlockSpec((P_TOPK, tt), lambda i, h: (h, i)),
            pl.BlockSpec((P_TOPK, tt), lambda i, h: (h, i)),
        ],
        out_shape=[
            jax.ShapeDtypeStruct((P_HEADS * P_TOPK, t), jnp.int32),
            jax.ShapeDtypeStruct((P_HEADS * P_TOPK, t), F32),
        ],
        scratch_shapes=[
            pltpu.VMEM((2, P_TOPK, tt), F32),
            pltpu.VMEM((2, P_TOPK, tt), F32),
            pltpu.VMEM((_NCAND, tt), F32),
            pltpu.VMEM((_NCAND, tt), F32),
            pltpu.VMEM((P_TOPK, tt), F32),
            pltpu.VMEM((P_TOPK, tt), F32),
        ],
        compiler_params=pltpu.CompilerParams(
            dimension_semantics=("parallel", "parallel"),
            vmem_limit_bytes=VMEM_LIMIT),
        name="peer_topk",
    )(s1, s2, pos)


PEER_TB = 128
PEER_G = 8
PEER_SETS = 4
DMA_QUEUES = 2
NSLOT = P_HEADS * P_TOPK
SG = SUBLANES
CC_AHEAD = 3
ROW_PITCH = 17
HID_UNROLL = 1
MIX_UNROLL = 2


def _gelu(x):
    return 0.5 * x * (1.0 + lax.erf(x * (2.0 ** -0.5)))


def _unrolled_loop(n, unroll, body):
    def trip(q, carry):
        for j in range(unroll):
            body(q * unroll + j, 0)
        return carry
    lax.fori_loop(0, n // unroll, trip, 0)


def _pack_body(u_ref, v_ref, o_ref):
    ub = lax.bitcast_convert_type(u_ref[...].astype(BF16).astype(F32), jnp.uint32) >> 16
    vb = lax.bitcast_convert_type(v_ref[...].astype(BF16).astype(F32), jnp.uint32)
    w = (vb & jnp.uint32(0xFFFF0000)) | ub
    nlc = w.shape[1] // HD
    for c in range(nlc):
        o_ref[pl.ds(c, w.shape[0], stride=nlc), :] = w[:, c * HD:(c + 1) * HD]


def _peer_pack(u_tab, v_tab, te):
    e, d = u_tab.shape
    return pl.pallas_call(
        _pack_body,
        grid=(e // te,),
        in_specs=[pl.BlockSpec((te, d), lambda i: (i, 0)),
                  pl.BlockSpec((te, d), lambda i: (i, 0))],
        out_specs=pl.BlockSpec((te * (d // HD), HD), lambda i: (i, 0)),
        out_shape=jax.ShapeDtypeStruct((e * (d // HD), HD), jnp.uint32),
        compiler_params=pltpu.CompilerParams(
            dimension_semantics=("parallel",),
            vmem_limit_bytes=VMEM_LIMIT),
        name="peer_pack",
    )(u_tab, v_tab)


def _peer_body(idx_ref, gt_ref, hn_ref, h_ref, fg_ref, w_hbm, o_ref,
               hid_scr, c_scr, y_scr, sem, *bufs, final, nsteps):
    tb, d = hn_ref.shape
    ngroups = tb // PEER_G
    nsg = NSLOT // SG
    ahead = PEER_SETS - 1
    nlc = d // HD
    step = pl.program_id(0)

    def issue_tok(tok, set_, r, s0, s1):
        ids = idx_ref.at[tok]
        dst = bufs[set_].at[pl.ds(r * NSLOT * ROW_PITCH, NSLOT * ROW_PITCH)]
        for s in range(s0, s1):
            e0 = pl.multiple_of(ids[s] * nlc, nlc)
            pltpu.make_async_copy(w_hbm.at[pl.ds(e0, nlc)], dst.at[pl.ds(s * ROW_PITCH, nlc)],
                                  sem.at[set_]).start(priority=s % DMA_QUEUES)

    def wait_set(set_):
        full = bufs[set_].at[pl.ds(0, PEER_G * NSLOT * nlc)]
        pltpu.make_async_copy(full, full, sem.at[set_]).wait()

    @pl.when(step == 0)
    def _():
        def first(r, carry):
            for a in range(ahead):
                issue_tok(a * PEER_G + r, a, r, 0, NSLOT)
            return carry
        lax.fori_loop(0, PEER_G, first, 0)

    per_tile = NSLOT // 2 // nsg
    lane = lax.broadcasted_iota(jnp.int32, (SG, tb), 1)
    hid_scr[...] = jnp.zeros_like(hid_scr)

    def group_body(g, set_):
        nxt = (set_ + ahead) % PEER_SETS
        wait_set(set_)


        def hid_tok(r, c2):
            t = g * PEER_G + r
            xrow = hn_ref[pl.ds(t, 1), :]
            xb = [jnp.broadcast_to(xrow[:, lc * HD:(lc + 1) * HD], (SG, HD)) for lc in range(nlc)]
            tiles = bufs[set_].at[pl.ds(r * NSLOT * ROW_PITCH, NSLOT * ROW_PITCH)]
            hs = []
            for sg in range(nsg):
                ws = [tiles[pl.ds(sg * SG * ROW_PITCH + lc, SG, stride=ROW_PITCH), :]
                      for lc in range(nlc)]
                issue_tok((g + ahead) * PEER_G + r, nxt, r, sg * per_tile, (sg + 1) * per_tile)
                acc = None
                for lc in range(nlc):
                    u = lax.bitcast_convert_type(ws[lc] << 16, F32)
                    term = u * xb[lc]
                    acc = term if acc is None else acc + term
                hs.append(jnp.sum(acc, axis=1, keepdims=True))
            for sg in range(nsg):
                pltpu.store(hid_scr.at[pl.ds(sg * SG, SG), :],
                            jnp.broadcast_to(hs[sg], (SG, tb)), mask=lane == t)
            return c2
        _unrolled_loop(PEER_G, HID_UNROLL, hid_tok)

        c_scr[...] = gt_ref[...] * _gelu(hid_scr[...])

        def mix_tok(r, c2):
            t = g * PEER_G + r
            tiles = bufs[set_].at[pl.ds(r * NSLOT * ROW_PITCH, NSLOT * ROW_PITCH)]
            def coef_col(sg):
                return jnp.sum(jnp.where(lane == t, c_scr[pl.ds(sg * SG, SG), :], 0.0),
                               axis=1, keepdims=True)
            cc = {sg: coef_col(sg) for sg in range(CC_AHEAD)}
            accs = [None] * nlc
            for sg in range(nsg):
                if sg + CC_AHEAD < nsg:
                    cc[sg + CC_AHEAD] = coef_col(sg + CC_AHEAD)
                ws = [tiles[pl.ds(sg * SG * ROW_PITCH + lc, SG, stride=ROW_PITCH), :]
                      for lc in range(nlc)]
                issue_tok((g + ahead) * PEER_G + r, nxt, r,
                          NSLOT // 2 + sg * per_tile, NSLOT // 2 + (sg + 1) * per_tile)
                for lc in range(nlc):
                    v = lax.bitcast_convert_type(ws[lc] & jnp.uint32(0xFFFF0000), F32)
                    term = v * cc[sg]
                    accs[lc] = term if accs[lc] is None else accs[lc] + term
            yrow = jnp.concatenate(
                [jnp.sum(a, axis=0, keepdims=True) for a in accs], axis=1)
            y_scr[pl.ds(t, 1), :] = yrow
            return c2
        _unrolled_loop(PEER_G, MIX_UNROLL, mix_tok)

    def group_round(q, carry):
        for j in range(PEER_SETS):
            group_body(q * PEER_SETS + j, j)
        return carry

    lax.fori_loop(0, ngroups // PEER_SETS, group_round, 0)

    @pl.when(step == nsteps - 1)
    def _():
        for a in range(ahead):
            wait_set((nsteps * ngroups + a) % PEER_SETS)

    hh = h_ref[...] + y_scr[...]
    if final:
        hh = hh * lax.rsqrt(jnp.mean(hh * hh, axis=-1, keepdims=True) + EPS) * fg_ref[...]
    o_ref[...] = hh


def _peer_mix(idx, gt, hn, h, fg, w_tab, final):
    t, d = hn.shape
    tb = PEER_TB
    nsteps = t // tb
    extra = (PEER_SETS - 1) * PEER_G
    assert (tb // PEER_G) % PEER_SETS == 0
    idx_pad = jnp.concatenate([idx, jnp.zeros((tb, NSLOT), idx.dtype)], axis=0)
    idx_ext = jnp.concatenate(
        [idx.reshape(nsteps, tb, NSLOT),
         idx_pad[tb:].reshape(nsteps, tb, NSLOT)[:, :extra]], axis=1)
    return pl.pallas_call(
        functools.partial(_peer_body, final=final, nsteps=nsteps),
        grid=(nsteps,),
        in_specs=[
            pl.BlockSpec((None, tb + extra, NSLOT), lambda i: (i, 0, 0),
                         memory_space=pltpu.SMEM),
            pl.BlockSpec((NSLOT, tb), lambda i: (0, i)),
            pl.BlockSpec((tb, d), lambda i: (i, 0)),
            pl.BlockSpec((tb, d), lambda i: (i, 0)),
            pl.BlockSpec((1, d), lambda i: (0, 0)),
            pl.BlockSpec(memory_space=pl.ANY),
        ],
        out_specs=pl.BlockSpec((tb, d), lambda i: (i, 0)),
        out_shape=jax.ShapeDtypeStruct((t, d), F32),
        scratch_shapes=[
            pltpu.VMEM((NSLOT, tb), F32),
            pltpu.VMEM((NSLOT, tb), F32),
            pltpu.VMEM((tb, d), F32),
            pltpu.SemaphoreType.DMA((PEER_SETS,)),
        ] + [pltpu.VMEM((PEER_G * NSLOT * ROW_PITCH, HD), jnp.uint32) for _ in range(PEER_SETS)],
        compiler_params=pltpu.CompilerParams(
            dimension_semantics=("arbitrary",),
            vmem_limit_bytes=VMEM_LIMIT),
        name="peer_mix",
    )(idx_ext, gt, hn, h, fg, w_tab)


def _tile(n, pref):
    return pref if n % pref == 0 else n


class _Tiles(NamedTuple):
    in_proj_rows: int
    in_proj_cols: int
    seq: int
    out_proj_rows: int
    topk_tokens: int
    pack_experts: int


def _plan_tiles(t, s, in_pad, n_experts):
    third = in_pad // 3 if (in_pad // HD) % 3 == 0 else in_pad
    return _Tiles(in_proj_rows=_tile(t, 512), in_proj_cols=third, seq=_tile(s, 256),
                  out_proj_rows=_tile(t, 256), topk_tokens=_tile(t, 512),
                  pack_experts=_tile(n_experts, 256))


def kernel(x, attn_norm_g, w_in, hgrn_lb_logits, hgrn_norm_g, gdn_conv_w, gdn_A_log, gdn_dt_bias,
           gdn_norm_g, w_out, ffn_norm_g, peer_w_query, peer_sub_keys, peer_u, peer_v,
           final_norm_g):
    b, s, d = x.shape
    t = b * s
    depth = w_in.shape[0]
    in_width = w_in.shape[2]
    n_small = 2 * B_V_HEADS
    n_main = in_width - n_small
    in_pad = n_main + HD
    lb_all = jnp.cumsum(jax.nn.softmax(hgrn_lb_logits.astype(F32), axis=0), axis=0)
    pos = jnp.array([i * P_TOPK + j for i, j in _PAIRS]
                    + [10 ** 6 + c for c in range(_NCAND - len(_PAIRS))], F32)
    pos = jnp.broadcast_to(pos[:, None], (_NCAND, HD))

    tiles = _plan_tiles(t, s, in_pad, peer_u.shape[1])

    h2 = x.reshape(t, d)
    for l in range(depth):
        w_l = jnp.pad(w_in[l], ((0, 0), (0, in_pad - in_width))).astype(BF16)
        proj = _in_proj(h2, attn_norm_g[l][None, :], w_l, tiles.in_proj_rows, tiles.in_proj_cols)
        proj3 = proj.reshape(b, s, in_pad)

        o_a = _hgrn(proj3, lb_all[l], hgrn_norm_g[l][None, :], tiles.seq)

        small = proj3[:, :, n_main:n_main + n_small]
        smt = small.reshape(b, s // CHUNK, CHUNK, n_small).transpose(0, 1, 3, 2)
        zeros8 = jnp.zeros((B_V_HEADS,), F32)
        a_neg = jnp.exp(gdn_A_log[l].astype(F32))
        dtb = gdn_dt_bias[l].astype(F32)
        pcol = jnp.zeros((2, HD), F32)
        pcol = pcol.at[0, B_V_HEADS:n_small].set(a_neg).at[1, B_V_HEADS:n_small].set(dtb)
        prow = jnp.zeros((n_small, HD), F32)
        prow = prow.at[:, 0].set(jnp.concatenate([zeros8, a_neg]))
        prow = prow.at[:, 1].set(jnp.concatenate([zeros8, dtb]))
        o_b = _gdn(proj3, smt, gdn_conv_w[l], pcol, prow, gdn_norm_g[l][None, :], tiles.seq)

        wq = peer_w_query[l].reshape(d, P_HEADS * 2 * HD).astype(BF16)
        h2, hn, s1, s2 = _out_proj(
            h2, o_a.reshape(t, -1), o_b.reshape(t, -1), w_out[l].astype(BF16),
            ffn_norm_g[l][None, :], wq, peer_sub_keys[l, 0], peer_sub_keys[l, 1],
            tiles.out_proj_rows)

        e_t, g_t = _peer_topk(s1, s2, pos, tiles.topk_tokens)
        w_tab = _peer_pack(peer_u[l], peer_v[l], tiles.pack_experts)
        h2 = _peer_mix(e_t.T, g_t, hn, h2, final_norm_g[None, :], w_tab, l == depth - 1)
    return h2.reshape(b, s, d)
```
